```python
import jax, jax.numpy as jnp
from jax import lax
import numpy as np

D_MODEL = 1024
BATCH = 8
SEQ = 4096
DEPTH = 2

N_A_LAYERS = DEPTH // 2
N_B_LAYERS = DEPTH - N_A_LAYERS

HEAD_DIM = 128
A_HEADS = 6
A_WIDTH = A_HEADS * HEAD_DIM
CHUNK = 64

B_HEADS = 6
B_WIDTH = B_HEADS * HEAD_DIM
DILATED_GROUPS = ((128, 1), (512, 4), (2048, 16))
N_GROUPS = len(DILATED_GROUPS)
ROPE_THETA = 10000.0

MEM_TOKENS = 256
MEM_HEADS = 4
MEM_HEAD_DIM = 64
MEM_WIDTH = MEM_HEADS * MEM_HEAD_DIM

MIX_WIDTH = A_WIDTH + MEM_WIDTH
A_COLS = 4 * A_WIDTH + MEM_WIDTH
B_COLS = N_GROUPS * B_WIDTH + MEM_WIDTH
FFN_HIDDEN = ((-(-8 * D_MODEL // 3)) + 255) // 256 * 256
EPS = 1e-6

kernel_name = "yoco_hgrn2_dilated_attn_memory_hybrid"


def rms_norm(x, g):
    xf = x.astype(jnp.float32)
    y = xf * lax.rsqrt(jnp.mean(xf * xf, axis=-1, keepdims=True) + EPS)
    return (y * g.astype(jnp.float32)).astype(x.dtype)


def rope(x, pos):
    dh = x.shape[-1]
    half = dh // 2
    inv = ROPE_THETA ** (-jnp.arange(half, dtype=jnp.float32) / half)
    ang = pos.astype(jnp.float32)[:, None] * inv[None, :]
    cos = jnp.cos(ang)[None, :, None, :]
    sin = jnp.sin(ang)[None, :, None, :]
    xf = x.astype(jnp.float32)
    x1, x2 = xf[..., :half], xf[..., half:]
    out = jnp.concatenate([x1 * cos - x2 * sin, x2 * cos + x1 * sin], axis=-1)
    return out.astype(x.dtype)


def hgrn2_chunkwise(q, f_logit, i, lb):
    Bn, S, H, dk = q.shape
    nC = S // CHUNK
    f = lb + (1.0 - lb) * jax.nn.sigmoid(f_logit.astype(jnp.float32))
    k = 1.0 - f
    logf = jnp.log(f)
    qf = jax.nn.silu(q.astype(jnp.float32))
    vf = i.astype(jnp.float32)

    def chunks(t):
        return t.reshape(Bn, nC, CHUNK, H, t.shape[-1]).transpose(1, 0, 3, 2, 4)

    qc, kc, vc, gc = chunks(qf), chunks(k), chunks(vf), chunks(logf)
    b = jnp.cumsum(gc, axis=3)
    b_end = b[:, :, :, -1:, :]
    q_in = qc * jnp.exp(b)
    k_in = kc * jnp.exp(-b)
    k_out = kc * jnp.exp(b_end - b)
    causal = jnp.tril(jnp.ones((CHUNK, CHUNK), dtype=bool))
    att = jnp.where(causal, jnp.einsum('nbhqd,nbhkd->nbhqk', q_in, k_in), 0.0)
    o_intra = jnp.einsum('nbhqk,nbhke->nbhqe', att, vc)
    decay = jnp.exp(b_end[:, :, :, 0, :])

    def step(state, xs):
        q_n, k_n, v_n, dec = xs
        o_n = jnp.einsum('bhqd,bhde->bhqe', q_n, state)
        state = dec[..., None] * state + jnp.einsum('bhkd,bhke->bhde', k_n, v_n)
        return state, o_n

    s0 = jnp.zeros((Bn, H, dk, vf.shape[-1]), jnp.float32)
    _, o_inter = lax.scan(step, s0, (q_in, k_out, vc, decay))
    o = o_intra + o_inter
    return o.transpose(1, 0, 3, 2, 4).reshape(Bn, S, H, vf.shape[-1])


def dilated_branch(q, k, v, window, dilation):
    Bn, S, H, dh = q.shape
    span = window // dilation
    blk = span
    L = S // dilation
    nb = -(-L // blk)
    Lp = nb * blk

    def by_residue(t):
        t = t.reshape(Bn, L, dilation, H, dh).transpose(0, 2, 3, 1, 4)
        return jnp.pad(t, ((0, 0), (0, 0), (0, 0), (0, Lp - L), (0, 0)))

    def band(t):
        tp = jnp.pad(t, ((0, 0), (0, 0), (0, 0), (blk, 0), (0, 0)))
        prev = tp[:, :, :, :Lp].reshape(Bn, dilation, H, nb, blk, dh)
        cur = tp[:, :, :, blk:].reshape(Bn, dilation, H, nb, blk, dh)
        return jnp.concatenate([prev, cur], axis=4)

    qb = by_residue(q).reshape(Bn, dilation, H, nb, blk, dh)
    kb = band(by_residue(k))
    vb = band(by_residue(v))
    s = jnp.einsum('brhnqd,brhnkd->brhnqk', qb, kb,
                   preferred_element_type=jnp.float32) * (dh ** -0.5)
    qpos = jnp.arange(nb)[:, None, None] * blk + jnp.arange(blk)[None, :, None]
    kpos = (jnp.arange(nb)[:, None, None] - 1) * blk + jnp.arange(2 * blk)[None, None, :]
    dist = qpos - kpos
    mask = (dist >= 0) & (dist <= span) & (kpos >= 0)
    s = jnp.where(mask, s, -jnp.inf)
    lse = jax.nn.logsumexp(s, axis=-1)
    p = jnp.exp(s - lse[..., None])
    o = jnp.einsum('brhnqk,brhnkd->brhnqd', p.astype(v.dtype), vb)
    o = o.reshape(Bn, dilation, H, Lp, dh)[:, :, :, :L].transpose(0, 3, 1, 2, 4).reshape(Bn, S, H, dh)
    lse = lse.reshape(Bn, dilation, H, Lp)[..., :L].transpose(0, 3, 1, 2).reshape(Bn, S, H)
    return o, lse


def memory_attention(q, mk, mv):
    s = jnp.einsum('bshd,bmhd->bhsm', q, mk,
                   preferred_element_type=jnp.float32) * (q.shape[-1] ** -0.5)
    p = jax.nn.softmax(s, axis=-1)
    return jnp.einsum('bhsm,bmhd->bshd', p.astype(mv.dtype), mv)


def setup_inputs(seed: int = 0) -> dict:
    key = jax.random.key(seed)
    ks = jax.random.split(key, 24)

    def w(k, shape, fan_in):
        return jax.random.normal(k, shape, jnp.float32) * (fan_in ** -0.5)

    def gain(k, shape):
        return 1.0 + 0.02 * jax.random.normal(k, shape, jnp.float32)

    return {
        "x": jax.random.normal(ks[0], (BATCH, SEQ, D_MODEL), jnp.float32),
        "mem": jax.random.normal(ks[1], (BATCH, MEM_TOKENS, D_MODEL), jnp.float32),
        "norm_mix": gain(ks[2], (DEPTH, D_MODEL)),
        "norm_ffn": gain(ks[3], (DEPTH, D_MODEL)),
        "a_w_in": w(ks[4], (N_A_LAYERS, D_MODEL, A_COLS), D_MODEL),
        "a_lb_logits": 0.1 * jax.random.normal(ks[5], (N_A_LAYERS + 1, A_WIDTH), jnp.float32),
        "a_onorm": gain(ks[6], (N_A_LAYERS, A_WIDTH)),
        "b_w_in": w(ks[7], (N_B_LAYERS, D_MODEL, B_COLS), D_MODEL),
        "b_qnorm": gain(ks[8], (N_B_LAYERS, N_GROUPS, HEAD_DIM)),
        "kv_norm": gain(ks[9], (D_MODEL,)),
        "w_kv": w(ks[10], (D_MODEL, 2 * B_WIDTH), D_MODEL),
        "b_knorm": gain(ks[11], (HEAD_DIM,)),
        "mem_norm": gain(ks[12], (DEPTH, D_MODEL)),
        "w_mem_kv": w(ks[13], (DEPTH, D_MODEL, 2 * MEM_WIDTH), D_MODEL),
        "mem_qnorm": gain(ks[14], (DEPTH, MEM_HEAD_DIM)),
        "mem_knorm": gain(ks[15], (DEPTH, MEM_HEAD_DIM)),
        "w_out": w(ks[16], (DEPTH, MIX_WIDTH, D_MODEL), MIX_WIDTH),
        "w_gate_up": w(ks[17], (DEPTH, D_MODEL, 2 * FFN_HIDDEN), D_MODEL),
        "w_down": w(ks[18], (DEPTH, FFN_HIDDEN, D_MODEL), FFN_HIDDEN),
    }


def reference(x, mem, norm_mix, norm_ffn, a_w_in, a_lb_logits, a_onorm, b_w_in, b_qnorm,
              kv_norm, w_kv, b_knorm, mem_norm, w_mem_kv, mem_qnorm, mem_knorm,
              w_out, w_gate_up, w_down):
    Bn, S, _ = x.shape
    pos = jnp.arange(S)
    lb_all = jnp.cumsum(jax.nn.softmax(a_lb_logits.astype(jnp.float32), axis=0), axis=0)
    h = x
    k_sh = None
    v_sh = None
    for l in range(DEPTH):
        xn = rms_norm(h, norm_mix[l])
        mn = rms_norm(mem, mem_norm[l])
        mk, mv = jnp.split(mn @ w_mem_kv[l], 2, axis=-1)
        mk = rms_norm(mk.reshape(Bn, MEM_TOKENS, MEM_HEADS, MEM_HEAD_DIM), mem_knorm[l])
        mv = mv.reshape(Bn, MEM_TOKENS, MEM_HEADS, MEM_HEAD_DIM)
        if l < N_A_LAYERS:
            proj = xn @ a_w_in[l]
            q, f, i, g, mq = jnp.split(proj, [A_WIDTH, 2 * A_WIDTH, 3 * A_WIDTH, 4 * A_WIDTH], axis=-1)
            shp = (Bn, S, A_HEADS, HEAD_DIM)
            o = hgrn2_chunkwise(q.reshape(shp), f.reshape(shp), i.reshape(shp),
                                lb_all[l].reshape(A_HEADS, HEAD_DIM))
            o = rms_norm(o, a_onorm[l].reshape(A_HEADS, HEAD_DIM)) * jax.nn.silu(g.reshape(shp).astype(jnp.float32))
            mix_main = o.reshape(Bn, S, A_WIDTH).astype(h.dtype)
        else:
            j = l - N_A_LAYERS
            proj = xn @ b_w_in[j]
            qs = proj[..., :N_GROUPS * B_WIDTH].reshape(Bn, S, N_GROUPS, B_HEADS, HEAD_DIM)
            mq = proj[..., N_GROUPS * B_WIDTH:]
            outs = []
            lses = []
            for gi, (win, dil) in enumerate(DILATED_GROUPS):
                qg = rope(rms_norm(qs[:, :, gi], b_qnorm[j, gi]), pos)
                o_g, lse_g = dilated_branch(qg, k_sh, v_sh, win, dil)
                outs.append(o_g)
                lses.append(lse_g)
            alpha = jax.nn.softmax(jnp.stack(lses, axis=0), axis=0)
            o = jnp.sum(alpha[..., None] * jnp.stack(outs, axis=0).astype(jnp.float32), axis=0)
            mix_main = o.reshape(Bn, S, B_WIDTH).astype(h.dtype)
        mq = rms_norm(mq.reshape(Bn, S, MEM_HEADS, MEM_HEAD_DIM), mem_qnorm[l])
        mo = memory_attention(mq, mk, mv).reshape(Bn, S, MEM_WIDTH)
        h = h + jnp.concatenate([mix_main, mo.astype(h.dtype)], axis=-1) @ w_out[l]
        gt, up = jnp.split(rms_norm(h, norm_ffn[l]) @ w_gate_up[l], 2, axis=-1)
        h = h + (jax.nn.silu(gt) * up) @ w_down[l]
        if l == N_A_LAYERS - 1:
            k_sh, v_sh = jnp.split(rms_norm(h, kv_norm) @ w_kv, 2, axis=-1)
            k_sh = rope(rms_norm(k_sh.reshape(Bn, S, B_HEADS, HEAD_DIM), b_knorm), pos)
            v_sh = v_sh.reshape(Bn, S, B_HEADS, HEAD_DIM)
    return h
```

```python
import functools

import jax
import jax.numpy as jnp
from jax import lax
from jax.experimental import pallas as pl
from jax.experimental.pallas import tpu as pltpu

F32 = jnp.float32
BF16 = jnp.bfloat16

EPS = 1e-6
HEAD_DIM = 128
CHUNK = 64
MEM_HEAD_DIM = 64
DILATED_GROUPS = ((128, 1), (512, 4), (2048, 16))
ROPE_THETA = 10000.0

V7X_LANES = 128
V7X_VMEM_SCOPED_MAX_BYTES = 60000 * 1024

ROW_TILE = 256
FFN_CHUNK = 2816
HGRN_ROWS = 512
ATTN_BLOCK = 128
MERGE_ROWS = 512

_NT = (((1,), (1,)), ((), ()))
_TN = (((0,), (0,)), ((), ()))


def _vmem_limit(pipelined_bytes, resident_bytes, temp_bytes):
    need = 2 * pipelined_bytes + resident_bytes + temp_bytes
    return int(min(max(need, 16 * 1024 * 1024), V7X_VMEM_SCOPED_MAX_BYTES))


def _nbytes(shape, dtype):
    n = 1
    for s in shape:
        n *= s
    return n * jnp.dtype(dtype).itemsize


def _resident(shape):
    zeros = (0,) * len(shape)
    return pl.BlockSpec(shape, lambda *_: zeros, pipeline_mode=pl.Buffered(1))


def _dot(a, b):
    return jnp.dot(a, b, preferred_element_type=F32)


def _rms(x, gain):
    ms = jnp.mean(x * x, axis=-1, keepdims=True)
    return x * lax.rsqrt(ms + EPS) * gain


def _silu(x):
    return x * jax.nn.sigmoid(x)


def _rms_head_pairs(x, gain):
    lo = lax.broadcasted_iota(jnp.int32, x.shape, 1) < MEM_HEAD_DIM
    x2 = x * x
    s_lo = jnp.sum(jnp.where(lo, x2, 0.0), axis=-1, keepdims=True)
    s_hi = jnp.sum(jnp.where(lo, 0.0, x2), axis=-1, keepdims=True)
    ms = jnp.where(lo, s_lo, s_hi) * (1.0 / MEM_HEAD_DIM)
    return x * lax.rsqrt(ms + EPS) * gain


def _rope(x, cos2, sin2):
    return x * cos2 + pltpu.roll(x, HEAD_DIM // 2, axis=1) * sin2


def _memory_attention(mq, qgain, mk_ref, mv_ref, mo_ref):
    scale = MEM_HEAD_DIM ** -0.5
    for t in range(mq.shape[1] // V7X_LANES):
        cols = slice(t * V7X_LANES, (t + 1) * V7X_LANES)
        qn = _rms_head_pairs(mq[:, cols], qgain)
        lo = lax.broadcasted_iota(jnp.int32, qn.shape, 1) < MEM_HEAD_DIM
        mk_t = mk_ref[:, cols]
        mv_t = mv_ref[:, cols]
        outs = []
        for keep in (lo, jnp.logical_not(lo)):
            qh = jnp.where(keep, qn, 0.0).astype(BF16)
            s = lax.dot_general(qh, mk_t, _NT, preferred_element_type=F32) * scale
            p = jnp.exp(s - jnp.max(s, axis=-1, keepdims=True))
            denom = jnp.sum(p, axis=-1, keepdims=True)
            outs.append(_dot(p.astype(BF16), mv_t) / denom)
        mo_ref[:, cols] = jnp.where(lo, outs[0], outs[1]).astype(mo_ref.dtype)


def _mem_kv_kernel(mem_ref, gain_ref, w_ref, kgain_ref, mk_ref, mv_ref):
    mw = mk_ref.shape[1]
    mn = _rms(mem_ref[...], gain_ref[...]).astype(BF16)
    kv = _dot(mn, w_ref[...])
    for t in range(mw // V7X_LANES):
        cols = slice(t * V7X_LANES, (t + 1) * V7X_LANES)
        mk_ref[:, cols] = _rms_head_pairs(kv[:, cols], kgain_ref[...]).astype(mk_ref.dtype)
    mv_ref[...] = kv[:, mw:].astype(mv_ref.dtype)


def _mem_kv(mem, mem_norm, w_mem_kv, mem_knorm):
    bn, mt, dm = mem.shape
    depth = w_mem_kv.shape[0]
    mw = w_mem_kv.shape[2] // 2
    kgain = jnp.concatenate([mem_knorm, mem_knorm], axis=-1).reshape(depth, 1, V7X_LANES)
    out = jax.ShapeDtypeStruct((depth, bn, mt, mw), BF16)
    return pl.pallas_call(
        _mem_kv_kernel,
        out_shape=(out, out),
        grid=(depth, bn),
        in_specs=[
            pl.BlockSpec((None, mt, dm), lambda l, b: (b, 0, 0)),
            pl.BlockSpec((None, 1, dm), lambda l, b: (l, 0, 0)),
            pl.BlockSpec((None, dm, 2 * mw), lambda l, b: (l, 0, 0)),
            pl.BlockSpec((None, 1, V7X_LANES), lambda l, b: (l, 0, 0)),
        ],
        out_specs=(
            pl.BlockSpec((None, None, mt, mw), lambda l, b: (l, b, 0, 0)),
            pl.BlockSpec((None, None, mt, mw), lambda l, b: (l, b, 0, 0)),
        ),
        name="mem_kv",
    )(mem, mem_norm.reshape(depth, 1, dm), w_mem_kv, kgain)


def _inproj_a_kernel(x_ref, gain_ref, w_ref, mqg_ref, mk_ref, mv_ref,
                     q_ref, f_ref, i_ref, g_ref, mo_ref):
    aw = q_ref.shape[1]
    xn = _rms(x_ref[...], gain_ref[...]).astype(BF16)
    q_ref[...] = _dot(xn, w_ref[:, 0:aw]).astype(q_ref.dtype)
    f_ref[...] = _dot(xn, w_ref[:, aw:2 * aw])
    i_ref[...] = _dot(xn, w_ref[:, 2 * aw:3 * aw]).astype(i_ref.dtype)
    g_ref[...] = _dot(xn, w_ref[:, 3 * aw:4 * aw]).astype(g_ref.dtype)
    mq = _dot(xn, w_ref[:, 4 * aw:])
    _memory_attention(mq, mqg_ref[...], mk_ref, mv_ref, mo_ref)


def _inproj_a(h, gain, w, mq_gain, mk, mv, layer, seq):
    t, dm = h.shape
    mt, mw = mk.shape[2], mk.shape[3]
    aw = (w.shape[1] - mw) // 4
    tiles_per_seq = seq // ROW_TILE
    row = lambda i: (i, 0)
    mem = lambda i: (layer, i // tiles_per_seq, 0, 0)
    wide = functools.partial(jax.ShapeDtypeStruct, (t, aw))
    pipelined = (_nbytes((ROW_TILE, dm), F32) + _nbytes((ROW_TILE, aw), F32)
                 + 3 * _nbytes((ROW_TILE, aw), BF16) + _nbytes((ROW_TILE, mw), BF16)
                 + 2 * _nbytes((mt, mw), BF16))
    return pl.pallas_call(
        _inproj_a_kernel,
        out_shape=(wide(BF16), wide(F32), wide(BF16), wide(BF16), jax.ShapeDtypeStruct((t, mw), BF16)),
        grid=(t // ROW_TILE,),
        in_specs=[
            pl.BlockSpec((ROW_TILE, dm), row),
            _resident((1, dm)),
            _resident(w.shape),
            _resident((1, V7X_LANES)),
            pl.BlockSpec((None, None, mt, mw), mem),
            pl.BlockSpec((None, None, mt, mw), mem),
        ],
        out_specs=(
            pl.BlockSpec((ROW_TILE, aw), row), pl.BlockSpec((ROW_TILE, aw), row),
            pl.BlockSpec((ROW_TILE, aw), row), pl.BlockSpec((ROW_TILE, aw), row),
            pl.BlockSpec((ROW_TILE, mw), row),
        ),
        compiler_params=pltpu.CompilerParams(
            dimension_semantics=("parallel",),
            vmem_limit_bytes=_vmem_limit(pipelined, _nbytes(w.shape, BF16),
                                         4 * _nbytes((ROW_TILE, aw), F32))),
        name="inproj_a",
    )(h, gain, w, mq_gain, mk, mv)


def _hgrn2_kernel(lbl_ref, on_ref, q_ref, f_ref, i_ref, g_ref, o_ref, state_ref, *, layer):
    @pl.when(pl.program_id(2) == 0)
    def _():
        state_ref[...] = jnp.zeros_like(state_ref)

    lg = lbl_ref[...]
    e = jnp.exp(lg - jnp.max(lg, axis=0, keepdims=True))
    lb = jnp.sum(e[:layer + 1], axis=0, keepdims=True) / jnp.sum(e, axis=0, keepdims=True)

    row = lax.broadcasted_iota(jnp.int32, (CHUNK, HEAD_DIM), 0)
    causal = (lax.broadcasted_iota(jnp.int32, (CHUNK, CHUNK), 0)
              >= lax.broadcasted_iota(jnp.int32, (CHUNK, CHUNK), 1))
    onorm = on_ref[...]

    def chunk(c, state_t):
        rows = pl.ds(pl.multiple_of(c * CHUNK, CHUNK), CHUNK)
        f = lb + (1.0 - lb) * jax.nn.sigmoid(f_ref[rows, :])
        k = 1.0 - f
        b = jnp.log(f)
        shift = 1
        while shift < CHUNK:
            b = b + jnp.where(row >= shift, pltpu.roll(b, shift, axis=0), 0.0)
            shift *= 2
        b_end = b[CHUNK - 1:CHUNK, :]
        q_in = (_silu(q_ref[rows, :].astype(F32)) * jnp.exp(b)).astype(BF16)
        k_in = (k * jnp.exp(-b)).astype(BF16)
        k_out = (k * jnp.exp(b_end - b)).astype(BF16)
        v = i_ref[rows, :]
        att = lax.dot_general(q_in, k_in, _NT, preferred_element_type=F32)
        att = jnp.where(causal, att, 0.0).astype(BF16)
        o = _dot(att, v) + lax.dot_general(q_in, state_t.astype(BF16), _NT, preferred_element_type=F32)
        gate = _silu(g_ref[rows, :].astype(F32))
        o_ref[rows, :] = (_rms(o, onorm) * gate).astype(o_ref.dtype)
        return state_t * jnp.exp(b_end) + lax.dot_general(v, k_out, _TN, preferred_element_type=F32)

    state_ref[...] = lax.fori_loop(0, q_ref.shape[0] // CHUNK, chunk, state_ref[...])


def _hgrn2(q, f, i, g, lb_logits, onorm, layer, bn, seq):
    t, aw = q.shape
    heads = aw // HEAD_DIM
    steps = seq // HGRN_ROWS
    blk = lambda b, h, s: (b * steps + s, h)
    spec = pl.BlockSpec((HGRN_ROWS, HEAD_DIM), blk)
    return pl.pallas_call(
        functools.partial(_hgrn2_kernel, layer=layer),
        out_shape=jax.ShapeDtypeStruct((t, aw), BF16),
        grid=(bn, heads, steps),
        in_specs=[
            pl.BlockSpec((lb_logits.shape[0], HEAD_DIM), lambda b, h, s: (0, h)),
            pl.BlockSpec((1, HEAD_DIM), lambda b, h, s: (0, h)),
            spec, spec, spec, spec,
        ],
        out_specs=spec,
        scratch_shapes=[pltpu.VMEM((HEAD_DIM, HEAD_DIM), F32)],
        compiler_params=pltpu.CompilerParams(
            dimension_semantics=("parallel", "parallel", "arbitrary")),
        name="hgrn2",
    )(lb_logits, onorm, q, f, i, g)


def _mix_ffn_kernel(*refs, with_kv):
    if with_kv:
        (h_ref, o_ref, mo_ref, wo_ref, nf_ref, wgu_ref, wd_ref,
         kvn_ref, wkv_ref, kn_ref, cos_ref, sin_ref, h_out, k_out, v_out) = refs
    else:
        h_ref, o_ref, mo_ref, wo_ref, nf_ref, wgu_ref, wd_ref, h_out = refs
    main_w = o_ref.shape[1]
    hidden = wd_ref.shape[0]
    h = h_ref[...] + _dot(o_ref[...], wo_ref[0:main_w, :]) + _dot(mo_ref[...], wo_ref[main_w:, :])
    hn = _rms(h, nf_ref[...]).astype(BF16)
    for c in range(hidden // FFN_CHUNK):
        cols = slice(c * FFN_CHUNK, (c + 1) * FFN_CHUNK)
        up_cols = slice(hidden + c * FFN_CHUNK, hidden + (c + 1) * FFN_CHUNK)
        act = (_silu(_dot(hn, wgu_ref[:, cols])) * _dot(hn, wgu_ref[:, up_cols])).astype(BF16)
        h = h + _dot(act, wd_ref[cols, :])
    h_out[...] = h
    if with_kv:
        kv_w = k_out.shape[1]
        kn = _rms(h, kvn_ref[...]).astype(BF16)
        k = _dot(kn, wkv_ref[:, 0:kv_w])
        for hd in range(kv_w // HEAD_DIM):
            cols = slice(hd * HEAD_DIM, (hd + 1) * HEAD_DIM)
            k_out[:, cols] = _rope(_rms(k[:, cols], kn_ref[...]), cos_ref[...], sin_ref[...])
        v_out[...] = _dot(kn, wkv_ref[:, kv_w:])


def _mix_ffn(h, o, mo, w_out, norm_ffn, w_gate_up, w_down, seq, kv=None):
    t, dm = h.shape
    main_w, mw = o.shape[1], mo.shape[1]
    hidden = w_down.shape[0]
    row = lambda i: (i, 0)
    in_specs = [
        pl.BlockSpec((ROW_TILE, dm), row), pl.BlockSpec((ROW_TILE, main_w), row),
        pl.BlockSpec((ROW_TILE, mw), row),
        _resident(w_out.shape), _resident((1, dm)), _resident(w_gate_up.shape), _resident(w_down.shape),
    ]
    args = [h, o, mo, w_out, norm_ffn, w_gate_up, w_down]
    out_shape = [jax.ShapeDtypeStruct((t, dm), F32)]
    out_specs = [pl.BlockSpec((ROW_TILE, dm), row)]
    resident = _nbytes(w_out.shape, BF16) + _nbytes(w_gate_up.shape, BF16) + _nbytes(w_down.shape, BF16)
    pipelined = 2 * _nbytes((ROW_TILE, dm), F32) + _nbytes((ROW_TILE, main_w + mw), BF16)
    if kv is not None:
        kv_norm, w_kv, k_norm, cos2, sin2 = kv
        kv_w = w_kv.shape[1] // 2
        tiles_per_seq = seq // ROW_TILE
        pos = lambda i: (i % tiles_per_seq, 0)
        in_specs += [_resident((1, dm)), _resident(w_kv.shape), _resident((1, HEAD_DIM)),
                     pl.BlockSpec((ROW_TILE, HEAD_DIM), pos), pl.BlockSpec((ROW_TILE, HEAD_DIM), pos)]
        args += [kv_norm, w_kv, k_norm, cos2, sin2]
        out_shape += [jax.ShapeDtypeStruct((t, kv_w), F32)] * 2
        out_specs += [pl.BlockSpec((ROW_TILE, kv_w), row)] * 2
        resident += _nbytes(w_kv.shape, BF16)
        pipelined += 2 * _nbytes((ROW_TILE, kv_w), F32) + 2 * _nbytes((ROW_TILE, HEAD_DIM), F32)
    temps = 3 * _nbytes((ROW_TILE, dm), F32) + 3 * _nbytes((ROW_TILE, FFN_CHUNK), F32)
    return pl.pallas_call(
        functools.partial(_mix_ffn_kernel, with_kv=kv is not None),
        out_shape=tuple(out_shape),
        grid=(t // ROW_TILE,),
        in_specs=in_specs,
        out_specs=tuple(out_specs),
        compiler_params=pltpu.CompilerParams(
            dimension_semantics=("parallel",),
            vmem_limit_bytes=_vmem_limit(pipelined, resident, temps)),
        name="mix_ffn_kv" if kv is not None else "mix_ffn",
    )(*args)


def _inproj_b_kernel(x_ref, gain_ref, w_ref, qn_ref, cos_ref, sin_ref, mqg_ref, mk_ref, mv_ref,
                     *out_refs):
    q_refs, mo_ref = out_refs[:-1], out_refs[-1]
    bw = q_refs[0].shape[1]
    xn = _rms(x_ref[...], gain_ref[...]).astype(BF16)
    for gi, q_ref in enumerate(q_refs):
        qs = _dot(xn, w_ref[:, gi * bw:(gi + 1) * bw])
        for hd in range(bw // HEAD_DIM):
            cols = slice(hd * HEAD_DIM, (hd + 1) * HEAD_DIM)
            q_ref[:, cols] = _rope(_rms(qs[:, cols], qn_ref[gi]), cos_ref[...], sin_ref[...])
    mq = _dot(xn, w_ref[:, len(q_refs) * bw:])
    _memory_attention(mq, mqg_ref[...], mk_ref, mv_ref, mo_ref)


def _inproj_b(h, gain, w, q_norm, cos2, sin2, mq_gain, mk, mv, layer, seq):
    t, dm = h.shape
    mt, mw = mk.shape[2], mk.shape[3]
    n_groups = q_norm.shape[0]
    bw = (w.shape[1] - mw) // n_groups
    tiles_per_seq = seq // ROW_TILE
    row = lambda i: (i, 0)
    pos = lambda i: (i % tiles_per_seq, 0)
    mem = lambda i: (layer, i // tiles_per_seq, 0, 0)
    pipelined = (_nbytes((ROW_TILE, dm), F32) + n_groups * _nbytes((ROW_TILE, bw), F32)
                 + 2 * _nbytes((ROW_TILE, HEAD_DIM), F32) + _nbytes((ROW_TILE, mw), BF16)
                 + 2 * _nbytes((mt, mw), BF16))
    return pl.pallas_call(
        _inproj_b_kernel,
        out_shape=tuple([jax.ShapeDtypeStruct((t, bw), F32)] * n_groups
                        + [jax.ShapeDtypeStruct((t, mw), BF16)]),
        grid=(t // ROW_TILE,),
        in_specs=[
            pl.BlockSpec((ROW_TILE, dm), row),
            _resident((1, dm)),
            _resident(w.shape),
            _resident((n_groups, 1, HEAD_DIM)),
            pl.BlockSpec((ROW_TILE, HEAD_DIM), pos), pl.BlockSpec((ROW_TILE, HEAD_DIM), pos),
            _resident((1, V7X_LANES)),
            pl.BlockSpec((None, None, mt, mw), mem),
            pl.BlockSpec((None, None, mt, mw), mem),
        ],
        out_specs=tuple([pl.BlockSpec((ROW_TILE, bw), row)] * n_groups
                        + [pl.BlockSpec((ROW_TILE, mw), row)]),
        compiler_params=pltpu.CompilerParams(
            dimension_semantics=("parallel",),
            vmem_limit_bytes=_vmem_limit(pipelined, _nbytes(w.shape, BF16),
                                         4 * _nbytes((ROW_TILE, bw), F32))),
        name="inproj_b",
    )(h, gain, w, q_norm.reshape(n_groups, 1, HEAD_DIM), cos2, sin2, mq_gain, mk, mv)


def _strided_rows(start, size, stride):
    return pl.ds(start, size) if stride == 1 else pl.ds(start, size, stride=stride)


def _dilated_kernel(*refs, dilations):
    n_groups = len(dilations)
    q_refs = refs[:n_groups]
    k_ref, v_ref, o_ref, og_ref, lse_ref, bias_ref = refs[n_groups:]
    seq = k_ref.shape[0]
    blk = ATTN_BLOCK
    scale = HEAD_DIM ** -0.5

    qi = lax.broadcasted_iota(jnp.int32, (blk, 2 * blk), 0)
    kj = lax.broadcasted_iota(jnp.int32, (blk, 2 * blk), 1)
    band = (kj >= qi) & (kj <= qi + blk)
    bias_ref[0] = jnp.where(band & (kj >= blk), 0.0, -jnp.inf)
    bias_ref[1] = jnp.where(band, 0.0, -jnp.inf)

    for gi, dil in enumerate(dilations):
        q_ref = q_refs[gi]
        n_blocks = seq // (blk * dil)
        span = blk * dil

        def block(idx, carry, q_ref=q_ref, gi=gi, dil=dil, n_blocks=n_blocks, span=span):
            n = lax.rem(idx, n_blocks)
            start = lax.div(idx, n_blocks) + n * span
            prev = jnp.maximum(start - span, 0)
            cur_rows = _strided_rows(start, blk, dil)
            prev_rows = _strided_rows(prev, blk, dil)
            q = q_ref[cur_rows, :].astype(BF16)
            kw = jnp.concatenate([k_ref[prev_rows, :], k_ref[cur_rows, :]], axis=0).astype(BF16)
            vw = jnp.concatenate([v_ref[prev_rows, :], v_ref[cur_rows, :]], axis=0).astype(BF16)
            s = lax.dot_general(q, kw, _NT, preferred_element_type=F32) * scale
            s = s + bias_ref[jnp.minimum(n, 1)]
            m = jnp.max(s, axis=-1, keepdims=True)
            p = jnp.exp(s - m)
            denom = jnp.sum(p, axis=-1, keepdims=True)
            og_ref[gi, cur_rows, :] = _dot(p.astype(BF16), vw) / denom
            lse_ref[gi, cur_rows, :] = jnp.broadcast_to(m + jnp.log(denom), (blk, HEAD_DIM))
            return carry

        lax.fori_loop(0, dil * n_blocks, block, 0)

    def merge(step, carry):
        rows = pl.ds(pl.multiple_of(step * MERGE_ROWS, MERGE_ROWS), MERGE_ROWS)
        lses = [lse_ref[gi, rows, :] for gi in range(n_groups)]
        top = functools.reduce(jnp.maximum, lses)
        ws = [jnp.exp(l - top) for l in lses]
        acc = sum(w * og_ref[gi, rows, :] for gi, w in enumerate(ws))
        o_ref[rows, :] = (acc / sum(ws)).astype(o_ref.dtype)
        return carry

    lax.fori_loop(0, seq // MERGE_ROWS, merge, 0)


def _dilated_attention(qs, k, v, bn, seq):
    t, width = k.shape
    heads = width // HEAD_DIM
    dilations = tuple(d for _, d in DILATED_GROUPS)
    n_groups = len(dilations)
    as_seq = lambda a: a.reshape(bn, seq, width)
    spec = pl.BlockSpec((None, seq, HEAD_DIM), lambda b, h: (b, 0, h))
    seq_bytes = _nbytes((seq, HEAD_DIM), F32)
    scratch = 2 * n_groups * seq_bytes + _nbytes((2, ATTN_BLOCK, 2 * ATTN_BLOCK), F32)
    out = pl.pallas_call(
        functools.partial(_dilated_kernel, dilations=dilations),
        out_shape=jax.ShapeDtypeStruct((bn, seq, width), BF16),
        grid=(bn, heads),
        in_specs=[spec] * (n_groups + 2),
        out_specs=spec,
        scratch_shapes=[
            pltpu.VMEM((n_groups, seq, HEAD_DIM), F32),
            pltpu.VMEM((n_groups, seq, HEAD_DIM), F32),
            pltpu.VMEM((2, ATTN_BLOCK, 2 * ATTN_BLOCK), F32),
        ],
        compiler_params=pltpu.CompilerParams(
            dimension_semantics=("parallel", "parallel"),
            vmem_limit_bytes=_vmem_limit((n_groups + 2) * seq_bytes + seq_bytes // 2, scratch,
                                         8 * _nbytes((MERGE_ROWS, HEAD_DIM), F32))),
        name="dilated_attention",
    )(*[as_seq(q) for q in qs], as_seq(k), as_seq(v))
    return out.reshape(t, width)


def _rope_tables(seq):
    half = HEAD_DIM // 2
    inv = ROPE_THETA ** (-jnp.arange(half, dtype=F32) / half)
    ang = jnp.arange(seq).astype(F32)[:, None] * inv[None, :]
    cos, sin = jnp.cos(ang), jnp.sin(ang)
    return jnp.concatenate([cos, cos], axis=-1), jnp.concatenate([-sin, sin], axis=-1)


def kernel(x, mem, norm_mix, norm_ffn, a_w_in, a_lb_logits, a_onorm, b_w_in, b_qnorm, kv_norm, w_kv,
           b_knorm, mem_norm, w_mem_kv, mem_qnorm, mem_knorm, w_out, w_gate_up, w_down):
    bn, seq, dm = x.shape
    depth = norm_mix.shape[0]
    n_a = a_w_in.shape[0]
    assert seq % (ATTN_BLOCK * max(d for _, d in DILATED_GROUPS)) == 0 and seq % HGRN_ROWS == 0
    assert all(w == ATTN_BLOCK * d for w, d in DILATED_GROUPS)

    bf = lambda a: a.astype(BF16)
    row_vec = lambda a: a.reshape(1, -1)
    pair = lambda a: jnp.concatenate([a, a], axis=-1).reshape(1, V7X_LANES)
    cos2, sin2 = _rope_tables(seq)

    mk, mv = _mem_kv(mem, mem_norm, bf(w_mem_kv), mem_knorm)
    h = x.reshape(bn * seq, dm)
    k_sh = v_sh = None
    for l in range(depth):
        gain = row_vec(norm_mix[l])
        if l < n_a:
            q, f, i, g, mo = _inproj_a(h, gain, bf(a_w_in[l]), pair(mem_qnorm[l]), mk, mv, l, seq)
            o = _hgrn2(q, f, i, g, a_lb_logits, row_vec(a_onorm[l]), l, bn, seq)
        else:
            j = l - n_a
            *qs, mo = _inproj_b(h, gain, bf(b_w_in[j]), b_qnorm[j], cos2, sin2,
                                pair(mem_qnorm[l]), mk, mv, l, seq)
            o = _dilated_attention(qs, k_sh, v_sh, bn, seq)
        kv = None
        if l == n_a - 1:
            kv = (row_vec(kv_norm), bf(w_kv), row_vec(b_knorm), cos2, sin2)
        outs = _mix_ffn(h, o, mo, bf(w_out[l]), row_vec(norm_ffn[l]), bf(w_gate_up[l]), bf(w_down[l]),
                        seq, kv)
        h = outs[0]
        if kv is not None:
            k_sh, v_sh = outs[1], outs[2]
    return h.reshape(bn, seq, dm)
```

```python
import functools

import jax
import jax.numpy as jnp
from jax import lax
from jax.experimental import pallas as pl
from jax.experimental.pallas import tpu as pltpu

F32 = jnp.float32
BF16 = jnp.bfloat16

EPS = 1e-6
HEAD_DIM = 128
CHUNK = 64
MEM_HEAD_DIM = 64
DILATED_GROUPS = ((128, 1), (512, 4), (2048, 16))
ROPE_THETA = 10000.0

V7X_LANES = 128
V7X_VMEM_SCOPED_MAX_BYTES = 60000 * 1024

ROW_TILE = 256
FFN_CHUNK = 2816
HGRN_ROWS = 512
ATTN_BLOCK = 128
MERGE_ROWS = 512
HGRN_UNROLL = 8
ATTN_UNROLL = 8

_NT = (((1,), (1,)), ((), ()))
_TN = (((0,), (0,)), ((), ()))


def _vmem_limit(pipelined_bytes, resident_bytes, temp_bytes):
    need = 2 * pipelined_bytes + resident_bytes + temp_bytes
    return int(min(max(need, 16 * 1024 * 1024), V7X_VMEM_SCOPED_MAX_BYTES))


def _nbytes(shape, dtype):
    n = 1
    for s in shape:
        n *= s
    return n * jnp.dtype(dtype).itemsize


def _resident(shape):
    zeros = (0,) * len(shape)
    return pl.BlockSpec(shape, lambda *_: zeros, pipeline_mode=pl.Buffered(1))


def _dot(a, b):
    return jnp.dot(a, b, preferred_element_type=F32)


def _rms(x, gain):
    ms = jnp.mean(x * x, axis=-1, keepdims=True)
    return x * lax.rsqrt(ms + EPS) * gain


def _silu(x):
    return x * jax.nn.sigmoid(x)


def _rms_head_pairs(x, gain):
    lo = lax.broadcasted_iota(jnp.int32, x.shape, 1) < MEM_HEAD_DIM
    x2 = x * x
    s_lo = jnp.sum(jnp.where(lo, x2, 0.0), axis=-1, keepdims=True)
    s_hi = jnp.sum(jnp.where(lo, 0.0, x2), axis=-1, keepdims=True)
    ms = jnp.where(lo, s_lo, s_hi) * (1.0 / MEM_HEAD_DIM)
    return x * lax.rsqrt(ms + EPS) * gain


def _rope(x, cos2, sin2):
    return x * cos2 + pltpu.roll(x, HEAD_DIM // 2, axis=1) * sin2


def _memory_probs(mq, qgain, mk_ref):
    scaled_gain = qgain * (MEM_HEAD_DIM ** -0.5)
    probs = []
    for t in range(mq.shape[1] // V7X_LANES):
        cols = slice(t * V7X_LANES, (t + 1) * V7X_LANES)
        qn = _rms_head_pairs(mq[:, cols], scaled_gain)
        lo = lax.broadcasted_iota(jnp.int32, qn.shape, 1) < MEM_HEAD_DIM
        for keep in (lo, jnp.logical_not(lo)):
            qh = jnp.where(keep, qn, 0.0).astype(BF16)
            s = lax.dot_general(qh, mk_ref[:, cols], _NT, preferred_element_type=F32)
            p = jnp.exp(s - jnp.max(s, axis=-1, keepdims=True))
            probs.append((p.astype(BF16), jnp.sum(p, axis=-1, keepdims=True)))
    return probs


def _memory_output(probs, mv_ref, mo_ref):
    for t in range(mo_ref.shape[1] // V7X_LANES):
        cols = slice(t * V7X_LANES, (t + 1) * V7X_LANES)
        outs = [_dot(p, mv_ref[:, cols]) / denom for p, denom in probs[2 * t:2 * t + 2]]
        lo = lax.broadcasted_iota(jnp.int32, outs[0].shape, 1) < MEM_HEAD_DIM
        mo_ref[:, cols] = jnp.where(lo, outs[0], outs[1]).astype(mo_ref.dtype)


def _mem_kv_kernel(mem_ref, gain_ref, w_ref, kgain_ref, mk_ref, mv_ref):
    mw = mk_ref.shape[1]
    mn = _rms(mem_ref[...], gain_ref[...]).astype(BF16)
    kv = _dot(mn, w_ref[...])
    for t in range(mw // V7X_LANES):
        cols = slice(t * V7X_LANES, (t + 1) * V7X_LANES)
        mk_ref[:, cols] = _rms_head_pairs(kv[:, cols], kgain_ref[...]).astype(mk_ref.dtype)
    mv_ref[...] = kv[:, mw:].astype(mv_ref.dtype)


def _mem_kv(mem, mem_norm, w_mem_kv, mem_knorm):
    bn, mt, dm = mem.shape
    depth = w_mem_kv.shape[0]
    mw = w_mem_kv.shape[2] // 2
    kgain = jnp.concatenate([mem_knorm, mem_knorm], axis=-1).reshape(depth, 1, V7X_LANES)
    out = jax.ShapeDtypeStruct((depth, bn, mt, mw), BF16)
    return pl.pallas_call(
        _mem_kv_kernel,
        out_shape=(out, out),
        grid=(depth, bn),
        in_specs=[
            pl.BlockSpec((None, mt, dm), lambda l, b: (b, 0, 0)),
            pl.BlockSpec((None, 1, dm), lambda l, b: (l, 0, 0)),
            pl.BlockSpec((None, dm, 2 * mw), lambda l, b: (l, 0, 0)),
            pl.BlockSpec((None, 1, V7X_LANES), lambda l, b: (l, 0, 0)),
        ],
        out_specs=(
            pl.BlockSpec((None, None, mt, mw), lambda l, b: (l, b, 0, 0)),
            pl.BlockSpec((None, None, mt, mw), lambda l, b: (l, b, 0, 0)),
        ),
        name="mem_kv",
    )(mem, mem_norm.reshape(depth, 1, dm), w_mem_kv, kgain)


def _inproj_a_kernel(x_ref, gain_ref, w_ref, mqg_ref, mk_ref, mv_ref,
                     q_ref, f_ref, i_ref, g_ref, mo_ref):
    aw = q_ref.shape[1]
    xn = _rms(x_ref[...], gain_ref[...]).astype(BF16)
    probs = _memory_probs(_dot(xn, w_ref[:, 4 * aw:]), mqg_ref[...], mk_ref)
    q_ref[...] = _dot(xn, w_ref[:, 0:aw]).astype(q_ref.dtype)
    _memory_output(probs, mv_ref, mo_ref)
    f_ref[...] = _dot(xn, w_ref[:, aw:2 * aw])
    i_ref[...] = _dot(xn, w_ref[:, 2 * aw:3 * aw]).astype(i_ref.dtype)
    g_ref[...] = _dot(xn, w_ref[:, 3 * aw:4 * aw]).astype(g_ref.dtype)


def _inproj_a(h, gain, w, mq_gain, mk, mv, layer, seq):
    t, dm = h.shape
    mt, mw = mk.shape[2], mk.shape[3]
    aw = (w.shape[1] - mw) // 4
    tiles_per_seq = seq // ROW_TILE
    row = lambda i: (i, 0)
    mem = lambda i: (layer, i // tiles_per_seq, 0, 0)
    wide = functools.partial(jax.ShapeDtypeStruct, (t, aw))
    pipelined = (_nbytes((ROW_TILE, dm), F32) + _nbytes((ROW_TILE, aw), F32)
                 + 3 * _nbytes((ROW_TILE, aw), BF16) + _nbytes((ROW_TILE, mw), BF16)
                 + 2 * _nbytes((mt, mw), BF16))
    return pl.pallas_call(
        _inproj_a_kernel,
        out_shape=(wide(BF16), wide(F32), wide(BF16), wide(BF16), jax.ShapeDtypeStruct((t, mw), BF16)),
        grid=(t // ROW_TILE,),
        in_specs=[
            pl.BlockSpec((ROW_TILE, dm), row),
            _resident((1, dm)),
            _resident(w.shape),
            _resident((1, V7X_LANES)),
            pl.BlockSpec((None, None, mt, mw), mem),
            pl.BlockSpec((None, None, mt, mw), mem),
        ],
        out_specs=(
            pl.BlockSpec((ROW_TILE, aw), row), pl.BlockSpec((ROW_TILE, aw), row),
            pl.BlockSpec((ROW_TILE, aw), row), pl.BlockSpec((ROW_TILE, aw), row),
            pl.BlockSpec((ROW_TILE, mw), row),
        ),
        compiler_params=pltpu.CompilerParams(
            dimension_semantics=("parallel",),
            vmem_limit_bytes=_vmem_limit(pipelined, _nbytes(w.shape, BF16),
                                         4 * _nbytes((ROW_TILE, aw), F32))),
        name="inproj_a",
    )(h, gain, w, mq_gain, mk, mv)


def _hgrn2_kernel(lbl_ref, on_ref, q_ref, f_ref, i_ref, g_ref, o_ref, state_ref, *, layer):
    @pl.when(pl.program_id(2) == 0)
    def _():
        state_ref[...] = jnp.zeros_like(state_ref)

    lg = lbl_ref[...]
    e = jnp.exp(lg - jnp.max(lg, axis=0, keepdims=True))
    lb = jnp.sum(e[:layer + 1], axis=0, keepdims=True) / jnp.sum(e, axis=0, keepdims=True)

    row = lax.broadcasted_iota(jnp.int32, (CHUNK, HEAD_DIM), 0)
    causal = (lax.broadcasted_iota(jnp.int32, (CHUNK, CHUNK), 0)
              >= lax.broadcasted_iota(jnp.int32, (CHUNK, CHUNK), 1))
    onorm = on_ref[...]

    def chunk(c, state_t):
        rows = pl.ds(pl.multiple_of(c * CHUNK, CHUNK), CHUNK)
        f = lb + (1.0 - lb) * jax.nn.sigmoid(f_ref[rows, :])
        k = 1.0 - f
        b = jnp.log(f)
        shift = 1
        while shift < CHUNK:
            b = b + jnp.where(row >= shift, pltpu.roll(b, shift, axis=0), 0.0)
            shift *= 2
        b_end = b[CHUNK - 1:CHUNK, :]
        q_in = (_silu(q_ref[rows, :].astype(F32)) * jnp.exp(b)).astype(BF16)
        k_in = (k * jnp.exp(-b)).astype(BF16)
        k_out = (k * jnp.exp(b_end - b)).astype(BF16)
        v = i_ref[rows, :]
        att = lax.dot_general(q_in, k_in, _NT, preferred_element_type=F32)
        att = jnp.where(causal, att, 0.0).astype(BF16)
        o = _dot(att, v) + lax.dot_general(q_in, state_t.astype(BF16), _NT, preferred_element_type=F32)
        gate = _silu(g_ref[rows, :].astype(F32))
        o_ref[rows, :] = (_rms(o, onorm) * gate).astype(o_ref.dtype)
        return state_t * jnp.exp(b_end) + lax.dot_general(v, k_out, _TN, preferred_element_type=F32)

    state_ref[...] = lax.fori_loop(0, q_ref.shape[0] // CHUNK, chunk, state_ref[...],
                                   unroll=HGRN_UNROLL)


def _hgrn2(q, f, i, g, lb_logits, onorm, layer, bn, seq):
    t, aw = q.shape
    heads = aw // HEAD_DIM
    steps = seq // HGRN_ROWS
    blk = lambda b, h, s: (b * steps + s, h)
    spec = pl.BlockSpec((HGRN_ROWS, HEAD_DIM), blk)
    return pl.pallas_call(
        functools.partial(_hgrn2_kernel, layer=layer),
        out_shape=jax.ShapeDtypeStruct((t, aw), BF16),
        grid=(bn, heads, steps),
        in_specs=[
            pl.BlockSpec((lb_logits.shape[0], HEAD_DIM), lambda b, h, s: (0, h)),
            pl.BlockSpec((1, HEAD_DIM), lambda b, h, s: (0, h)),
            spec, spec, spec, spec,
        ],
        out_specs=spec,
        scratch_shapes=[pltpu.VMEM((HEAD_DIM, HEAD_DIM), F32)],
        compiler_params=pltpu.CompilerParams(
            dimension_semantics=("parallel", "parallel", "arbitrary")),
        name="hgrn2",
    )(lb_logits, onorm, q, f, i, g)


def _mix_ffn_kernel(*refs, with_kv):
    if with_kv:
        (h_ref, o_ref, mo_ref, wo_ref, nf_ref, wgu_ref, wd_ref,
         kvn_ref, wkv_ref, kn_ref, cos_ref, sin_ref, h_out, k_out, v_out) = refs
    else:
        h_ref, o_ref, mo_ref, wo_ref, nf_ref, wgu_ref, wd_ref, h_out = refs
    main_w = o_ref.shape[1]
    hidden = wd_ref.shape[0]
    h = h_ref[...] + _dot(o_ref[...], wo_ref[0:main_w, :]) + _dot(mo_ref[...], wo_ref[main_w:, :])
    hn = _rms(h, nf_ref[...]).astype(BF16)
    for c in range(hidden // FFN_CHUNK):
        cols = slice(c * FFN_CHUNK, (c + 1) * FFN_CHUNK)
        up_cols = slice(hidden + c * FFN_CHUNK, hidden + (c + 1) * FFN_CHUNK)
        act = (_silu(_dot(hn, wgu_ref[:, cols])) * _dot(hn, wgu_ref[:, up_cols])).astype(BF16)
        h = h + _dot(act, wd_ref[cols, :])
    h_out[...] = h
    if with_kv:
        kv_w = k_out.shape[1]
        kn = _rms(h, kvn_ref[...]).astype(BF16)
        k = _dot(kn, wkv_ref[:, 0:kv_w])
        for hd in range(kv_w // HEAD_DIM):
            cols = slice(hd * HEAD_DIM, (hd + 1) * HEAD_DIM)
            k_out[:, cols] = _rope(_rms(k[:, cols], kn_ref[...]), cos_ref[...], sin_ref[...])
        v_out[...] = _dot(kn, wkv_ref[:, kv_w:])


def _mix_ffn(h, o, mo, w_out, norm_ffn, w_gate_up, w_down, seq, kv=None):
    t, dm = h.shape
    main_w, mw = o.shape[1], mo.shape[1]
    hidden = w_down.shape[0]
    row = lambda i: (i, 0)
    in_specs = [
        pl.BlockSpec((ROW_TILE, dm), row), pl.BlockSpec((ROW_TILE, main_w), row),
        pl.BlockSpec((ROW_TILE, mw), row),
        _resident(w_out.shape), _resident((1, dm)), _resident(w_gate_up.shape), _resident(w_down.shape),
    ]
    args = [h, o, mo, w_out, norm_ffn, w_gate_up, w_down]
    out_shape = [jax.ShapeDtypeStruct((t, dm), F32)]
    out_specs = [pl.BlockSpec((ROW_TILE, dm), row)]
    resident = _nbytes(w_out.shape, BF16) + _nbytes(w_gate_up.shape, BF16) + _nbytes(w_down.shape, BF16)
    pipelined = 2 * _nbytes((ROW_TILE, dm), F32) + _nbytes((ROW_TILE, main_w + mw), BF16)
    if kv is not None:
        kv_norm, w_kv, k_norm, cos2, sin2 = kv
        kv_w = w_kv.shape[1] // 2
        tiles_per_seq = seq // ROW_TILE
        pos = lambda i: (i % tiles_per_seq, 0)
        in_specs += [_resident((1, dm)), _resident(w_kv.shape), _resident((1, HEAD_DIM)),
                     pl.BlockSpec((ROW_TILE, HEAD_DIM), pos), pl.BlockSpec((ROW_TILE, HEAD_DIM), pos)]
        args += [kv_norm, w_kv, k_norm, cos2, sin2]
        out_shape += [jax.ShapeDtypeStruct((t, kv_w), F32)] * 2
        out_specs += [pl.BlockSpec((ROW_TILE, kv_w), row)] * 2
        resident += _nbytes(w_kv.shape, BF16)
        pipelined += 2 * _nbytes((ROW_TILE, kv_w), F32) + 2 * _nbytes((ROW_TILE, HEAD_DIM), F32)
    temps = 3 * _nbytes((ROW_TILE, dm), F32) + 3 * _nbytes((ROW_TILE, FFN_CHUNK), F32)
    return pl.pallas_call(
        functools.partial(_mix_ffn_kernel, with_kv=kv is not None),
        out_shape=tuple(out_shape),
        grid=(t // ROW_TILE,),
        in_specs=in_specs,
        out_specs=tuple(out_specs),
        compiler_params=pltpu.CompilerParams(
            dimension_semantics=("parallel",),
            vmem_limit_bytes=_vmem_limit(pipelined, resident, temps)),
        name="mix_ffn_kv" if kv is not None else "mix_ffn",
    )(*args)


def _inproj_b_kernel(x_ref, gain_ref, w_ref, qn_ref, cos_ref, sin_ref, mqg_ref, mk_ref, mv_ref,
                     *out_refs):
    q_refs, mo_ref = out_refs[:-1], out_refs[-1]
    bw = q_refs[0].shape[1]
    xn = _rms(x_ref[...], gain_ref[...]).astype(BF16)
    probs = None
    for gi, q_ref in enumerate(q_refs):
        qs = _dot(xn, w_ref[:, gi * bw:(gi + 1) * bw])
        if gi == 0:
            probs = _memory_probs(_dot(xn, w_ref[:, len(q_refs) * bw:]), mqg_ref[...], mk_ref)
        if gi == 1:
            _memory_output(probs, mv_ref, mo_ref)
        for hd in range(bw // HEAD_DIM):
            cols = slice(hd * HEAD_DIM, (hd + 1) * HEAD_DIM)
            q_ref[:, cols] = _rope(_rms(qs[:, cols], qn_ref[gi]), cos_ref[...], sin_ref[...])


def _inproj_b(h, gain, w, q_norm, cos2, sin2, mq_gain, mk, mv, layer, seq):
    t, dm = h.shape
    mt, mw = mk.shape[2], mk.shape[3]
    n_groups = q_norm.shape[0]
    bw = (w.shape[1] - mw) // n_groups
    tiles_per_seq = seq // ROW_TILE
    row = lambda i: (i, 0)
    pos = lambda i: (i % tiles_per_seq, 0)
    mem = lambda i: (layer, i // tiles_per_seq, 0, 0)
    pipelined = (_nbytes((ROW_TILE, dm), F32) + n_groups * _nbytes((ROW_TILE, bw), F32)
                 + 2 * _nbytes((ROW_TILE, HEAD_DIM), F32) + _nbytes((ROW_TILE, mw), BF16)
                 + 2 * _nbytes((mt, mw), BF16))
    return pl.pallas_call(
        _inproj_b_kernel,
        out_shape=tuple([jax.ShapeDtypeStruct((t, bw), F32)] * n_groups
                        + [jax.ShapeDtypeStruct((t, mw), BF16)]),
        grid=(t // ROW_TILE,),
        in_specs=[
            pl.BlockSpec((ROW_TILE, dm), row),
            _resident((1, dm)),
            _resident(w.shape),
            _resident((n_groups, 1, HEAD_DIM)),
            pl.BlockSpec((ROW_TILE, HEAD_DIM), pos), pl.BlockSpec((ROW_TILE, HEAD_DIM), pos),
            _resident((1, V7X_LANES)),
            pl.BlockSpec((None, None, mt, mw), mem),
            pl.BlockSpec((None, None, mt, mw), mem),
        ],
        out_specs=tuple([pl.BlockSpec((ROW_TILE, bw), row)] * n_groups
                        + [pl.BlockSpec((ROW_TILE, mw), row)]),
        compiler_params=pltpu.CompilerParams(
            dimension_semantics=("parallel",),
            vmem_limit_bytes=_vmem_limit(pipelined, _nbytes(w.shape, BF16),
                                         4 * _nbytes((ROW_TILE, bw), F32))),
        name="inproj_b",
    )(h, gain, w, q_norm.reshape(n_groups, 1, HEAD_DIM), cos2, sin2, mq_gain, mk, mv)


def _strided_rows(start, size, stride):
    return pl.ds(start, size) if stride == 1 else pl.ds(start, size, stride=stride)


def _dilated_kernel(*refs, dilations):
    n_groups = len(dilations)
    q_refs = refs[:n_groups]
    k_ref, v_ref, o_ref, og_ref, lse_ref, bias_ref = refs[n_groups:]
    seq = k_ref.shape[0]
    blk = ATTN_BLOCK
    scale = HEAD_DIM ** -0.5

    qi = lax.broadcasted_iota(jnp.int32, (blk, 2 * blk), 0)
    kj = lax.broadcasted_iota(jnp.int32, (blk, 2 * blk), 1)
    band = (kj >= qi) & (kj <= qi + blk)
    bias_ref[0] = jnp.where(band & (kj >= blk), 0.0, -jnp.inf)
    bias_ref[1] = jnp.where(band, 0.0, -jnp.inf)

    for gi, dil in enumerate(dilations):
        q_ref = q_refs[gi]
        n_blocks = seq // (blk * dil)
        span = blk * dil

        def block(idx, carry, q_ref=q_ref, gi=gi, dil=dil, n_blocks=n_blocks, span=span):
            n = lax.rem(idx, n_blocks)
            start = lax.div(idx, n_blocks) + n * span
            prev = jnp.maximum(start - span, 0)
            cur_rows = _strided_rows(start, blk, dil)
            prev_rows = _strided_rows(prev, blk, dil)
            q = q_ref[cur_rows, :].astype(BF16)
            kw = jnp.concatenate([k_ref[prev_rows, :], k_ref[cur_rows, :]], axis=0).astype(BF16)
            vw = jnp.concatenate([v_ref[prev_rows, :], v_ref[cur_rows, :]], axis=0).astype(BF16)
            s = lax.dot_general(q, kw, _NT, preferred_element_type=F32) * scale
            s = s + bias_ref[jnp.minimum(n, 1)]
            m = jnp.max(s, axis=-1, keepdims=True)
            p = jnp.exp(s - m)
            denom = jnp.sum(p, axis=-1, keepdims=True)
            og_ref[gi, cur_rows, :] = _dot(p.astype(BF16), vw) / denom
            lse_ref[gi, cur_rows, :] = jnp.broadcast_to(m + jnp.log(denom), (blk, HEAD_DIM))
            return carry

        lax.fori_loop(0, dil * n_blocks, block, 0, unroll=ATTN_UNROLL)

    def merge(step, carry):
        rows = pl.ds(pl.multiple_of(step * MERGE_ROWS, MERGE_ROWS), MERGE_ROWS)
        lses = [lse_ref[gi, rows, :] for gi in range(n_groups)]
        top = functools.reduce(jnp.maximum, lses)
        ws = [jnp.exp(l - top) for l in lses]
        acc = sum(w * og_ref[gi, rows, :] for gi, w in enumerate(ws))
        o_ref[rows, :] = (acc / sum(ws)).astype(o_ref.dtype)
        return carry

    lax.fori_loop(0, seq // MERGE_ROWS, merge, 0)


def _dilated_attention(qs, k, v, bn, seq):
    t, width = k.shape
    heads = width // HEAD_DIM
    dilations = tuple(d for _, d in DILATED_GROUPS)
    n_groups = len(dilations)
    as_seq = lambda a: a.reshape(bn, seq, width)
    spec = pl.BlockSpec((None, seq, HEAD_DIM), lambda b, h: (b, 0, h))
    seq_bytes = _nbytes((seq, HEAD_DIM), F32)
    scratch = 2 * n_groups * seq_bytes + _nbytes((2, ATTN_BLOCK, 2 * ATTN_BLOCK), F32)
    out = pl.pallas_call(
        functools.partial(_dilated_kernel, dilations=dilations),
        out_shape=jax.ShapeDtypeStruct((bn, seq, width), BF16),
        grid=(bn, heads),
        in_specs=[spec] * (n_groups + 2),
        out_specs=spec,
        scratch_shapes=[
            pltpu.VMEM((n_groups, seq, HEAD_DIM), F32),
            pltpu.VMEM((n_groups, seq, HEAD_DIM), F32),
            pltpu.VMEM((2, ATTN_BLOCK, 2 * ATTN_BLOCK), F32),
        ],
        compiler_params=pltpu.CompilerParams(
            dimension_semantics=("parallel", "parallel"),
            vmem_limit_bytes=_vmem_limit((n_groups + 2) * seq_bytes + seq_bytes // 2, scratch,
                                         8 * _nbytes((MERGE_ROWS, HEAD_DIM), F32))),
        name="dilated_attention",
    )(*[as_seq(q) for q in qs], as_seq(k), as_seq(v))
    return out.reshape(t, width)


def _rope_tables(seq):
    half = HEAD_DIM // 2
    inv = ROPE_THETA ** (-jnp.arange(half, dtype=F32) / half)
    ang = jnp.arange(seq).astype(F32)[:, None] * inv[None, :]
    cos, sin = jnp.cos(ang), jnp.sin(ang)
    return jnp.concatenate([cos, cos], axis=-1), jnp.concatenate([-sin, sin], axis=-1)


def kernel(x, mem, norm_mix, norm_ffn, a_w_in, a_lb_logits, a_onorm, b_w_in, b_qnorm, kv_norm, w_kv,
           b_knorm, mem_norm, w_mem_kv, mem_qnorm, mem_knorm, w_out, w_gate_up, w_down):
    bn, seq, dm = x.shape
    depth = norm_mix.shape[0]
    n_a = a_w_in.shape[0]
    assert seq % (ATTN_BLOCK * max(d for _, d in DILATED_GROUPS)) == 0 and seq % HGRN_ROWS == 0
    assert all(w == ATTN_BLOCK * d for w, d in DILATED_GROUPS)

    bf = lambda a: a.astype(BF16)
    row_vec = lambda a: a.reshape(1, -1)
    pair = lambda a: jnp.concatenate([a, a], axis=-1).reshape(1, V7X_LANES)
    cos2, sin2 = _rope_tables(seq)

    mk, mv = _mem_kv(mem, mem_norm, bf(w_mem_kv), mem_knorm)
    h = x.reshape(bn * seq, dm)
    k_sh = v_sh = None
    for l in range(depth):
        gain = row_vec(norm_mix[l])
        if l < n_a:
            q, f, i, g, mo = _inproj_a(h, gain, bf(a_w_in[l]), pair(mem_qnorm[l]), mk, mv, l, seq)
            o = _hgrn2(q, f, i, g, a_lb_logits, row_vec(a_onorm[l]), l, bn, seq)
        else:
            j = l - n_a
            *qs, mo = _inproj_b(h, gain, bf(b_w_in[j]), b_qnorm[j], cos2, sin2,
                                pair(mem_qnorm[l]), mk, mv, l, seq)
            o = _dilated_attention(qs, k_sh, v_sh, bn, seq)
        kv = None
        if l == n_a - 1:
            kv = (row_vec(kv_norm), bf(w_kv), row_vec(b_knorm), cos2, sin2)
        outs = _mix_ffn(h, o, mo, bf(w_out[l]), row_vec(norm_ffn[l]), bf(w_gate_up[l]), bf(w_down[l]),
                        seq, kv)
        h = outs[0]
        if kv is not None:
            k_sh, v_sh = outs[1], outs[2]
    return h.reshape(bn, seq, dm)
```

```python
import functools
import math

import jax
import jax.numpy as jnp
from jax import lax
from jax.experimental import pallas as pl
from jax.experimental.pallas import tpu as pltpu

F32 = jnp.float32
BF16 = jnp.bfloat16

EPS = 1e-6
HEAD_DIM = 128
CHUNK = 64
MEM_HEAD_DIM = 64
DILATED_GROUPS = ((128, 1), (512, 4), (2048, 16))
ROPE_THETA = 10000.0
LOG2E = math.log2(math.e)

V7X_LANES = 128
V7X_VMEM_SCOPED_MAX_BYTES = 60000 * 1024

ROW_TILE = 256
FFN_CHUNK = 2816
HGRN_ROWS = 1024
ATTN_BLOCK = 128
MERGE_ROWS = 512
ATTN_UNROLL = 16

_NT = (((1,), (1,)), ((), ()))
_TN = (((0,), (0,)), ((), ()))


def _vmem_limit(pipelined_bytes, resident_bytes, temp_bytes):
    need = 2 * pipelined_bytes + resident_bytes + temp_bytes
    return int(min(max(need, 16 * 1024 * 1024), V7X_VMEM_SCOPED_MAX_BYTES))


def _nbytes(shape, dtype):
    n = 1
    for s in shape:
        n *= s
    return n * jnp.dtype(dtype).itemsize


def _resident(shape):
    zeros = (0,) * len(shape)
    return pl.BlockSpec(shape, lambda *_: zeros, pipeline_mode=pl.Buffered(1))


def _dot(a, b):
    return jnp.dot(a, b, preferred_element_type=F32)


def _rms(x, gain):
    ms = jnp.mean(x * x, axis=-1, keepdims=True)
    return x * lax.rsqrt(ms + EPS) * gain


def _silu(x):
    return x * jax.nn.sigmoid(x)


def _head_cols(hd):
    return slice(hd * HEAD_DIM, (hd + 1) * HEAD_DIM)


def _rms_head_pairs(x, gain):
    lo = lax.broadcasted_iota(jnp.int32, x.shape, 1) < MEM_HEAD_DIM
    x2 = x * x
    s_lo = jnp.sum(jnp.where(lo, x2, 0.0), axis=-1, keepdims=True)
    s_hi = jnp.sum(jnp.where(lo, 0.0, x2), axis=-1, keepdims=True)
    ms = jnp.where(lo, s_lo, s_hi) * (1.0 / MEM_HEAD_DIM)
    return x * lax.rsqrt(ms + EPS) * gain


def _rope(x, cos2, sin2):
    return x * cos2 + pltpu.roll(x, HEAD_DIM // 2, axis=1) * sin2


def _store_by_residue(slab_ref, out_ref, dil):
    rows = slab_ref.shape[1] // dil
    for r in range(dil):
        for hd in range(slab_ref.shape[0]):
            out_ref[r, :, _head_cols(hd)] = slab_ref[hd, pl.ds(r, rows, stride=dil), :].astype(out_ref.dtype)


def _memory_probs(mq, qgain, mk_ref):
    scaled_gain = qgain * (MEM_HEAD_DIM ** -0.5)
    probs = []
    for t in range(mq.shape[1] // V7X_LANES):
        cols = slice(t * V7X_LANES, (t + 1) * V7X_LANES)
        qn = _rms_head_pairs(mq[:, cols], scaled_gain)
        lo = lax.broadcasted_iota(jnp.int32, qn.shape, 1) < MEM_HEAD_DIM
        for keep in (lo, jnp.logical_not(lo)):
            qh = jnp.where(keep, qn, 0.0).astype(BF16)
            s = lax.dot_general(qh, mk_ref[:, cols], _NT, preferred_element_type=F32)
            p = jnp.exp(s - jnp.max(s, axis=-1, keepdims=True))
            probs.append((p.astype(BF16), jnp.sum(p, axis=-1, keepdims=True)))
    return probs


def _memory_output(probs, mv_ref, mo_ref):
    for t in range(mo_ref.shape[1] // V7X_LANES):
        cols = slice(t * V7X_LANES, (t + 1) * V7X_LANES)
        outs = [_dot(p, mv_ref[:, cols]) / denom for p, denom in probs[2 * t:2 * t + 2]]
        lo = lax.broadcasted_iota(jnp.int32, outs[0].shape, 1) < MEM_HEAD_DIM
        mo_ref[:, cols] = jnp.where(lo, outs[0], outs[1]).astype(mo_ref.dtype)


def _mem_kv_kernel(mem_ref, gain_ref, w_ref, kgain_ref, mk_ref, mv_ref):
    mw = mk_ref.shape[1]
    mn = _rms(mem_ref[...], gain_ref[...]).astype(BF16)
    kv = _dot(mn, w_ref[...])
    for t in range(mw // V7X_LANES):
        cols = slice(t * V7X_LANES, (t + 1) * V7X_LANES)
        mk_ref[:, cols] = _rms_head_pairs(kv[:, cols], kgain_ref[...]).astype(mk_ref.dtype)
    mv_ref[...] = kv[:, mw:].astype(mv_ref.dtype)


def _mem_kv(mem, mem_norm, w_mem_kv, mem_knorm):
    bn, mt, dm = mem.shape
    depth = w_mem_kv.shape[0]
    mw = w_mem_kv.shape[2] // 2
    kgain = jnp.concatenate([mem_knorm, mem_knorm], axis=-1).reshape(depth, 1, V7X_LANES)
    out = jax.ShapeDtypeStruct((depth, bn, mt, mw), BF16)
    return pl.pallas_call(
        _mem_kv_kernel,
        out_shape=(out, out),
        grid=(depth, bn),
        in_specs=[
            pl.BlockSpec((None, mt, dm), lambda l, b: (b, 0, 0)),
            pl.BlockSpec((None, 1, dm), lambda l, b: (l, 0, 0)),
            pl.BlockSpec((None, dm, 2 * mw), lambda l, b: (l, 0, 0)),
            pl.BlockSpec((None, 1, V7X_LANES), lambda l, b: (l, 0, 0)),
        ],
        out_specs=(
            pl.BlockSpec((None, None, mt, mw), lambda l, b: (l, b, 0, 0)),
            pl.BlockSpec((None, None, mt, mw), lambda l, b: (l, b, 0, 0)),
        ),
        name="mem_kv",
    )(mem, mem_norm.reshape(depth, 1, dm), w_mem_kv, kgain)


def _inproj_a_kernel(x_ref, gain_ref, w_ref, lbl_ref, mqg_ref, mk_ref, mv_ref,
                     qs_ref, lf_ref, k_ref, v_ref, gate_ref, mo_ref, *, layer):
    aw = qs_ref.shape[1]
    xn = _rms(x_ref[...], gain_ref[...]).astype(BF16)
    lg = lbl_ref[...]
    e = jnp.exp(lg - jnp.max(lg, axis=0, keepdims=True))
    lb = jnp.sum(e[:layer + 1], axis=0, keepdims=True) / jnp.sum(e, axis=0, keepdims=True)
    probs = _memory_probs(_dot(xn, w_ref[:, 4 * aw:]), mqg_ref[...], mk_ref)
    qs_ref[...] = _silu(_dot(xn, w_ref[:, 0:aw])).astype(qs_ref.dtype)
    _memory_output(probs, mv_ref, mo_ref)
    f = lb + (1.0 - lb) * jax.nn.sigmoid(_dot(xn, w_ref[:, aw:2 * aw]))
    lf_ref[...] = jnp.log(f) * LOG2E
    k_ref[...] = (1.0 - f).astype(k_ref.dtype)
    v_ref[...] = _dot(xn, w_ref[:, 2 * aw:3 * aw]).astype(v_ref.dtype)
    gate_ref[...] = _silu(_dot(xn, w_ref[:, 3 * aw:4 * aw])).astype(gate_ref.dtype)


def _inproj_a(h, gain, w, lb_logits, mq_gain, mk, mv, layer, seq):
    t, dm = h.shape
    mt, mw = mk.shape[2], mk.shape[3]
    aw = (w.shape[1] - mw) // 4
    tiles_per_seq = seq // ROW_TILE
    row = lambda i: (i, 0)
    mem = lambda i: (layer, i // tiles_per_seq, 0, 0)
    wide = functools.partial(jax.ShapeDtypeStruct, (t, aw))
    pipelined = (_nbytes((ROW_TILE, dm), F32) + _nbytes((ROW_TILE, aw), F32)
                 + 4 * _nbytes((ROW_TILE, aw), BF16) + _nbytes((ROW_TILE, mw), BF16)
                 + 2 * _nbytes((mt, mw), BF16))
    return pl.pallas_call(
        functools.partial(_inproj_a_kernel, layer=layer),
        out_shape=(wide(BF16), wide(F32), wide(BF16), wide(BF16), wide(BF16),
                   jax.ShapeDtypeStruct((t, mw), BF16)),
        grid=(t // ROW_TILE,),
        in_specs=[
            pl.BlockSpec((ROW_TILE, dm), row),
            _resident((1, dm)),
            _resident(w.shape),
            _resident(lb_logits.shape),
            _resident((1, V7X_LANES)),
            pl.BlockSpec((None, None, mt, mw), mem),
            pl.BlockSpec((None, None, mt, mw), mem),
        ],
        out_specs=tuple([pl.BlockSpec((ROW_TILE, aw), row)] * 5 + [pl.BlockSpec((ROW_TILE, mw), row)]),
        compiler_params=pltpu.CompilerParams(
            dimension_semantics=("parallel",),
            vmem_limit_bytes=_vmem_limit(pipelined, _nbytes(w.shape, BF16),
                                         4 * _nbytes((ROW_TILE, aw), F32))),
        name="inproj_a",
    )(h, gain, w, lb_logits, mq_gain, mk, mv)


def _hgrn2_kernel(on_ref, qs_ref, lf_ref, k_ref, v_ref, gate_ref, o_ref, state_ref):
    @pl.when(pl.program_id(2) == 0)
    def _():
        state_ref[...] = jnp.zeros_like(state_ref)

    row = lax.broadcasted_iota(jnp.int32, (CHUNK, HEAD_DIM), 0)
    causal = (lax.broadcasted_iota(jnp.int32, (CHUNK, CHUNK), 0)
              >= lax.broadcasted_iota(jnp.int32, (CHUNK, CHUNK), 1))
    onorm = on_ref[...]

    chunks = [slice(c * CHUNK, (c + 1) * CHUNK) for c in range(qs_ref.shape[0] // CHUNK)]
    q_ins, vs, decays, atts, kvs = [], [], [], [], []
    for rows in chunks:
        b = lf_ref[rows, :]
        shift = 1
        while shift < CHUNK:
            b = b + jnp.where(row >= shift, pltpu.roll(b, shift, axis=0), 0.0)
            shift *= 2
        b_end = b[CHUNK - 1:CHUNK, :]
        k = k_ref[rows, :].astype(F32)
        q_in = (qs_ref[rows, :].astype(F32) * jnp.exp2(b)).astype(BF16)
        k_in = (k * jnp.exp2(-b)).astype(BF16)
        k_out = (k * jnp.exp2(b_end - b)).astype(BF16)
        v = v_ref[rows, :]
        q_ins.append(q_in)
        vs.append(v)
        decays.append(jnp.exp2(b_end))
        atts.append(lax.dot_general(q_in, k_in, _NT, preferred_element_type=F32))
        kvs.append(lax.dot_general(v, k_out, _TN, preferred_element_type=F32))

    state_t = state_ref[...]
    states = []
    for decay, kv in zip(decays, kvs):
        states.append(state_t.astype(BF16))
        state_t = state_t * decay + kv
    state_ref[...] = state_t

    for rows, q_in, v, att, state_in in zip(chunks, q_ins, vs, atts, states):
        att = jnp.where(causal, att, 0.0).astype(BF16)
        o = _dot(att, v) + lax.dot_general(q_in, state_in, _NT, preferred_element_type=F32)
        o_ref[rows, :] = (_rms(o, onorm) * gate_ref[rows, :].astype(F32)).astype(o_ref.dtype)


def _hgrn2(qs, lf, k, v, gate, onorm, bn, seq):
    t, aw = qs.shape
    heads = aw // HEAD_DIM
    steps = seq // HGRN_ROWS
    spec = pl.BlockSpec((HGRN_ROWS, HEAD_DIM), lambda b, h, s: (b * steps + s, h))
    return pl.pallas_call(
        _hgrn2_kernel,
        out_shape=jax.ShapeDtypeStruct((t, aw), BF16),
        grid=(bn, heads, steps),
        in_specs=[pl.BlockSpec((1, HEAD_DIM), lambda b, h, s: (0, h)), spec, spec, spec, spec, spec],
        out_specs=spec,
        scratch_shapes=[pltpu.VMEM((HEAD_DIM, HEAD_DIM), F32)],
        compiler_params=pltpu.CompilerParams(
            dimension_semantics=("parallel", "parallel", "arbitrary")),
        name="hgrn2",
    )(onorm, qs, lf, k, v, gate)


def _mix_ffn_kernel(*refs, dilations):
    n_kv = len(dilations)
    h_ref, o_ref, mo_ref, wo_ref, nf_ref, wgu_ref, wd_ref = refs[:7]
    if n_kv:
        kvn_ref, wkv_ref, kn_ref, cos_ref, sin_ref = refs[7:12]
        h_out = refs[12]
        k_outs = refs[13:13 + n_kv]
        v_outs = refs[13 + n_kv:13 + 2 * n_kv]
        k_slab, v_slab = refs[13 + 2 * n_kv:]
    else:
        h_out = refs[7]
    main_w = o_ref.shape[1]
    hidden = wd_ref.shape[0]
    h = h_ref[...] + _dot(o_ref[...], wo_ref[0:main_w, :]) + _dot(mo_ref[...], wo_ref[main_w:, :])
    hn = _rms(h, nf_ref[...]).astype(BF16)
    for c in range(hidden // FFN_CHUNK):
        cols = slice(c * FFN_CHUNK, (c + 1) * FFN_CHUNK)
        up_cols = slice(hidden + c * FFN_CHUNK, hidden + (c + 1) * FFN_CHUNK)
        act = (_silu(_dot(hn, wgu_ref[:, cols])) * _dot(hn, wgu_ref[:, up_cols])).astype(BF16)
        h = h + _dot(act, wd_ref[cols, :])
    h_out[...] = h
    if n_kv:
        kv_w = wkv_ref.shape[1] // 2
        kn = _rms(h, kvn_ref[...]).astype(BF16)
        k = _dot(kn, wkv_ref[:, 0:kv_w])
        for hd in range(kv_w // HEAD_DIM):
            k_slab[hd] = _rope(_rms(k[:, _head_cols(hd)], kn_ref[...]), cos_ref[...], sin_ref[...])
        v = _dot(kn, wkv_ref[:, kv_w:])
        for hd in range(kv_w // HEAD_DIM):
            v_slab[hd] = v[:, _head_cols(hd)]
        for dil, k_out, v_out in zip(dilations, k_outs, v_outs):
            if dil == 1:
                for hd in range(kv_w // HEAD_DIM):
                    k_out[:, _head_cols(hd)] = k_slab[hd].astype(k_out.dtype)
                v_out[...] = v.astype(v_out.dtype)
            else:
                _store_by_residue(k_slab, k_out, dil)
                _store_by_residue(v_slab, v_out, dil)


def _residue_out(bn, seq, width, dil):
    tiles_per_seq = seq // ROW_TILE
    if dil == 1:
        return (jax.ShapeDtypeStruct((bn * seq, width), BF16),
                pl.BlockSpec((ROW_TILE, width), lambda i: (i, 0)))
    return (jax.ShapeDtypeStruct((bn, dil, seq // dil, width), BF16),
            pl.BlockSpec((None, dil, ROW_TILE // dil, width),
                         lambda i: (i // tiles_per_seq, 0, i % tiles_per_seq, 0)))


def _mix_ffn(h, o, mo, w_out, norm_ffn, w_gate_up, w_down, bn, seq, kv=None):
    t, dm = h.shape
    main_w, mw = o.shape[1], mo.shape[1]
    hidden = w_down.shape[0]
    row = lambda i: (i, 0)
    in_specs = [
        pl.BlockSpec((ROW_TILE, dm), row), pl.BlockSpec((ROW_TILE, main_w), row),
        pl.BlockSpec((ROW_TILE, mw), row),
        _resident(w_out.shape), _resident((1, dm)), _resident(w_gate_up.shape), _resident(w_down.shape),
    ]
    args = [h, o, mo, w_out, norm_ffn, w_gate_up, w_down]
    out_shape = [jax.ShapeDtypeStruct((t, dm), F32)]
    out_specs = [pl.BlockSpec((ROW_TILE, dm), row)]
    scratch = []
    dilations = ()
    resident = _nbytes(w_out.shape, BF16) + _nbytes(w_gate_up.shape, BF16) + _nbytes(w_down.shape, BF16)
    pipelined = 2 * _nbytes((ROW_TILE, dm), F32) + _nbytes((ROW_TILE, main_w + mw), BF16)
    if kv is not None:
        kv_norm, w_kv, k_norm, cos2, sin2 = kv
        kv_w = w_kv.shape[1] // 2
        dilations = tuple(d for _, d in DILATED_GROUPS)
        tiles_per_seq = seq // ROW_TILE
        pos = lambda i: (i % tiles_per_seq, 0)
        in_specs += [_resident((1, dm)), _resident(w_kv.shape), _resident((1, HEAD_DIM)),
                     pl.BlockSpec((ROW_TILE, HEAD_DIM), pos), pl.BlockSpec((ROW_TILE, HEAD_DIM), pos)]
        args += [kv_norm, w_kv, k_norm, cos2, sin2]
        for _ in range(2):
            for dil in dilations:
                shape, spec = _residue_out(bn, seq, kv_w, dil)
                out_shape.append(shape)
                out_specs.append(spec)
        slab = (kv_w // HEAD_DIM, ROW_TILE, HEAD_DIM)
        scratch = [pltpu.VMEM(slab, F32), pltpu.VMEM(slab, F32)]
        resident += _nbytes(w_kv.shape, BF16) + 2 * _nbytes(slab, F32)
        pipelined += (2 * len(dilations) * _nbytes((ROW_TILE, kv_w), BF16)
                      + 2 * _nbytes((ROW_TILE, HEAD_DIM), F32))
    temps = 3 * _nbytes((ROW_TILE, dm), F32) + 3 * _nbytes((ROW_TILE, FFN_CHUNK), F32)
    return pl.pallas_call(
        functools.partial(_mix_ffn_kernel, dilations=dilations),
        out_shape=tuple(out_shape),
        grid=(t // ROW_TILE,),
        in_specs=in_specs,
        out_specs=tuple(out_specs),
        scratch_shapes=scratch,
        compiler_params=pltpu.CompilerParams(
            dimension_semantics=("parallel",),
            vmem_limit_bytes=_vmem_limit(pipelined, resident, temps)),
        name="mix_ffn_kv" if kv is not None else "mix_ffn",
    )(*args)


def _inproj_b_kernel(*refs, dilations):
    n_groups = len(dilations)
    x_ref, gain_ref, w_ref, qn_ref, cos_ref, sin_ref, mqg_ref, mk_ref, mv_ref = refs[:9]
    q_refs = refs[9:9 + n_groups]
    mo_ref = refs[9 + n_groups]
    slabs = refs[10 + n_groups:]
    bw = (w_ref.shape[1] - mo_ref.shape[1]) // n_groups
    xn = _rms(x_ref[...], gain_ref[...]).astype(BF16)
    q_scale = (HEAD_DIM ** -0.5) * LOG2E
    cos_s = cos_ref[...] * q_scale
    sin_s = sin_ref[...] * q_scale
    probs = None
    slab_iter = iter(slabs)
    for gi, (dil, q_ref) in enumerate(zip(dilations, q_refs)):
        qs = _dot(xn, w_ref[:, gi * bw:(gi + 1) * bw])
        if gi == 0:
            probs = _memory_probs(_dot(xn, w_ref[:, n_groups * bw:]), mqg_ref[...], mk_ref)
        if gi == 1:
            _memory_output(probs, mv_ref, mo_ref)
        slab = None if dil == 1 else next(slab_iter)
        for hd in range(bw // HEAD_DIM):
            q = _rope(_rms(qs[:, _head_cols(hd)], qn_ref[gi]), cos_s, sin_s)
            if dil == 1:
                q_ref[:, _head_cols(hd)] = q.astype(q_ref.dtype)
            else:
                slab[hd] = q
        if dil != 1:
            _store_by_residue(slab, q_ref, dil)


def _inproj_b(h, gain, w, q_norm, cos2, sin2, mq_gain, mk, mv, layer, bn, seq):
    t, dm = h.shape
    mt, mw = mk.shape[2], mk.shape[3]
    dilations = tuple(d for _, d in DILATED_GROUPS)
    n_groups = len(dilations)
    bw = (w.shape[1] - mw) // n_groups
    tiles_per_seq = seq // ROW_TILE
    row = lambda i: (i, 0)
    pos = lambda i: (i % tiles_per_seq, 0)
    mem = lambda i: (layer, i // tiles_per_seq, 0, 0)
    q_outs = [_residue_out(bn, seq, bw, dil) for dil in dilations]
    slab = (bw // HEAD_DIM, ROW_TILE, HEAD_DIM)
    n_slabs = sum(1 for dil in dilations if dil != 1)
    pipelined = (_nbytes((ROW_TILE, dm), F32) + n_groups * _nbytes((ROW_TILE, bw), BF16)
                 + 2 * _nbytes((ROW_TILE, HEAD_DIM), F32) + _nbytes((ROW_TILE, mw), BF16)
                 + 2 * _nbytes((mt, mw), BF16))
    return pl.pallas_call(
        functools.partial(_inproj_b_kernel, dilations=dilations),
        out_shape=tuple([shape for shape, _ in q_outs] + [jax.ShapeDtypeStruct((t, mw), BF16)]),
        grid=(t // ROW_TILE,),
        in_specs=[
            pl.BlockSpec((ROW_TILE, dm), row),
            _resident((1, dm)),
            _resident(w.shape),
            _resident((n_groups, 1, HEAD_DIM)),
            pl.BlockSpec((ROW_TILE, HEAD_DIM), pos), pl.BlockSpec((ROW_TILE, HEAD_DIM), pos),
            _resident((1, V7X_LANES)),
            pl.BlockSpec((None, None, mt, mw), mem),
            pl.BlockSpec((None, None, mt, mw), mem),
        ],
        out_specs=tuple([spec for _, spec in q_outs] + [pl.BlockSpec((ROW_TILE, mw), row)]),
        scratch_shapes=[pltpu.VMEM(slab, F32)] * n_slabs,
        compiler_params=pltpu.CompilerParams(
            dimension_semantics=("parallel",),
            vmem_limit_bytes=_vmem_limit(pipelined, _nbytes(w.shape, BF16) + n_slabs * _nbytes(slab, F32),
                                         4 * _nbytes((ROW_TILE, bw), F32))),
        name="inproj_b",
    )(h, gain, w, q_norm.reshape(n_groups, 1, HEAD_DIM), cos2, sin2, mq_gain, mk, mv)


def _dilated_kernel(*refs, dilations):
    n_groups = len(dilations)
    q_refs = refs[:n_groups]
    k_refs = refs[n_groups:2 * n_groups]
    v_refs = refs[2 * n_groups:3 * n_groups]
    o_ref, og_ref, lse_ref, bias_ref = refs[3 * n_groups:]
    seq = o_ref.shape[0]
    blk = ATTN_BLOCK

    qi = lax.broadcasted_iota(jnp.int32, (blk, 2 * blk), 0)
    kj = lax.broadcasted_iota(jnp.int32, (blk, 2 * blk), 1)
    band = (kj >= qi) & (kj <= qi + blk)
    bias_ref[0] = jnp.where(band & (kj >= blk), 0.0, -jnp.inf)
    bias_ref[1] = jnp.where(band, 0.0, -jnp.inf)
    bias_ref[2] = jnp.where(kj <= qi, 0.0, -jnp.inf)

    for gi, dil in enumerate(dilations):
        q_ref, k_ref, v_ref = q_refs[gi], k_refs[gi], v_refs[gi]
        n_blocks = seq // (blk * dil)

        def blocks(step, carry, q_ref=q_ref, k_ref=k_ref, v_ref=v_ref, gi=gi, dil=dil, n_blocks=n_blocks):
            idxs = [step * ATTN_UNROLL + u for u in range(ATTN_UNROLL)]
            starts = [pl.multiple_of(idx * blk, blk) for idx in idxs]
            windows = [pl.ds(pl.multiple_of(jnp.maximum(start - blk, 0), blk), 2 * blk) for start in starts]
            scores = [lax.dot_general(q_ref[pl.ds(start, blk), :], k_ref[window, :], _NT,
                                      preferred_element_type=F32)
                      for start, window in zip(starts, windows)]
            soft = []
            for idx, s in zip(idxs, scores):
                n = lax.rem(idx, n_blocks)
                s = s + bias_ref[jnp.where(idx == 0, 2, jnp.minimum(n, 1))]
                m = jnp.max(s, axis=-1, keepdims=True)
                p = jnp.exp2(s - m)
                denom = jnp.sum(p, axis=-1, keepdims=True)
                soft.append((p.astype(BF16), m, denom))
            for idx, start, window, (p, m, denom) in zip(idxs, starts, windows, soft):
                o = _dot(p, v_ref[window, :]) / denom
                lse2 = jnp.broadcast_to(m + jnp.log(denom) * LOG2E, (blk, HEAD_DIM))
                if dil == 1:
                    out_rows = pl.ds(start, blk)
                else:
                    n = lax.rem(idx, n_blocks)
                    out_rows = pl.ds(n * (blk * dil) + lax.div(idx, n_blocks), blk, stride=dil)
                og_ref[gi, out_rows, :] = o
                lse_ref[gi, out_rows, :] = lse2
            return carry

        lax.fori_loop(0, dil * n_blocks // ATTN_UNROLL, blocks, 0)

    def merge(step, carry):
        rows = pl.ds(pl.multiple_of(step * MERGE_ROWS, MERGE_ROWS), MERGE_ROWS)
        lses = [lse_ref[gi, rows, :] for gi in range(n_groups)]
        top = functools.reduce(jnp.maximum, lses)
        ws = [jnp.exp2(l - top) for l in lses]
        acc = sum(w * og_ref[gi, rows, :] for gi, w in enumerate(ws))
        o_ref[rows, :] = (acc / sum(ws)).astype(o_ref.dtype)
        return carry

    lax.fori_loop(0, seq // MERGE_ROWS, merge, 0)


def _dilated_attention(qs, ks, vs, bn, seq):
    width = qs[0].shape[-1]
    heads = width // HEAD_DIM
    dilations = tuple(d for _, d in DILATED_GROUPS)
    n_groups = len(dilations)
    as_seq = lambda a: a.reshape(bn, seq, width)
    spec = pl.BlockSpec((None, seq, HEAD_DIM), lambda b, h: (b, 0, h))
    seq_bf16 = _nbytes((seq, HEAD_DIM), BF16)
    seq_f32 = _nbytes((seq, HEAD_DIM), F32)
    n_bias = 3
    scratch = 2 * n_groups * seq_f32 + _nbytes((n_bias, ATTN_BLOCK, 2 * ATTN_BLOCK), F32)
    out = pl.pallas_call(
        functools.partial(_dilated_kernel, dilations=dilations),
        out_shape=jax.ShapeDtypeStruct((bn, seq, width), BF16),
        grid=(bn, heads),
        in_specs=[spec] * (3 * n_groups),
        out_specs=spec,
        scratch_shapes=[
            pltpu.VMEM((n_groups, seq, HEAD_DIM), F32),
            pltpu.VMEM((n_groups, seq, HEAD_DIM), F32),
            pltpu.VMEM((n_bias, ATTN_BLOCK, 2 * ATTN_BLOCK), F32),
        ],
        compiler_params=pltpu.CompilerParams(
            dimension_semantics=("parallel", "parallel"),
            vmem_limit_bytes=_vmem_limit((3 * n_groups + 1) * seq_bf16, scratch,
                                         8 * _nbytes((MERGE_ROWS, HEAD_DIM), F32))),
        name="dilated_attention",
    )(*[as_seq(a) for a in (*qs, *ks, *vs)])
    return out.reshape(bn * seq, width)


def _rope_tables(seq):
    half = HEAD_DIM // 2
    inv = ROPE_THETA ** (-jnp.arange(half, dtype=F32) / half)
    ang = jnp.arange(seq).astype(F32)[:, None] * inv[None, :]
    cos, sin = jnp.cos(ang), jnp.sin(ang)
    return jnp.concatenate([cos, cos], axis=-1), jnp.concatenate([-sin, sin], axis=-1)


def kernel(x, mem, norm_mix, norm_ffn, a_w_in, a_lb_logits, a_onorm, b_w_in, b_qnorm, kv_norm, w_kv,
           b_knorm, mem_norm, w_mem_kv, mem_qnorm, mem_knorm, w_out, w_gate_up, w_down):
    bn, seq, dm = x.shape
    depth = norm_mix.shape[0]
    n_a = a_w_in.shape[0]
    max_dil = max(d for _, d in DILATED_GROUPS)
    assert seq % (ATTN_BLOCK * max_dil) == 0 and seq % HGRN_ROWS == 0 and seq % ROW_TILE == 0
    assert seq % (ATTN_BLOCK * ATTN_UNROLL) == 0 and HGRN_ROWS % CHUNK == 0 and ROW_TILE % max_dil == 0
    assert all(w == ATTN_BLOCK * d for w, d in DILATED_GROUPS)

    bf = lambda a: a.astype(BF16)
    row_vec = lambda a: a.reshape(1, -1)
    pair = lambda a: jnp.concatenate([a, a], axis=-1).reshape(1, V7X_LANES)
    cos2, sin2 = _rope_tables(seq)
    n_groups = len(DILATED_GROUPS)

    mk, mv = _mem_kv(mem, mem_norm, bf(w_mem_kv), mem_knorm)
    h = x.reshape(bn * seq, dm)
    ks = vs = None
    for l in range(depth):
        gain = row_vec(norm_mix[l])
        if l < n_a:
            qs, lf, k, v, gate, mo = _inproj_a(h, gain, bf(a_w_in[l]), a_lb_logits, pair(mem_qnorm[l]),
                                               mk, mv, l, seq)
            o = _hgrn2(qs, lf, k, v, gate, row_vec(a_onorm[l]), bn, seq)
        else:
            j = l - n_a
            *q_groups, mo = _inproj_b(h, gain, bf(b_w_in[j]), b_qnorm[j], cos2, sin2,
                                      pair(mem_qnorm[l]), mk, mv, l, bn, seq)
            o = _dilated_attention(q_groups, ks, vs, bn, seq)
        kv = None
        if l == n_a - 1:
            kv = (row_vec(kv_norm), bf(w_kv), row_vec(b_knorm), cos2, sin2)
        outs = _mix_ffn(h, o, mo, bf(w_out[l]), row_vec(norm_ffn[l]), bf(w_gate_up[l]), bf(w_down[l]),
                        bn, seq, kv)
        h = outs[0]
        if kv is not None:
            ks, vs = outs[1:1 + n_groups], outs[1 + n_groups:1 + 2 * n_groups]
    return h.reshape(bn, seq, dm)
```

```python
import functools
import math

import jax
import jax.numpy as jnp
from jax import lax
from jax.experimental import pallas as pl
from jax.experimental.pallas import tpu as pltpu

F32 = jnp.float32
BF16 = jnp.bfloat16

EPS = 1e-6
HEAD_DIM = 128
CHUNK = 64
MEM_HEAD_DIM = 64
DILATED_GROUPS = ((128, 1), (512, 4), (2048, 16))
ROPE_THETA = 10000.0
LOG2E = math.log2(math.e)

V7X_LANES = 128
V7X_VMEM_SCOPED_MAX_BYTES = 60000 * 1024

ROW_TILE = 256
ATTN_IN_TILE = 512
FFN_CHUNK = 2816
HGRN_ROWS = 1024
ATTN_BLOCK = 128
MERGE_ROWS = 512
ATTN_UNROLL = 16

_NT = (((1,), (1,)), ((), ()))
_TN = (((0,), (0,)), ((), ()))


def _vmem_limit(pipelined_bytes, resident_bytes, temp_bytes):
    need = 2 * pipelined_bytes + resident_bytes + temp_bytes
    return int(min(max(need, 16 * 1024 * 1024), V7X_VMEM_SCOPED_MAX_BYTES))


def _nbytes(shape, dtype):
    n = 1
    for s in shape:
        n *= s
    return n * jnp.dtype(dtype).itemsize


def _resident(shape):
    zeros = (0,) * len(shape)
    return pl.BlockSpec(shape, lambda *_: zeros, pipeline_mode=pl.Buffered(1))


def _dot(a, b):
    return jnp.dot(a, b, preferred_element_type=F32)


def _rms(x, gain):
    ms = jnp.mean(x * x, axis=-1, keepdims=True)
    return x * lax.rsqrt(ms + EPS) * gain


def _silu(x):
    return x * jax.nn.sigmoid(x)


def _head_cols(hd):
    return slice(hd * HEAD_DIM, (hd + 1) * HEAD_DIM)


def _rms_head_pairs(x, gain):
    lo = lax.broadcasted_iota(jnp.int32, x.shape, 1) < MEM_HEAD_DIM
    x2 = x * x
    s_lo = jnp.sum(jnp.where(lo, x2, 0.0), axis=-1, keepdims=True)
    s_hi = jnp.sum(jnp.where(lo, 0.0, x2), axis=-1, keepdims=True)
    ms = jnp.where(lo, s_lo, s_hi) * (1.0 / MEM_HEAD_DIM)
    return x * lax.rsqrt(ms + EPS) * gain


def _rope(x, cos2, sin2):
    return x * cos2 + pltpu.roll(x, HEAD_DIM // 2, axis=1) * sin2


def _store_by_residue(slab_ref, out_ref, dil, row0):
    rows = slab_ref.shape[1] // dil
    dst = slice(row0 // dil, row0 // dil + rows)
    for r in range(dil):
        for hd in range(slab_ref.shape[0]):
            out_ref[r, dst, _head_cols(hd)] = slab_ref[hd, pl.ds(r, rows, stride=dil), :].astype(out_ref.dtype)


def _memory_probs(mq, qgain, mk_ref):
    scaled_gain = qgain * (MEM_HEAD_DIM ** -0.5)
    probs = []
    for t in range(mq.shape[1] // V7X_LANES):
        cols = slice(t * V7X_LANES, (t + 1) * V7X_LANES)
        qn = _rms_head_pairs(mq[:, cols], scaled_gain)
        lo = lax.broadcasted_iota(jnp.int32, qn.shape, 1) < MEM_HEAD_DIM
        for keep in (lo, jnp.logical_not(lo)):
            qh = jnp.where(keep, qn, 0.0).astype(BF16)
            s = lax.dot_general(qh, mk_ref[:, cols], _NT, preferred_element_type=F32)
            p = jnp.exp(s - jnp.max(s, axis=-1, keepdims=True))
            probs.append((p.astype(BF16), jnp.sum(p, axis=-1, keepdims=True)))
    return probs


def _memory_output(probs, mv_ref, mo_ref, rows=slice(None)):
    for t in range(mo_ref.shape[1] // V7X_LANES):
        cols = slice(t * V7X_LANES, (t + 1) * V7X_LANES)
        outs = [_dot(p, mv_ref[:, cols]) / denom for p, denom in probs[2 * t:2 * t + 2]]
        lo = lax.broadcasted_iota(jnp.int32, outs[0].shape, 1) < MEM_HEAD_DIM
        mo_ref[rows, cols] = jnp.where(lo, outs[0], outs[1]).astype(mo_ref.dtype)


def _mem_kv_kernel(mem_ref, gain_ref, w_ref, kgain_ref, mk_ref, mv_ref):
    mw = mk_ref.shape[1]
    mn = _rms(mem_ref[...], gain_ref[...]).astype(BF16)
    kv = _dot(mn, w_ref[...])
    for t in range(mw // V7X_LANES):
        cols = slice(t * V7X_LANES, (t + 1) * V7X_LANES)
        mk_ref[:, cols] = _rms_head_pairs(kv[:, cols], kgain_ref[...]).astype(mk_ref.dtype)
    mv_ref[...] = kv[:, mw:].astype(mv_ref.dtype)


def _mem_kv(mem, mem_norm, w_mem_kv, mem_knorm):
    bn, mt, dm = mem.shape
    depth = w_mem_kv.shape[0]
    mw = w_mem_kv.shape[2] // 2
    kgain = jnp.concatenate([mem_knorm, mem_knorm], axis=-1).reshape(depth, 1, V7X_LANES)
    out = jax.ShapeDtypeStruct((depth, bn, mt, mw), BF16)
    return pl.pallas_call(
        _mem_kv_kernel,
        out_shape=(out, out),
        grid=(depth, bn),
        in_specs=[
            pl.BlockSpec((None, mt, dm), lambda l, b: (b, 0, 0)),
            pl.BlockSpec((None, 1, dm), lambda l, b: (l, 0, 0)),
            pl.BlockSpec((None, dm, 2 * mw), lambda l, b: (l, 0, 0)),
            pl.BlockSpec((None, 1, V7X_LANES), lambda l, b: (l, 0, 0)),
        ],
        out_specs=(
            pl.BlockSpec((None, None, mt, mw), lambda l, b: (l, b, 0, 0)),
            pl.BlockSpec((None, None, mt, mw), lambda l, b: (l, b, 0, 0)),
        ),
        name="mem_kv",
    )(mem, mem_norm.reshape(depth, 1, dm), w_mem_kv, kgain)


def _inproj_a_kernel(x_ref, gain_ref, w_ref, lbl_ref, mqg_ref, mk_ref, mv_ref,
                     qs_ref, lf_ref, k_ref, v_ref, gate_ref, mo_ref, *, layer):
    aw = qs_ref.shape[1]
    xn = _rms(x_ref[...], gain_ref[...]).astype(BF16)
    lg = lbl_ref[...]
    e = jnp.exp(lg - jnp.max(lg, axis=0, keepdims=True))
    lb = jnp.sum(e[:layer + 1], axis=0, keepdims=True) / jnp.sum(e, axis=0, keepdims=True)
    probs = _memory_probs(_dot(xn, w_ref[:, 4 * aw:]), mqg_ref[...], mk_ref)
    qs_ref[...] = _silu(_dot(xn, w_ref[:, 0:aw])).astype(qs_ref.dtype)
    _memory_output(probs, mv_ref, mo_ref)
    f = lb + (1.0 - lb) * jax.nn.sigmoid(_dot(xn, w_ref[:, aw:2 * aw]))
    lf_ref[...] = jnp.log(f) * LOG2E
    k_ref[...] = (1.0 - f).astype(k_ref.dtype)
    v_ref[...] = _dot(xn, w_ref[:, 2 * aw:3 * aw]).astype(v_ref.dtype)
    gate_ref[...] = _silu(_dot(xn, w_ref[:, 3 * aw:4 * aw])).astype(gate_ref.dtype)


def _inproj_a(h, gain, w, lb_logits, mq_gain, mk, mv, layer, seq):
    t, dm = h.shape
    mt, mw = mk.shape[2], mk.shape[3]
    aw = (w.shape[1] - mw) // 4
    tiles_per_seq = seq // ROW_TILE
    row = lambda i: (i, 0)
    mem = lambda i: (layer, i // tiles_per_seq, 0, 0)
    wide = functools.partial(jax.ShapeDtypeStruct, (t, aw))
    pipelined = (_nbytes((ROW_TILE, dm), F32) + _nbytes((ROW_TILE, aw), F32)
                 + 4 * _nbytes((ROW_TILE, aw), BF16) + _nbytes((ROW_TILE, mw), BF16)
                 + 2 * _nbytes((mt, mw), BF16))
    return pl.pallas_call(
        functools.partial(_inproj_a_kernel, layer=layer),
        out_shape=(wide(BF16), wide(F32), wide(BF16), wide(BF16), wide(BF16),
                   jax.ShapeDtypeStruct((t, mw), BF16)),
        grid=(t // ROW_TILE,),
        in_specs=[
            pl.BlockSpec((ROW_TILE, dm), row),
            _resident((1, dm)),
            _resident(w.shape),
            _resident(lb_logits.shape),
            _resident((1, V7X_LANES)),
            pl.BlockSpec((None, None, mt, mw), mem),
            pl.BlockSpec((None, None, mt, mw), mem),
        ],
        out_specs=tuple([pl.BlockSpec((ROW_TILE, aw), row)] * 5 + [pl.BlockSpec((ROW_TILE, mw), row)]),
        compiler_params=pltpu.CompilerParams(
            dimension_semantics=("parallel",),
            vmem_limit_bytes=_vmem_limit(pipelined, _nbytes(w.shape, BF16),
                                         4 * _nbytes((ROW_TILE, aw), F32))),
        name="inproj_a",
    )(h, gain, w, lb_logits, mq_gain, mk, mv)


def _hgrn2_kernel(on_ref, qs_ref, lf_ref, k_ref, v_ref, gate_ref, o_ref, state_ref):
    @pl.when(pl.program_id(2) == 0)
    def _():
        state_ref[...] = jnp.zeros_like(state_ref)

    row = lax.broadcasted_iota(jnp.int32, (CHUNK, HEAD_DIM), 0)
    causal = (lax.broadcasted_iota(jnp.int32, (CHUNK, CHUNK), 0)
              >= lax.broadcasted_iota(jnp.int32, (CHUNK, CHUNK), 1))
    onorm = on_ref[...]

    chunks = [slice(c * CHUNK, (c + 1) * CHUNK) for c in range(qs_ref.shape[0] // CHUNK)]
    q_ins, vs, decays, atts, kvs = [], [], [], [], []
    for rows in chunks:
        b = lf_ref[rows, :]
        shift = 1
        while shift < CHUNK:
            b = b + jnp.where(row >= shift, pltpu.roll(b, shift, axis=0), 0.0)
            shift *= 2
        b_end = b[CHUNK - 1:CHUNK, :]
        k = k_ref[rows, :].astype(F32)
        q_in = (qs_ref[rows, :].astype(F32) * jnp.exp2(b)).astype(BF16)
        k_in = (k * jnp.exp2(-b)).astype(BF16)
        k_out = (k * jnp.exp2(b_end - b)).astype(BF16)
        v = v_ref[rows, :]
        q_ins.append(q_in)
        vs.append(v)
        decays.append(jnp.exp2(b_end))
        atts.append(lax.dot_general(q_in, k_in, _NT, preferred_element_type=F32))
        kvs.append(lax.dot_general(v, k_out, _TN, preferred_element_type=F32))

    state_t = state_ref[...]
    states = []
    for decay, kv in zip(decays, kvs):
        states.append(state_t.astype(BF16))
        state_t = state_t * decay + kv
    state_ref[...] = state_t

    for rows, q_in, v, att, state_in in zip(chunks, q_ins, vs, atts, states):
        att = jnp.where(causal, att, 0.0).astype(BF16)
        o = _dot(att, v) + lax.dot_general(q_in, state_in, _NT, preferred_element_type=F32)
        o_ref[rows, :] = (_rms(o, onorm) * gate_ref[rows, :].astype(F32)).astype(o_ref.dtype)


def _hgrn2(qs, lf, k, v, gate, onorm, bn, seq):
    t, aw = qs.shape
    heads = aw // HEAD_DIM
    steps = seq // HGRN_ROWS
    spec = pl.BlockSpec((HGRN_ROWS, HEAD_DIM), lambda b, h, s: (b * steps + s, h))
    return pl.pallas_call(
        _hgrn2_kernel,
        out_shape=jax.ShapeDtypeStruct((t, aw), BF16),
        grid=(bn, heads, steps),
        in_specs=[pl.BlockSpec((1, HEAD_DIM), lambda b, h, s: (0, h)), spec, spec, spec, spec, spec],
        out_specs=spec,
        scratch_shapes=[pltpu.VMEM((HEAD_DIM, HEAD_DIM), F32)],
        compiler_params=pltpu.CompilerParams(
            dimension_semantics=("parallel", "parallel", "arbitrary")),
        name="hgrn2",
    )(onorm, qs, lf, k, v, gate)


def _mix_ffn_kernel(h_ref, o_ref, mo_ref, wo_ref, nf_ref, wgu_ref, wd_ref, h_out):
    main_w = o_ref.shape[1]
    hidden = wd_ref.shape[0]
    h = h_ref[...] + _dot(o_ref[...], wo_ref[0:main_w, :]) + _dot(mo_ref[...], wo_ref[main_w:, :])
    hn = _rms(h, nf_ref[...]).astype(BF16)
    for c in range(hidden // FFN_CHUNK):
        cols = slice(c * FFN_CHUNK, (c + 1) * FFN_CHUNK)
        up_cols = slice(hidden + c * FFN_CHUNK, hidden + (c + 1) * FFN_CHUNK)
        act = (_silu(_dot(hn, wgu_ref[:, cols])) * _dot(hn, wgu_ref[:, up_cols])).astype(BF16)
        h = h + _dot(act, wd_ref[cols, :])
    h_out[...] = h


def _layer_resident(stacked, layer):
    zeros = (0,) * (stacked.ndim - 1)
    return pl.BlockSpec((None,) + stacked.shape[1:], lambda *_: (layer,) + zeros,
                        pipeline_mode=pl.Buffered(1))


def _mix_ffn(h, o, mo, w_out, norm_ffn, w_gate_up, w_down, layer):
    t, dm = h.shape
    main_w, mw = o.shape[1], mo.shape[1]
    row = lambda i: (i, 0)
    resident = sum(_nbytes(w.shape[1:], BF16) for w in (w_out, w_gate_up, w_down))
    pipelined = 2 * _nbytes((ROW_TILE, dm), F32) + _nbytes((ROW_TILE, main_w + mw), BF16)
    temps = 3 * _nbytes((ROW_TILE, dm), F32) + 3 * _nbytes((ROW_TILE, FFN_CHUNK), F32)
    return pl.pallas_call(
        _mix_ffn_kernel,
        out_shape=jax.ShapeDtypeStruct((t, dm), F32),
        grid=(t // ROW_TILE,),
        in_specs=[
            pl.BlockSpec((ROW_TILE, dm), row), pl.BlockSpec((ROW_TILE, main_w), row),
            pl.BlockSpec((ROW_TILE, mw), row),
            _layer_resident(w_out, layer), _resident((1, dm)), _layer_resident(w_gate_up, layer),
            _layer_resident(w_down, layer),
        ],
        out_specs=pl.BlockSpec((ROW_TILE, dm), row),
        compiler_params=pltpu.CompilerParams(
            dimension_semantics=("parallel",),
            vmem_limit_bytes=_vmem_limit(pipelined, resident, temps)),
        name="mix_ffn",
    )(h, o, mo, w_out, norm_ffn, w_gate_up, w_down)


def _inproj_b_kernel(*refs, dilations):
    n_groups = len(dilations)
    (x_ref, gain_ref, kvg_ref, w_ref, wkv_ref, qn_ref, kn_ref, cos_ref, sin_ref,
     mqg_ref, mk_ref, mv_ref) = refs[:12]
    q_refs = refs[12:12 + n_groups]
    k_refs = refs[12 + n_groups:12 + 2 * n_groups]
    v_refs = refs[12 + 2 * n_groups:12 + 3 * n_groups]
    mo_ref = refs[12 + 3 * n_groups]
    xn_scr, kn_scr = refs[-2:]
    slabs = refs[13 + 3 * n_groups:-2]
    q_slabs = dict(zip([d for d in dilations if d != 1], slabs[:-2]))
    k_slab, v_slab = slabs[-2:]
    kv_w = wkv_ref.shape[1] // 2
    bw = (w_ref.shape[1] - mo_ref.shape[1]) // n_groups
    heads = range(bw // HEAD_DIM)
    pair = 2
    q_scale = (HEAD_DIM ** -0.5) * LOG2E

    for row0 in range(0, x_ref.shape[0], ROW_TILE):
        rows = slice(row0, row0 + ROW_TILE)
        x = x_ref[rows, :]
        xhat = x * lax.rsqrt(jnp.mean(x * x, axis=-1, keepdims=True) + EPS)
        xn_scr[rows, :] = (xhat * gain_ref[...]).astype(BF16)
        kn_scr[rows, :] = (xhat * kvg_ref[...]).astype(BF16)
        cos, sin = cos_ref[rows, :], sin_ref[rows, :]
        cos_s, sin_s = cos * q_scale, sin * q_scale

        def queries(gi, row0=row0, rows=rows, cos_s=cos_s, sin_s=sin_s):
            dil, q_ref = dilations[gi], q_refs[gi]
            for hd in heads:
                if hd % pair == 0:
                    col0 = gi * bw + hd * HEAD_DIM
                    qs = _dot(xn_scr[rows, :], w_ref[:, col0:col0 + pair * HEAD_DIM])
                q = _rope(_rms(qs[:, _head_cols(hd % pair)], qn_ref[gi]), cos_s, sin_s)
                if dil == 1:
                    q_ref[rows, _head_cols(hd)] = q.astype(q_ref.dtype)
                else:
                    q_slabs[dil][hd] = q
            if dil != 1:
                _store_by_residue(q_slabs[dil], q_ref, dil, row0)

        def copies(slab, out_refs, row0=row0, rows=rows):
            for dil, out_ref in zip(dilations, out_refs):
                if dil == 1:
                    for hd in heads:
                        out_ref[rows, _head_cols(hd)] = slab[hd].astype(out_ref.dtype)
                else:
                    _store_by_residue(slab, out_ref, dil, row0)

        order = sorted(range(n_groups), key=lambda gi: -dilations[gi])
        queries(order[0])
        probs = _memory_probs(_dot(xn_scr[rows, :], w_ref[:, n_groups * bw:]), mqg_ref[...], mk_ref)
        for hd in heads:
            if hd % pair == 0:
                k = _dot(kn_scr[rows, :], wkv_ref[:, hd * HEAD_DIM:(hd + pair) * HEAD_DIM])
            k_slab[hd] = _rope(_rms(k[:, _head_cols(hd % pair)], kn_ref[...]), cos, sin)
        copies(k_slab, k_refs)
        for gi in order[1:-1]:
            queries(gi)
        _memory_output(probs, mv_ref, mo_ref, rows)
        for hd in heads:
            if hd % pair == 0:
                v = _dot(kn_scr[rows, :], wkv_ref[:, kv_w + hd * HEAD_DIM:kv_w + (hd + pair) * HEAD_DIM])
            v_slab[hd] = v[:, _head_cols(hd % pair)]
        copies(v_slab, v_refs)
        queries(order[-1])


def _residue_out(bn, seq, width, dil):
    tiles_per_seq = seq // ATTN_IN_TILE
    if dil == 1:
        return (jax.ShapeDtypeStruct((bn * seq, width), BF16),
                pl.BlockSpec((ATTN_IN_TILE, width), lambda i: (i, 0)))
    return (jax.ShapeDtypeStruct((bn, dil, seq // dil, width), BF16),
            pl.BlockSpec((None, dil, ATTN_IN_TILE // dil, width),
                         lambda i: (i // tiles_per_seq, 0, i % tiles_per_seq, 0)))


def _inproj_b(h, gain, kv_gain, w, w_kv, q_norm, k_norm, cos2, sin2, mq_gain, mk, mv, layer, bn, seq):
    t, dm = h.shape
    mt, mw = mk.shape[2], mk.shape[3]
    dilations = tuple(d for _, d in DILATED_GROUPS)
    n_groups = len(dilations)
    bw = (w.shape[1] - mw) // n_groups
    kv_w = w_kv.shape[1] // 2
    assert kv_w == bw
    tile = ATTN_IN_TILE
    tiles_per_seq = seq // tile
    row = lambda i: (i, 0)
    pos = lambda i: (i % tiles_per_seq, 0)
    mem = lambda i: (layer, i // tiles_per_seq, 0, 0)
    copies = [_residue_out(bn, seq, bw, dil) for dil in dilations] * 3
    slab = (bw // HEAD_DIM, ROW_TILE, HEAD_DIM)
    n_slabs = sum(1 for dil in dilations if dil != 1) + 2
    pipelined = (_nbytes((tile, dm), F32) + 3 * n_groups * _nbytes((tile, bw), BF16)
                 + 2 * _nbytes((tile, HEAD_DIM), F32) + _nbytes((tile, mw), BF16)
                 + 2 * _nbytes((mt, mw), BF16))
    resident = (_nbytes(w.shape, BF16) + _nbytes(w_kv.shape, BF16) + n_slabs * _nbytes(slab, F32)
                + 2 * _nbytes((tile, dm), BF16))
    return pl.pallas_call(
        functools.partial(_inproj_b_kernel, dilations=dilations),
        out_shape=tuple([shape for shape, _ in copies] + [jax.ShapeDtypeStruct((t, mw), BF16)]),
        grid=(t // tile,),
        in_specs=[
            pl.BlockSpec((tile, dm), row),
            _resident((1, dm)), _resident((1, dm)),
            _resident(w.shape), _resident(w_kv.shape),
            _resident((n_groups, 1, HEAD_DIM)), _resident((1, HEAD_DIM)),
            pl.BlockSpec((tile, HEAD_DIM), pos), pl.BlockSpec((tile, HEAD_DIM), pos),
            _resident((1, V7X_LANES)),
            pl.BlockSpec((None, None, mt, mw), mem),
            pl.BlockSpec((None, None, mt, mw), mem),
        ],
        out_specs=tuple([spec for _, spec in copies] + [pl.BlockSpec((tile, mw), row)]),
        scratch_shapes=[pltpu.VMEM(slab, F32)] * n_slabs + [pltpu.VMEM((tile, dm), BF16)] * 2,
        compiler_params=pltpu.CompilerParams(
            dimension_semantics=("parallel",),
            vmem_limit_bytes=_vmem_limit(pipelined, resident, 6 * _nbytes((ROW_TILE, bw), F32))),
        name="inproj_b",
    )(h, gain, kv_gain, w, w_kv, q_norm.reshape(n_groups, 1, HEAD_DIM), k_norm, cos2, sin2,
      mq_gain, mk, mv)


def _dilated_kernel(*refs, dilations):
    n_groups = len(dilations)
    q_refs = refs[:n_groups]
    k_refs = refs[n_groups:2 * n_groups]
    v_refs = refs[2 * n_groups:3 * n_groups]
    o_ref, og_ref, lse_ref, bias_ref = refs[3 * n_groups:]
    seq = o_ref.shape[0]
    blk = ATTN_BLOCK

    qi = lax.broadcasted_iota(jnp.int32, (blk, 2 * blk), 0)
    kj = lax.broadcasted_iota(jnp.int32, (blk, 2 * blk), 1)
    band = (kj >= qi) & (kj <= qi + blk)
    bias_ref[0] = jnp.where(band & (kj >= blk), 0.0, -jnp.inf)
    bias_ref[1] = jnp.where(band, 0.0, -jnp.inf)
    bias_ref[2] = jnp.where(kj <= qi, 0.0, -jnp.inf)

    for gi, dil in enumerate(dilations):
        q_ref, k_ref, v_ref = q_refs[gi], k_refs[gi], v_refs[gi]
        n_blocks = seq // (blk * dil)

        def blocks(step, carry, q_ref=q_ref, k_ref=k_ref, v_ref=v_ref, gi=gi, dil=dil, n_blocks=n_blocks):
            idxs = [step * ATTN_UNROLL + u for u in range(ATTN_UNROLL)]
            starts = [pl.multiple_of(idx * blk, blk) for idx in idxs]
            windows = [pl.ds(pl.multiple_of(jnp.maximum(start - blk, 0), blk), 2 * blk) for start in starts]
            scores = [lax.dot_general(q_ref[pl.ds(start, blk), :], k_ref[window, :], _NT,
                                      preferred_element_type=F32)
                      for start, window in zip(starts, windows)]
            soft = []
            for idx, s in zip(idxs, scores):
                n = lax.rem(idx, n_blocks)
                s = s + bias_ref[jnp.where(idx == 0, 2, jnp.minimum(n, 1))]
                m = jnp.max(s, axis=-1, keepdims=True)
                p = jnp.exp2(s - m)
                denom = jnp.sum(p, axis=-1, keepdims=True)
                soft.append((p.astype(BF16), m, denom))
            for idx, start, window, (p, m, denom) in zip(idxs, starts, windows, soft):
                o = _dot(p, v_ref[window, :]) / denom
                lse2 = jnp.broadcast_to(m + jnp.log(denom) * LOG2E, (blk, HEAD_DIM))
                if dil == 1:
                    out_rows = pl.ds(start, blk)
                else:
                    n = lax.rem(idx, n_blocks)
                    out_rows = pl.ds(n * (blk * dil) + lax.div(idx, n_blocks), blk, stride=dil)
                og_ref[gi, out_rows, :] = o
                lse_ref[gi, out_rows, :] = lse2
            return carry

        lax.fori_loop(0, dil * n_blocks // ATTN_UNROLL, blocks, 0)

    def merge(step, carry):
        rows = pl.ds(pl.multiple_of(step * MERGE_ROWS, MERGE_ROWS), MERGE_ROWS)
        lses = [lse_ref[gi, rows, :] for gi in range(n_groups)]
        top = functools.reduce(jnp.maximum, lses)
        ws = [jnp.exp2(l - top) for l in lses]
        acc = sum(w * og_ref[gi, rows, :] for gi, w in enumerate(ws))
        o_ref[rows, :] = (acc / sum(ws)).astype(o_ref.dtype)
        return carry

    lax.fori_loop(0, seq // MERGE_ROWS, merge, 0)


def _dilated_attention(qs, ks, vs, bn, seq):
    width = qs[0].shape[-1]
    heads = width // HEAD_DIM
    dilations = tuple(d for _, d in DILATED_GROUPS)
    n_groups = len(dilations)
    as_seq = lambda a: a.reshape(bn, seq, width)
    spec = pl.BlockSpec((None, seq, HEAD_DIM), lambda b, h: (b, 0, h))
    seq_bf16 = _nbytes((seq, HEAD_DIM), BF16)
    seq_f32 = _nbytes((seq, HEAD_DIM), F32)
    n_bias = 3
    scratch = 2 * n_groups * seq_f32 + _nbytes((n_bias, ATTN_BLOCK, 2 * ATTN_BLOCK), F32)
    out = pl.pallas_call(
        functools.partial(_dilated_kernel, dilations=dilations),
        out_shape=jax.ShapeDtypeStruct((bn, seq, width), BF16),
        grid=(bn, heads),
        in_specs=[spec] * (3 * n_groups),
        out_specs=spec,
        scratch_shapes=[
            pltpu.VMEM((n_groups, seq, HEAD_DIM), F32),
            pltpu.VMEM((n_groups, seq, HEAD_DIM), F32),
            pltpu.VMEM((n_bias, ATTN_BLOCK, 2 * ATTN_BLOCK), F32),
        ],
        compiler_params=pltpu.CompilerParams(
            dimension_semantics=("parallel", "parallel"),
            vmem_limit_bytes=_vmem_limit((3 * n_groups + 1) * seq_bf16, scratch,
                                         8 * _nbytes((MERGE_ROWS, HEAD_DIM), F32))),
        name="dilated_attention",
    )(*[as_seq(a) for a in (*qs, *ks, *vs)])
    return out.reshape(bn * seq, width)


def _rope_tables(seq):
    half = HEAD_DIM // 2
    inv = ROPE_THETA ** (-jnp.arange(half, dtype=F32) / half)
    ang = jnp.arange(seq).astype(F32)[:, None] * inv[None, :]
    cos, sin = jnp.cos(ang), jnp.sin(ang)
    return jnp.concatenate([cos, cos], axis=-1), jnp.concatenate([-sin, sin], axis=-1)


def kernel(x, mem, norm_mix, norm_ffn, a_w_in, a_lb_logits, a_onorm, b_w_in, b_qnorm, kv_norm, w_kv,
           b_knorm, mem_norm, w_mem_kv, mem_qnorm, mem_knorm, w_out, w_gate_up, w_down):
    bn, seq, dm = x.shape
    depth = norm_mix.shape[0]
    n_a = a_w_in.shape[0]
    max_dil = max(d for _, d in DILATED_GROUPS)
    assert seq % (ATTN_BLOCK * max_dil) == 0 and seq % HGRN_ROWS == 0 and seq % ROW_TILE == 0
    assert seq % (ATTN_BLOCK * ATTN_UNROLL) == 0 and HGRN_ROWS % CHUNK == 0 and ROW_TILE % max_dil == 0
    assert all(w == ATTN_BLOCK * d for w, d in DILATED_GROUPS)
    assert seq % ATTN_IN_TILE == 0 and ATTN_IN_TILE % ROW_TILE == 0
    assert depth - n_a == 1

    bf = lambda a: a.astype(BF16)
    row_vec = lambda a: a.reshape(1, -1)
    pair = lambda a: jnp.concatenate([a, a], axis=-1).reshape(1, V7X_LANES)
    cos2, sin2 = _rope_tables(seq)
    n_groups = len(DILATED_GROUPS)

    w_out_bf, w_gate_up_bf, w_down_bf = bf(w_out), bf(w_gate_up), bf(w_down)
    mk, mv = _mem_kv(mem, mem_norm, bf(w_mem_kv), mem_knorm)
    h = x.reshape(bn * seq, dm)
    for l in range(depth):
        gain = row_vec(norm_mix[l])
        if l < n_a:
            qs, lf, k, v, gate, mo = _inproj_a(h, gain, bf(a_w_in[l]), a_lb_logits, pair(mem_qnorm[l]),
                                               mk, mv, l, seq)
            o = _hgrn2(qs, lf, k, v, gate, row_vec(a_onorm[l]), bn, seq)
        else:
            j = l - n_a
            *copies, mo = _inproj_b(h, gain, row_vec(kv_norm), bf(b_w_in[j]), bf(w_kv), b_qnorm[j],
                                    row_vec(b_knorm), cos2, sin2, pair(mem_qnorm[l]), mk, mv, l, bn, seq)
            q_groups, ks, vs = (copies[i * n_groups:(i + 1) * n_groups] for i in range(3))
            o = _dilated_attention(q_groups, ks, vs, bn, seq)
        h = _mix_ffn(h, o, mo, w_out_bf, row_vec(norm_ffn[l]), w_gate_up_bf, w_down_bf, l)
    return h.reshape(bn, seq, dm)
```

```python
import functools
import math

import jax
import jax.numpy as jnp
from jax import lax
from jax.experimental import pallas as pl
from jax.experimental.pallas import tpu as pltpu

F32 = jnp.float32
BF16 = jnp.bfloat16

EPS = 1e-6
HEAD_DIM = 128
CHUNK = 64
MEM_HEAD_DIM = 64
DILATED_GROUPS = ((128, 1), (512, 4), (2048, 16))
ROPE_THETA = 10000.0
LOG2E = math.log2(math.e)

V7X_LANES = 128
V7X_VMEM_SCOPED_MAX_BYTES = 60000 * 1024

ROW_TILE = 256
ATTN_IN_TILE = 512
RESIDUE_STRIDE = 4
FFN_CHUNK = 2816
HGRN_ROWS = 1024
ATTN_BLOCK = 128
MERGE_ROWS = 512
ATTN_UNROLL = 16

_NT = (((1,), (1,)), ((), ()))
_TN = (((0,), (0,)), ((), ()))


def _vmem_limit(pipelined_bytes, resident_bytes, temp_bytes):
    need = 2 * pipelined_bytes + resident_bytes + temp_bytes
    return int(min(max(need, 16 * 1024 * 1024), V7X_VMEM_SCOPED_MAX_BYTES))


def _nbytes(shape, dtype):
    n = 1
    for s in shape:
        n *= s
    return n * jnp.dtype(dtype).itemsize


def _resident(shape):
    zeros = (0,) * len(shape)
    return pl.BlockSpec(shape, lambda *_: zeros, pipeline_mode=pl.Buffered(1))


def _dot(a, b):
    return jnp.dot(a, b, preferred_element_type=F32)


def _rms(x, gain):
    ms = jnp.mean(x * x, axis=-1, keepdims=True)
    return x * lax.rsqrt(ms + EPS) * gain


def _silu(x):
    return x * jax.nn.sigmoid(x)


def _head_cols(hd):
    return slice(hd * HEAD_DIM, (hd + 1) * HEAD_DIM)


def _rms_head_pairs(x, gain):
    lo = lax.broadcasted_iota(jnp.int32, x.shape, 1) < MEM_HEAD_DIM
    x2 = x * x
    s_lo = jnp.sum(jnp.where(lo, x2, 0.0), axis=-1, keepdims=True)
    s_hi = jnp.sum(jnp.where(lo, 0.0, x2), axis=-1, keepdims=True)
    ms = jnp.where(lo, s_lo, s_hi) * (1.0 / MEM_HEAD_DIM)
    return x * lax.rsqrt(ms + EPS) * gain


def _rope(x, cos2, sin2):
    return x * cos2 + pltpu.roll(x, HEAD_DIM // 2, axis=1) * sin2


def _store_by_residue(slab_ref, tmp_ref, dil_outs, row0):
    n_heads, tile_rows = slab_ref.shape[0], slab_ref.shape[1]
    base = RESIDUE_STRIDE
    two_hops = any(dil > base for dil, _ in dil_outs)
    if two_hops:
        part = tile_rows // base
        for r in range(base):
            for hd in range(n_heads):
                tmp_ref[hd, r * part:(r + 1) * part, :] = slab_ref[hd, pl.ds(r, part, stride=base), :]
    for dil, out_ref in dil_outs:
        rows = tile_rows // dil
        dst = slice(row0 // dil, row0 // dil + rows)
        for r in range(dil):
            for hd in range(n_heads):
                if dil < base or (dil == base and not two_hops):
                    piece = slab_ref[hd, pl.ds(r, rows, stride=dil), :]
                elif dil == base:
                    piece = tmp_ref[hd, r * rows:(r + 1) * rows, :]
                else:
                    assert dil % base == 0 and dil // base <= base
                    piece = tmp_ref[hd, pl.ds((r % base) * part + r // base, rows, stride=dil // base), :]
                out_ref[r, dst, _head_cols(hd)] = piece.astype(out_ref.dtype)


def _memory_probs(mq, qgain, mk_ref):
    scaled_gain = qgain * (MEM_HEAD_DIM ** -0.5)
    probs = []
    for t in range(mq.shape[1] // V7X_LANES):
        cols = slice(t * V7X_LANES, (t + 1) * V7X_LANES)
        qn = _rms_head_pairs(mq[:, cols], scaled_gain)
        lo = lax.broadcasted_iota(jnp.int32, qn.shape, 1) < MEM_HEAD_DIM
        for keep in (lo, jnp.logical_not(lo)):
            qh = jnp.where(keep, qn, 0.0).astype(BF16)
            s = lax.dot_general(qh, mk_ref[:, cols], _NT, preferred_element_type=F32)
            p = jnp.exp(s - jnp.max(s, axis=-1, keepdims=True))
            probs.append((p.astype(BF16), jnp.sum(p, axis=-1, keepdims=True)))
    return probs


def _memory_output(probs, mv_ref, mo_ref, rows=slice(None)):
    for t in range(mo_ref.shape[1] // V7X_LANES):
        cols = slice(t * V7X_LANES, (t + 1) * V7X_LANES)
        outs = [_dot(p, mv_ref[:, cols]) / denom for p, denom in probs[2 * t:2 * t + 2]]
        lo = lax.broadcasted_iota(jnp.int32, outs[0].shape, 1) < MEM_HEAD_DIM
        mo_ref[rows, cols] = jnp.where(lo, outs[0], outs[1]).astype(mo_ref.dtype)


def _mem_kv_kernel(mem_ref, gain_ref, w_ref, kgain_ref, mk_ref, mv_ref):
    mw = mk_ref.shape[1]
    mn = _rms(mem_ref[...], gain_ref[...]).astype(BF16)
    kv = _dot(mn, w_ref[...])
    for t in range(mw // V7X_LANES):
        cols = slice(t * V7X_LANES, (t + 1) * V7X_LANES)
        mk_ref[:, cols] = _rms_head_pairs(kv[:, cols], kgain_ref[...]).astype(mk_ref.dtype)
    mv_ref[...] = kv[:, mw:].astype(mv_ref.dtype)


def _mem_kv(mem, mem_norm, w_mem_kv, mem_knorm):
    bn, mt, dm = mem.shape
    depth = w_mem_kv.shape[0]
    mw = w_mem_kv.shape[2] // 2
    kgain = jnp.concatenate([mem_knorm, mem_knorm], axis=-1).reshape(depth, 1, V7X_LANES)
    out = jax.ShapeDtypeStruct((depth, bn, mt, mw), BF16)
    return pl.pallas_call(
        _mem_kv_kernel,
        out_shape=(out, out),
        grid=(depth, bn),
        in_specs=[
            pl.BlockSpec((None, mt, dm), lambda l, b: (b, 0, 0)),
            pl.BlockSpec((None, 1, dm), lambda l, b: (l, 0, 0)),
            pl.BlockSpec((None, dm, 2 * mw), lambda l, b: (l, 0, 0)),
            pl.BlockSpec((None, 1, V7X_LANES), lambda l, b: (l, 0, 0)),
        ],
        out_specs=(
            pl.BlockSpec((None, None, mt, mw), lambda l, b: (l, b, 0, 0)),
            pl.BlockSpec((None, None, mt, mw), lambda l, b: (l, b, 0, 0)),
        ),
        name="mem_kv",
    )(mem, mem_norm.reshape(depth, 1, dm), w_mem_kv, kgain)


def _inproj_a_kernel(x_ref, gain_ref, w_ref, lbl_ref, mqg_ref, mk_ref, mv_ref,
                     qs_ref, lf_ref, k_ref, v_ref, gate_ref, mo_ref, *, layer):
    aw = qs_ref.shape[1]
    xn = _rms(x_ref[...], gain_ref[...]).astype(BF16)
    lg = lbl_ref[...]
    e = jnp.exp(lg - jnp.max(lg, axis=0, keepdims=True))
    lb = jnp.sum(e[:layer + 1], axis=0, keepdims=True) / jnp.sum(e, axis=0, keepdims=True)
    probs = _memory_probs(_dot(xn, w_ref[:, 4 * aw:]), mqg_ref[...], mk_ref)
    qs_ref[...] = _silu(_dot(xn, w_ref[:, 0:aw])).astype(qs_ref.dtype)
    _memory_output(probs, mv_ref, mo_ref)
    f = lb + (1.0 - lb) * jax.nn.sigmoid(_dot(xn, w_ref[:, aw:2 * aw]))
    lf_ref[...] = jnp.log(f) * LOG2E
    k_ref[...] = (1.0 - f).astype(k_ref.dtype)
    v_ref[...] = _dot(xn, w_ref[:, 2 * aw:3 * aw]).astype(v_ref.dtype)
    gate_ref[...] = _silu(_dot(xn, w_ref[:, 3 * aw:4 * aw])).astype(gate_ref.dtype)


def _inproj_a(h, gain, w, lb_logits, mq_gain, mk, mv, layer, seq):
    t, dm = h.shape
    mt, mw = mk.shape[2], mk.shape[3]
    aw = (w.shape[1] - mw) // 4
    tiles_per_seq = seq // ROW_TILE
    row = lambda i: (i, 0)
    mem = lambda i: (layer, i // tiles_per_seq, 0, 0)
    wide = functools.partial(jax.ShapeDtypeStruct, (t, aw))
    pipelined = (_nbytes((ROW_TILE, dm), F32) + _nbytes((ROW_TILE, aw), F32)
                 + 4 * _nbytes((ROW_TILE, aw), BF16) + _nbytes((ROW_TILE, mw), BF16)
                 + 2 * _nbytes((mt, mw), BF16))
    return pl.pallas_call(
        functools.partial(_inproj_a_kernel, layer=layer),
        out_shape=(wide(BF16), wide(F32), wide(BF16), wide(BF16), wide(BF16),
                   jax.ShapeDtypeStruct((t, mw), BF16)),
        grid=(t // ROW_TILE,),
        in_specs=[
            pl.BlockSpec((ROW_TILE, dm), row),
            _resident((1, dm)),
            _resident(w.shape),
            _resident(lb_logits.shape),
            _resident((1, V7X_LANES)),
            pl.BlockSpec((None, None, mt, mw), mem),
            pl.BlockSpec((None, None, mt, mw), mem),
        ],
        out_specs=tuple([pl.BlockSpec((ROW_TILE, aw), row)] * 5 + [pl.BlockSpec((ROW_TILE, mw), row)]),
        compiler_params=pltpu.CompilerParams(
            dimension_semantics=("parallel",),
            vmem_limit_bytes=_vmem_limit(pipelined, _nbytes(w.shape, BF16),
                                         4 * _nbytes((ROW_TILE, aw), F32))),
        name="inproj_a",
    )(h, gain, w, lb_logits, mq_gain, mk, mv)


def _hgrn2_kernel(on_ref, qs_ref, lf_ref, k_ref, v_ref, gate_ref, o_ref, state_ref):
    @pl.when(pl.program_id(2) == 0)
    def _():
        state_ref[...] = jnp.zeros_like(state_ref)

    row = lax.broadcasted_iota(jnp.int32, (CHUNK, HEAD_DIM), 0)
    causal = (lax.broadcasted_iota(jnp.int32, (CHUNK, CHUNK), 0)
              >= lax.broadcasted_iota(jnp.int32, (CHUNK, CHUNK), 1))
    onorm = on_ref[...]

    chunks = [slice(c * CHUNK, (c + 1) * CHUNK) for c in range(qs_ref.shape[0] // CHUNK)]
    q_ins, vs, decays, atts, kvs = [], [], [], [], []
    for rows in chunks:
        b = lf_ref[rows, :]
        shift = 1
        while shift < CHUNK:
            b = b + jnp.where(row >= shift, pltpu.roll(b, shift, axis=0), 0.0)
            shift *= 2
        b_end = b[CHUNK - 1:CHUNK, :]
        k = k_ref[rows, :].astype(F32)
        q_in = (qs_ref[rows, :].astype(F32) * jnp.exp2(b)).astype(BF16)
        k_in = (k * jnp.exp2(-b)).astype(BF16)
        k_out = (k * jnp.exp2(b_end - b)).astype(BF16)
        v = v_ref[rows, :]
        q_ins.append(q_in)
        vs.append(v)
        decays.append(jnp.exp2(b_end))
        atts.append(lax.dot_general(q_in, k_in, _NT, preferred_element_type=F32))
        kvs.append(lax.dot_general(v, k_out, _TN, preferred_element_type=F32))

    state_t = state_ref[...]
    states = []
    for decay, kv in zip(decays, kvs):
        states.append(state_t.astype(BF16))
        state_t = state_t * decay + kv
    state_ref[...] = state_t

    for rows, q_in, v, att, state_in in zip(chunks, q_ins, vs, atts, states):
        att = jnp.where(causal, att, 0.0).astype(BF16)
        o = _dot(att, v) + lax.dot_general(q_in, state_in, _NT, preferred_element_type=F32)
        o_ref[rows, :] = (_rms(o, onorm) * gate_ref[rows, :].astype(F32)).astype(o_ref.dtype)


def _hgrn2(qs, lf, k, v, gate, onorm, bn, seq):
    t, aw = qs.shape
    heads = aw // HEAD_DIM
    steps = seq // HGRN_ROWS
    spec = pl.BlockSpec((HGRN_ROWS, HEAD_DIM), lambda b, h, s: (b * steps + s, h))
    return pl.pallas_call(
        _hgrn2_kernel,
        out_shape=jax.ShapeDtypeStruct((t, aw), BF16),
        grid=(bn, heads, steps),
        in_specs=[pl.BlockSpec((1, HEAD_DIM), lambda b, h, s: (0, h)), spec, spec, spec, spec, spec],
        out_specs=spec,
        scratch_shapes=[pltpu.VMEM((HEAD_DIM, HEAD_DIM), F32)],
        compiler_params=pltpu.CompilerParams(
            dimension_semantics=("parallel", "parallel", "arbitrary")),
        name="hgrn2",
    )(onorm, qs, lf, k, v, gate)


def _mix_ffn_kernel(h_ref, o_ref, mo_ref, wo_ref, nf_ref, wgu_ref, wd_ref, h_out):
    main_w = o_ref.shape[1]
    hidden = wd_ref.shape[0]
    h = h_ref[...] + _dot(o_ref[...], wo_ref[0:main_w, :]) + _dot(mo_ref[...], wo_ref[main_w:, :])
    hn = _rms(h, nf_ref[...]).astype(BF16)
    for c in range(hidden // FFN_CHUNK):
        cols = slice(c * FFN_CHUNK, (c + 1) * FFN_CHUNK)
        up_cols = slice(hidden + c * FFN_CHUNK, hidden + (c + 1) * FFN_CHUNK)
        act = (_silu(_dot(hn, wgu_ref[:, cols])) * _dot(hn, wgu_ref[:, up_cols])).astype(BF16)
        h = h + _dot(act, wd_ref[cols, :])
    h_out[...] = h


def _layer_resident(stacked, layer):
    zeros = (0,) * (stacked.ndim - 1)
    return pl.BlockSpec((None,) + stacked.shape[1:], lambda *_: (layer,) + zeros,
                        pipeline_mode=pl.Buffered(1))


def _mix_ffn(h, o, mo, w_out, norm_ffn, w_gate_up, w_down, layer):
    t, dm = h.shape
    main_w, mw = o.shape[1], mo.shape[1]
    row = lambda i: (i, 0)
    resident = sum(_nbytes(w.shape[1:], BF16) for w in (w_out, w_gate_up, w_down))
    pipelined = 2 * _nbytes((ROW_TILE, dm), F32) + _nbytes((ROW_TILE, main_w + mw), BF16)
    temps = 3 * _nbytes((ROW_TILE, dm), F32) + 3 * _nbytes((ROW_TILE, FFN_CHUNK), F32)
    return pl.pallas_call(
        _mix_ffn_kernel,
        out_shape=jax.ShapeDtypeStruct((t, dm), F32),
        grid=(t // ROW_TILE,),
        in_specs=[
            pl.BlockSpec((ROW_TILE, dm), row), pl.BlockSpec((ROW_TILE, main_w), row),
            pl.BlockSpec((ROW_TILE, mw), row),
            _layer_resident(w_out, layer), _resident((1, dm)), _layer_resident(w_gate_up, layer),
            _layer_resident(w_down, layer),
        ],
        out_specs=pl.BlockSpec((ROW_TILE, dm), row),
        compiler_params=pltpu.CompilerParams(
            dimension_semantics=("parallel",),
            vmem_limit_bytes=_vmem_limit(pipelined, resident, temps)),
        name="mix_ffn",
    )(h, o, mo, w_out, norm_ffn, w_gate_up, w_down)


def _inproj_b_kernel(*refs, dilations):
    n_groups = len(dilations)
    (x_ref, gain_ref, kvg_ref, w_ref, wkv_ref, qn_ref, kn_ref, cos_ref, sin_ref,
     mqg_ref, mk_ref, mv_ref) = refs[:12]
    q_refs = refs[12:12 + n_groups]
    k_refs = refs[12 + n_groups:12 + 2 * n_groups]
    v_refs = refs[12 + 2 * n_groups:12 + 3 * n_groups]
    mo_ref = refs[12 + 3 * n_groups]
    scratch = refs[13 + 3 * n_groups:]
    q_slabs = dict(zip([d for d in dilations if d != 1], scratch[:-5]))
    k_slab, v_slab, q_tmp, k_tmp, v_tmp = scratch[-5:]
    kv_w = wkv_ref.shape[1] // 2
    bw = (w_ref.shape[1] - mo_ref.shape[1]) // n_groups
    heads = range(bw // HEAD_DIM)
    pair = 2
    q_scale = (HEAD_DIM ** -0.5) * LOG2E

    for row0 in range(0, x_ref.shape[0], ROW_TILE):
        rows = slice(row0, row0 + ROW_TILE)
        x = x_ref[rows, :]
        xhat = x * lax.rsqrt(jnp.mean(x * x, axis=-1, keepdims=True) + EPS)
        xn = (xhat * gain_ref[...]).astype(BF16)
        kn = (xhat * kvg_ref[...]).astype(BF16)
        cos, sin = cos_ref[rows, :], sin_ref[rows, :]
        cos_s, sin_s = cos * q_scale, sin * q_scale

        def queries(gi, row0=row0, rows=rows, xn=xn, cos_s=cos_s, sin_s=sin_s):
            dil, q_ref = dilations[gi], q_refs[gi]
            for hd in heads:
                if hd % pair == 0:
                    col0 = gi * bw + hd * HEAD_DIM
                    qs = _dot(xn, w_ref[:, col0:col0 + pair * HEAD_DIM])
                q = _rope(_rms(qs[:, _head_cols(hd % pair)], qn_ref[gi]), cos_s, sin_s)
                if dil == 1:
                    q_ref[rows, _head_cols(hd)] = q.astype(q_ref.dtype)
                else:
                    q_slabs[dil][hd] = q
            if dil != 1:
                _store_by_residue(q_slabs[dil], q_tmp, [(dil, q_ref)], row0)

        def copies(slab, tmp, out_refs, row0=row0, rows=rows):
            for dil, out_ref in zip(dilations, out_refs):
                if dil == 1:
                    for hd in heads:
                        out_ref[rows, _head_cols(hd)] = slab[hd].astype(out_ref.dtype)
            _store_by_residue(slab, tmp, [(d, ref) for d, ref in zip(dilations, out_refs) if d != 1], row0)

        order = sorted(range(n_groups), key=lambda gi: -dilations[gi])
        queries(order[0])
        probs = _memory_probs(_dot(xn, w_ref[:, n_groups * bw:]), mqg_ref[...], mk_ref)
        for hd in heads:
            if hd % pair == 0:
                k = _dot(kn, wkv_ref[:, hd * HEAD_DIM:(hd + pair) * HEAD_DIM])
            k_slab[hd] = _rope(_rms(k[:, _head_cols(hd % pair)], kn_ref[...]), cos, sin)
        copies(k_slab, k_tmp, k_refs)
        for gi in order[1:-1]:
            queries(gi)
        _memory_output(probs, mv_ref, mo_ref, rows)
        for hd in heads:
            if hd % pair == 0:
                v = _dot(kn, wkv_ref[:, kv_w + hd * HEAD_DIM:kv_w + (hd + pair) * HEAD_DIM])
            v_slab[hd] = v[:, _head_cols(hd % pair)]
        copies(v_slab, v_tmp, v_refs)
        queries(order[-1])


def _residue_out(bn, seq, width, dil):
    tiles_per_seq = seq // ATTN_IN_TILE
    if dil == 1:
        return (jax.ShapeDtypeStruct((bn * seq, width), BF16),
                pl.BlockSpec((ATTN_IN_TILE, width), lambda i: (i, 0)))
    return (jax.ShapeDtypeStruct((bn, dil, seq // dil, width), BF16),
            pl.BlockSpec((None, dil, ATTN_IN_TILE // dil, width),
                         lambda i: (i // tiles_per_seq, 0, i % tiles_per_seq, 0)))


def _inproj_b(h, gain, kv_gain, w, w_kv, q_norm, k_norm, cos2, sin2, mq_gain, mk, mv, layer, bn, seq):
    t, dm = h.shape
    mt, mw = mk.shape[2], mk.shape[3]
    dilations = tuple(d for _, d in DILATED_GROUPS)
    n_groups = len(dilations)
    bw = (w.shape[1] - mw) // n_groups
    kv_w = w_kv.shape[1] // 2
    assert kv_w == bw
    tile = ATTN_IN_TILE
    tiles_per_seq = seq // tile
    row = lambda i: (i, 0)
    pos = lambda i: (i % tiles_per_seq, 0)
    mem = lambda i: (layer, i // tiles_per_seq, 0, 0)
    copies = [_residue_out(bn, seq, bw, dil) for dil in dilations] * 3
    slab = (bw // HEAD_DIM, ROW_TILE, HEAD_DIM)
    n_slabs = sum(1 for dil in dilations if dil != 1) + 5
    pipelined = (_nbytes((tile, dm), F32) + 3 * n_groups * _nbytes((tile, bw), BF16)
                 + 2 * _nbytes((tile, HEAD_DIM), F32) + _nbytes((tile, mw), BF16)
                 + 2 * _nbytes((mt, mw), BF16))
    resident = _nbytes(w.shape, BF16) + _nbytes(w_kv.shape, BF16) + n_slabs * _nbytes(slab, F32)
    return pl.pallas_call(
        functools.partial(_inproj_b_kernel, dilations=dilations),
        out_shape=tuple([shape for shape, _ in copies] + [jax.ShapeDtypeStruct((t, mw), BF16)]),
        grid=(t // tile,),
        in_specs=[
            pl.BlockSpec((tile, dm), row),
            _resident((1, dm)), _resident((1, dm)),
            _resident(w.shape), _resident(w_kv.shape),
            _resident((n_groups, 1, HEAD_DIM)), _resident((1, HEAD_DIM)),
            pl.BlockSpec((tile, HEAD_DIM), pos), pl.BlockSpec((tile, HEAD_DIM), pos),
            _resident((1, V7X_LANES)),
            pl.BlockSpec((None, None, mt, mw), mem),
            pl.BlockSpec((None, None, mt, mw), mem),
        ],
        out_specs=tuple([spec for _, spec in copies] + [pl.BlockSpec((tile, mw), row)]),
        scratch_shapes=[pltpu.VMEM(slab, F32)] * n_slabs,
        compiler_params=pltpu.CompilerParams(
            dimension_semantics=("parallel",),
            vmem_limit_bytes=_vmem_limit(pipelined, resident, 6 * _nbytes((ROW_TILE, bw), F32))),
        name="inproj_b",
    )(h, gain, kv_gain, w, w_kv, q_norm.reshape(n_groups, 1, HEAD_DIM), k_norm, cos2, sin2,
      mq_gain, mk, mv)


def _dilated_kernel(*refs, dilations):
    n_groups = len(dilations)
    q_refs = refs[:n_groups]
    k_refs = refs[n_groups:2 * n_groups]
    v_refs = refs[2 * n_groups:3 * n_groups]
    o_ref, og_ref, lse_ref, bias_ref = refs[3 * n_groups:]
    seq = o_ref.shape[0]
    blk = ATTN_BLOCK

    qi = lax.broadcasted_iota(jnp.int32, (blk, 2 * blk), 0)
    kj = lax.broadcasted_iota(jnp.int32, (blk, 2 * blk), 1)
    band = (kj >= qi) & (kj <= qi + blk)
    bias_ref[0] = jnp.where(band & (kj >= blk), 0.0, -jnp.inf)
    bias_ref[1] = jnp.where(band, 0.0, -jnp.inf)
    bias_ref[2] = jnp.where(kj <= qi, 0.0, -jnp.inf)

    for gi, dil in enumerate(dilations):
        q_ref, k_ref, v_ref = q_refs[gi], k_refs[gi], v_refs[gi]
        n_blocks = seq // (blk * dil)

        def blocks(step, carry, q_ref=q_ref, k_ref=k_ref, v_ref=v_ref, gi=gi, dil=dil, n_blocks=n_blocks):
            idxs = [step * ATTN_UNROLL + u for u in range(ATTN_UNROLL)]
            starts = [pl.multiple_of(idx * blk, blk) for idx in idxs]
            windows = [pl.ds(pl.multiple_of(jnp.maximum(start - blk, 0), blk), 2 * blk) for start in starts]
            scores = [lax.dot_general(q_ref[pl.ds(start, blk), :], k_ref[window, :], _NT,
                                      preferred_element_type=F32)
                      for start, window in zip(starts, windows)]
            soft = []
            for idx, s in zip(idxs, scores):
                n = lax.rem(idx, n_blocks)
                s = s + bias_ref[jnp.where(idx == 0, 2, jnp.minimum(n, 1))]
                m = jnp.max(s, axis=-1, keepdims=True)
                p = jnp.exp2(s - m)
                denom = jnp.sum(p, axis=-1, keepdims=True)
                soft.append((p.astype(BF16), m, denom))
            for idx, start, window, (p, m, denom) in zip(idxs, starts, windows, soft):
                o = _dot(p, v_ref[window, :]) / denom
                lse2 = jnp.broadcast_to(m + jnp.log(denom) * LOG2E, (blk, HEAD_DIM))
                if dil == 1:
                    out_rows = pl.ds(start, blk)
                else:
                    n = lax.rem(idx, n_blocks)
                    out_rows = pl.ds(n * (blk * dil) + lax.div(idx, n_blocks), blk, stride=dil)
                og_ref[gi, out_rows, :] = o
                lse_ref[gi, out_rows, :] = lse2
            return carry

        lax.fori_loop(0, dil * n_blocks // ATTN_UNROLL, blocks, 0)

    def merge(step, carry):
        rows = pl.ds(pl.multiple_of(step * MERGE_ROWS, MERGE_ROWS), MERGE_ROWS)
        lses = [lse_ref[gi, rows, :] for gi in range(n_groups)]
        top = functools.reduce(jnp.maximum, lses)
        ws = [jnp.exp2(l - top) for l in lses]
        acc = sum(w * og_ref[gi, rows, :] for gi, w in enumerate(ws))
        o_ref[rows, :] = (acc / sum(ws)).astype(o_ref.dtype)
        return carry

    lax.fori_loop(0, seq // MERGE_ROWS, merge, 0)


def _dilated_attention(qs, ks, vs, bn, seq):
    width = qs[0].shape[-1]
    heads = width // HEAD_DIM
    dilations = tuple(d for _, d in DILATED_GROUPS)
    n_groups = len(dilations)
    as_seq = lambda a: a.reshape(bn, seq, width)
    spec = pl.BlockSpec((None, seq, HEAD_DIM), lambda b, h: (b, 0, h))
    seq_bf16 = _nbytes((seq, HEAD_DIM), BF16)
    seq_f32 = _nbytes((seq, HEAD_DIM), F32)
    n_bias = 3
    scratch = 2 * n_groups * seq_f32 + _nbytes((n_bias, ATTN_BLOCK, 2 * ATTN_BLOCK), F32)
    out = pl.pallas_call(
        functools.partial(_dilated_kernel, dilations=dilations),
        out_shape=jax.ShapeDtypeStruct((bn, seq, width), BF16),
        grid=(bn, heads),
        in_specs=[spec] * (3 * n_groups),
        out_specs=spec,
        scratch_shapes=[
            pltpu.VMEM((n_groups, seq, HEAD_DIM), F32),
            pltpu.VMEM((n_groups, seq, HEAD_DIM), F32),
            pltpu.VMEM((n_bias, ATTN_BLOCK, 2 * ATTN_BLOCK), F32),
        ],
        compiler_params=pltpu.CompilerParams(
            dimension_semantics=("parallel", "parallel"),
            vmem_limit_bytes=_vmem_limit((3 * n_groups + 1) * seq_bf16, scratch,
                                         8 * _nbytes((MERGE_ROWS, HEAD_DIM), F32))),
        name="dilated_attention",
    )(*[as_seq(a) for a in (*qs, *ks, *vs)])
    return out.reshape(bn * seq, width)


def _rope_tables(seq):
    half = HEAD_DIM // 2
    inv = ROPE_THETA ** (-jnp.arange(half, dtype=F32) / half)
    ang = jnp.arange(seq).astype(F32)[:, None] * inv[None, :]
    cos, sin = jnp.cos(ang), jnp.sin(ang)
    return jnp.concatenate([cos, cos], axis=-1), jnp.concatenate([-sin, sin], axis=-1)


def kernel(x, mem, norm_mix, norm_ffn, a_w_in, a_lb_logits, a_onorm, b_w_in, b_qnorm, kv_norm, w_kv,
           b_knorm, mem_norm, w_mem_kv, mem_qnorm, mem_knorm, w_out, w_gate_up, w_down):
    bn, seq, dm = x.shape
    depth = norm_mix.shape[0]
    n_a = a_w_in.shape[0]
    max_dil = max(d for _, d in DILATED_GROUPS)
    assert seq % (ATTN_BLOCK * max_dil) == 0 and seq % HGRN_ROWS == 0 and seq % ROW_TILE == 0
    assert seq % (ATTN_BLOCK * ATTN_UNROLL) == 0 and HGRN_ROWS % CHUNK == 0 and ROW_TILE % max_dil == 0
    assert all(w == ATTN_BLOCK * d for w, d in DILATED_GROUPS)
    assert seq % ATTN_IN_TILE == 0 and ATTN_IN_TILE % ROW_TILE == 0
    assert depth - n_a == 1

    bf = lambda a: a.astype(BF16)
    row_vec = lambda a: a.reshape(1, -1)
    pair = lambda a: jnp.concatenate([a, a], axis=-1).reshape(1, V7X_LANES)
    cos2, sin2 = _rope_tables(seq)
    n_groups = len(DILATED_GROUPS)

    w_out_bf, w_gate_up_bf, w_down_bf = bf(w_out), bf(w_gate_up), bf(w_down)
    mk, mv = _mem_kv(mem, mem_norm, bf(w_mem_kv), mem_knorm)
    h = x.reshape(bn * seq, dm)
    for l in range(depth):
        gain = row_vec(norm_mix[l])
        if l < n_a:
            qs, lf, k, v, gate, mo = _inproj_a(h, gain, bf(a_w_in[l]), a_lb_logits, pair(mem_qnorm[l]),
                                               mk, mv, l, seq)
            o = _hgrn2(qs, lf, k, v, gate, row_vec(a_onorm[l]), bn, seq)
        else:
            j = l - n_a
            *copies, mo = _inproj_b(h, gain, row_vec(kv_norm), bf(b_w_in[j]), bf(w_kv), b_qnorm[j],
                                    row_vec(b_knorm), cos2, sin2, pair(mem_qnorm[l]), mk, mv, l, bn, seq)
            q_groups, ks, vs = (copies[i * n_groups:(i + 1) * n_groups] for i in range(3))
            o = _dilated_attention(q_groups, ks, vs, bn, seq)
        h = _mix_ffn(h, o, mo, w_out_bf, row_vec(norm_ffn[l]), w_gate_up_bf, w_down_bf, l)
    return h.reshape(bn, seq, dm)
```

```python
import functools
import math

import jax
import jax.numpy as jnp
from jax import lax
from jax.experimental import pallas as pl
from jax.experimental.pallas import tpu as pltpu

F32 = jnp.float32
BF16 = jnp.bfloat16

EPS = 1e-6
HEAD_DIM = 128
CHUNK = 64
MEM_HEAD_DIM = 64
DILATED_GROUPS = ((128, 1), (512, 4), (2048, 16))
ROPE_THETA = 10000.0
LOG2E = math.log2(math.e)

V7X_LANES = 128
V7X_VMEM_SCOPED_MAX_BYTES = 60000 * 1024

ROW_TILE = 256
ATTN_IN_TILE = 512
HGRN_IN_TILE = 256
RESIDUE_STRIDE = 4
FFN_TILE = 512
FFN_CHUNK = 2816
HGRN_ROWS = 2048
ATTN_BLOCK = 128
MERGE_ROWS = 512
ATTN_UNROLL = 16

_NT = (((1,), (1,)), ((), ()))
_TN = (((0,), (0,)), ((), ()))


def _vmem_limit(pipelined_bytes, resident_bytes, temp_bytes):
    need = 2 * pipelined_bytes + resident_bytes + temp_bytes
    return int(min(max(need, 16 * 1024 * 1024), V7X_VMEM_SCOPED_MAX_BYTES))


def _nbytes(shape, dtype):
    n = 1
    for s in shape:
        n *= s
    return n * jnp.dtype(dtype).itemsize


def _resident(shape):
    zeros = (0,) * len(shape)
    return pl.BlockSpec(shape, lambda *_: zeros, pipeline_mode=pl.Buffered(1))


def _dot(a, b):
    return jnp.dot(a, b, preferred_element_type=F32)


def _rms(x, gain):
    ms = jnp.mean(x * x, axis=-1, keepdims=True)
    return x * lax.rsqrt(ms + EPS) * gain


def _silu(x):
    return x * jax.nn.sigmoid(x)


def _head_cols(hd):
    return slice(hd * HEAD_DIM, (hd + 1) * HEAD_DIM)


def _rms_head_pairs(x, gain):
    lo = lax.broadcasted_iota(jnp.int32, x.shape, 1) < MEM_HEAD_DIM
    x2 = x * x
    s_lo = jnp.sum(jnp.where(lo, x2, 0.0), axis=-1, keepdims=True)
    s_hi = jnp.sum(jnp.where(lo, 0.0, x2), axis=-1, keepdims=True)
    ms = jnp.where(lo, s_lo, s_hi) * (1.0 / MEM_HEAD_DIM)
    return x * lax.rsqrt(ms + EPS) * gain


def _rope(x, cos2, sin2):
    return x * cos2 + pltpu.roll(x, HEAD_DIM // 2, axis=1) * sin2


def _store_by_residue(slab_ref, tmp_ref, dil_outs, row0):
    n_heads, tile_rows = slab_ref.shape[0], slab_ref.shape[1]
    base = RESIDUE_STRIDE
    two_hops = any(dil > base for dil, _ in dil_outs)
    if two_hops:
        part = tile_rows // base
        for r in range(base):
            for hd in range(n_heads):
                tmp_ref[hd, r * part:(r + 1) * part, :] = slab_ref[hd, pl.ds(r, part, stride=base), :]
    for dil, out_ref in dil_outs:
        rows = tile_rows // dil
        dst = slice(row0 // dil, row0 // dil + rows)
        for r in range(dil):
            for hd in range(n_heads):
                if dil < base or (dil == base and not two_hops):
                    piece = slab_ref[hd, pl.ds(r, rows, stride=dil), :]
                elif dil == base:
                    piece = tmp_ref[hd, r * rows:(r + 1) * rows, :]
                else:
                    assert dil % base == 0 and dil // base <= base
                    piece = tmp_ref[hd, pl.ds((r % base) * part + r // base, rows, stride=dil // base), :]
                out_ref[r, dst, _head_cols(hd)] = piece.astype(out_ref.dtype)


def _memory_probs(mq, qgain, mk_ref):
    scaled_gain = qgain * (MEM_HEAD_DIM ** -0.5)
    probs = []
    for t in range(mq.shape[1] // V7X_LANES):
        cols = slice(t * V7X_LANES, (t + 1) * V7X_LANES)
        qn = _rms_head_pairs(mq[:, cols], scaled_gain)
        lo = lax.broadcasted_iota(jnp.int32, qn.shape, 1) < MEM_HEAD_DIM
        for keep in (lo, jnp.logical_not(lo)):
            qh = jnp.where(keep, qn, 0.0).astype(BF16)
            s = lax.dot_general(qh, mk_ref[:, cols], _NT, preferred_element_type=F32)
            p = jnp.exp(s - jnp.max(s, axis=-1, keepdims=True))
            probs.append((p.astype(BF16), jnp.sum(p, axis=-1, keepdims=True)))
    return probs


def _memory_output(probs, mv_ref, mo_ref, rows=slice(None)):
    for t in range(mo_ref.shape[1] // V7X_LANES):
        cols = slice(t * V7X_LANES, (t + 1) * V7X_LANES)
        outs = [_dot(p, mv_ref[:, cols]) / denom for p, denom in probs[2 * t:2 * t + 2]]
        lo = lax.broadcasted_iota(jnp.int32, outs[0].shape, 1) < MEM_HEAD_DIM
        mo_ref[rows, cols] = jnp.where(lo, outs[0], outs[1]).astype(mo_ref.dtype)


def _mem_kv_kernel(mem_ref, gain_ref, w_ref, kgain_ref, mk_ref, mv_ref):
    mw = mk_ref.shape[1]
    mn = _rms(mem_ref[...], gain_ref[...]).astype(BF16)
    kv = _dot(mn, w_ref[...])
    for t in range(mw // V7X_LANES):
        cols = slice(t * V7X_LANES, (t + 1) * V7X_LANES)
        mk_ref[:, cols] = _rms_head_pairs(kv[:, cols], kgain_ref[...]).astype(mk_ref.dtype)
    mv_ref[...] = kv[:, mw:].astype(mv_ref.dtype)


def _mem_kv(mem, mem_norm, w_mem_kv, mem_knorm):
    bn, mt, dm = mem.shape
    depth = w_mem_kv.shape[0]
    mw = w_mem_kv.shape[2] // 2
    kgain = jnp.concatenate([mem_knorm, mem_knorm], axis=-1).reshape(depth, 1, V7X_LANES)
    out = jax.ShapeDtypeStruct((depth, bn, mt, mw), BF16)
    return pl.pallas_call(
        _mem_kv_kernel,
        out_shape=(out, out),
        grid=(depth, bn),
        in_specs=[
            pl.BlockSpec((None, mt, dm), lambda l, b: (b, 0, 0)),
            pl.BlockSpec((None, 1, dm), lambda l, b: (l, 0, 0)),
            pl.BlockSpec((None, dm, 2 * mw), lambda l, b: (l, 0, 0)),
            pl.BlockSpec((None, 1, V7X_LANES), lambda l, b: (l, 0, 0)),
        ],
        out_specs=(
            pl.BlockSpec((None, None, mt, mw), lambda l, b: (l, b, 0, 0)),
            pl.BlockSpec((None, None, mt, mw), lambda l, b: (l, b, 0, 0)),
        ),
        name="mem_kv",
    )(mem, mem_norm.reshape(depth, 1, dm), w_mem_kv, kgain)


def _inproj_a_kernel(x_ref, gain_ref, w_ref, lbl_ref, mqg_ref, mk_ref, mv_ref,
                     qs_ref, lf_ref, k_ref, v_ref, gate_ref, mo_ref, *, layer):
    aw = qs_ref.shape[1]
    lg = lbl_ref[...]
    e = jnp.exp(lg - jnp.max(lg, axis=0, keepdims=True))
    lb = jnp.sum(e[:layer + 1], axis=0, keepdims=True) / jnp.sum(e, axis=0, keepdims=True)
    for row0 in range(0, x_ref.shape[0], ROW_TILE):
        rows = slice(row0, row0 + ROW_TILE)
        xn = _rms(x_ref[rows, :], gain_ref[...]).astype(BF16)
        probs = _memory_probs(_dot(xn, w_ref[:, 4 * aw:]), mqg_ref[...], mk_ref)
        qs_ref[rows, :] = _silu(_dot(xn, w_ref[:, 0:aw])).astype(qs_ref.dtype)
        _memory_output(probs, mv_ref, mo_ref, rows)
        f = lb + (1.0 - lb) * jax.nn.sigmoid(_dot(xn, w_ref[:, aw:2 * aw]))
        lf_ref[rows, :] = jnp.log(f) * LOG2E
        k_ref[rows, :] = (1.0 - f).astype(k_ref.dtype)
        v_ref[rows, :] = _dot(xn, w_ref[:, 2 * aw:3 * aw]).astype(v_ref.dtype)
        gate_ref[rows, :] = _silu(_dot(xn, w_ref[:, 3 * aw:4 * aw])).astype(gate_ref.dtype)


def _inproj_a(h, gain, w, lb_logits, mq_gain, mk, mv, layer, seq):
    t, dm = h.shape
    mt, mw = mk.shape[2], mk.shape[3]
    aw = (w.shape[1] - mw) // 4
    tile = HGRN_IN_TILE
    tiles_per_seq = seq // tile
    row = lambda i: (i, 0)
    mem = lambda i: (layer, i // tiles_per_seq, 0, 0)
    wide = functools.partial(jax.ShapeDtypeStruct, (t, aw))
    pipelined = (_nbytes((tile, dm), F32) + _nbytes((tile, aw), F32)
                 + 4 * _nbytes((tile, aw), BF16) + _nbytes((tile, mw), BF16)
                 + 2 * _nbytes((mt, mw), BF16))
    return pl.pallas_call(
        functools.partial(_inproj_a_kernel, layer=layer),
        out_shape=(wide(BF16), wide(F32), wide(BF16), wide(BF16), wide(BF16),
                   jax.ShapeDtypeStruct((t, mw), BF16)),
        grid=(t // tile,),
        in_specs=[
            pl.BlockSpec((tile, dm), row),
            _resident((1, dm)),
            _resident(w.shape),
            _resident(lb_logits.shape),
            _resident((1, V7X_LANES)),
            pl.BlockSpec((None, None, mt, mw), mem),
            pl.BlockSpec((None, None, mt, mw), mem),
        ],
        out_specs=tuple([pl.BlockSpec((tile, aw), row)] * 5 + [pl.BlockSpec((tile, mw), row)]),
        compiler_params=pltpu.CompilerParams(
            dimension_semantics=("parallel",),
            vmem_limit_bytes=_vmem_limit(pipelined, _nbytes(w.shape, BF16),
                                         4 * _nbytes((ROW_TILE, aw), F32))),
        name="inproj_a",
    )(h, gain, w, lb_logits, mq_gain, mk, mv)


def _hgrn2_kernel(on_ref, qs_ref, lf_ref, k_ref, v_ref, gate_ref, o_ref, state_ref):
    @pl.when(pl.program_id(2) == 0)
    def _():
        state_ref[...] = jnp.zeros_like(state_ref)

    row = lax.broadcasted_iota(jnp.int32, (CHUNK, HEAD_DIM), 0)
    causal = (lax.broadcasted_iota(jnp.int32, (CHUNK, CHUNK), 0)
              >= lax.broadcasted_iota(jnp.int32, (CHUNK, CHUNK), 1))
    onorm = on_ref[...]

    chunks = [slice(c * CHUNK, (c + 1) * CHUNK) for c in range(qs_ref.shape[0] // CHUNK)]
    q_ins, vs, decays, atts, kvs = [], [], [], [], []
    for rows in chunks:
        b = lf_ref[rows, :]
        shift = 1
        while shift < CHUNK:
            b = b + jnp.where(row >= shift, pltpu.roll(b, shift, axis=0), 0.0)
            shift *= 2
        b_end = b[CHUNK - 1:CHUNK, :]
        k = k_ref[rows, :].astype(F32)
        q_in = (qs_ref[rows, :].astype(F32) * jnp.exp2(b)).astype(BF16)
        k_in = (k * jnp.exp2(-b)).astype(BF16)
        k_out = (k * jnp.exp2(b_end - b)).astype(BF16)
        v = v_ref[rows, :]
        q_ins.append(q_in)
        vs.append(v)
        decays.append(jnp.exp2(b_end))
        atts.append(lax.dot_general(q_in, k_in, _NT, preferred_element_type=F32))
        kvs.append(lax.dot_general(v, k_out, _TN, preferred_element_type=F32))

    state_t = state_ref[...]
    states = []
    for decay, kv in zip(decays, kvs):
        states.append(state_t.astype(BF16))
        state_t = state_t * decay + kv
    state_ref[...] = state_t

    for rows, q_in, v, att, state_in in zip(chunks, q_ins, vs, atts, states):
        att = jnp.where(causal, att, 0.0).astype(BF16)
        o = _dot(att, v) + lax.dot_general(q_in, state_in, _NT, preferred_element_type=F32)
        o_ref[rows, :] = (_rms(o, onorm) * gate_ref[rows, :].astype(F32)).astype(o_ref.dtype)


def _hgrn2(qs, lf, k, v, gate, onorm, bn, seq):
    t, aw = qs.shape
    heads = aw // HEAD_DIM
    steps = seq // HGRN_ROWS
    spec = pl.BlockSpec((HGRN_ROWS, HEAD_DIM), lambda b, h, s: (b * steps + s, h))
    return pl.pallas_call(
        _hgrn2_kernel,
        out_shape=jax.ShapeDtypeStruct((t, aw), BF16),
        grid=(bn, heads, steps),
        in_specs=[pl.BlockSpec((1, HEAD_DIM), lambda b, h, s: (0, h)), spec, spec, spec, spec, spec],
        out_specs=spec,
        scratch_shapes=[pltpu.VMEM((HEAD_DIM, HEAD_DIM), F32)],
        compiler_params=pltpu.CompilerParams(
            dimension_semantics=("parallel", "parallel", "arbitrary")),
        name="hgrn2",
    )(onorm, qs, lf, k, v, gate)


def _mix_ffn_kernel(h_ref, o_ref, mo_ref, wo_ref, nf_ref, wgu_ref, wd_ref, h_out):
    main_w = o_ref.shape[1]
    hidden = wd_ref.shape[0]
    h = h_ref[...] + _dot(o_ref[...], wo_ref[0:main_w, :]) + _dot(mo_ref[...], wo_ref[main_w:, :])
    hn = _rms(h, nf_ref[...]).astype(BF16)
    for c in range(hidden // FFN_CHUNK):
        cols = slice(c * FFN_CHUNK, (c + 1) * FFN_CHUNK)
        up_cols = slice(hidden + c * FFN_CHUNK, hidden + (c + 1) * FFN_CHUNK)
        act = (_silu(_dot(hn, wgu_ref[:, cols])) * _dot(hn, wgu_ref[:, up_cols])).astype(BF16)
        h = h + _dot(act, wd_ref[cols, :])
    h_out[...] = h


def _layer_resident(stacked, layer):
    zeros = (0,) * (stacked.ndim - 1)
    return pl.BlockSpec((None,) + stacked.shape[1:], lambda *_: (layer,) + zeros,
                        pipeline_mode=pl.Buffered(1))


def _mix_ffn(h, o, mo, w_out, norm_ffn, w_gate_up, w_down, layer):
    t, dm = h.shape
    main_w, mw = o.shape[1], mo.shape[1]
    row = lambda i: (i, 0)
    resident = sum(_nbytes(w.shape[1:], BF16) for w in (w_out, w_gate_up, w_down))
    tile = FFN_TILE
    pipelined = 2 * _nbytes((tile, dm), F32) + _nbytes((tile, main_w + mw), BF16)
    temps = 3 * _nbytes((tile, dm), F32) + 3 * _nbytes((tile, FFN_CHUNK), F32)
    return pl.pallas_call(
        _mix_ffn_kernel,
        out_shape=jax.ShapeDtypeStruct((t, dm), F32),
        grid=(t // tile,),
        in_specs=[
            pl.BlockSpec((tile, dm), row), pl.BlockSpec((tile, main_w), row),
            pl.BlockSpec((tile, mw), row),
            _layer_resident(w_out, layer), _resident((1, dm)), _layer_resident(w_gate_up, layer),
            _layer_resident(w_down, layer),
        ],
        out_specs=pl.BlockSpec((tile, dm), row),
        compiler_params=pltpu.CompilerParams(
            dimension_semantics=("parallel",),
            vmem_limit_bytes=_vmem_limit(pipelined, resident, temps)),
        name="mix_ffn",
    )(h, o, mo, w_out, norm_ffn, w_gate_up, w_down)


def _inproj_b_kernel(*refs, dilations):
    n_groups = len(dilations)
    (x_ref, gain_ref, kvg_ref, w_ref, wkv_ref, qn_ref, kn_ref, cos_ref, sin_ref,
     mqg_ref, mk_ref, mv_ref) = refs[:12]
    q_refs = refs[12:12 + n_groups]
    k_refs = refs[12 + n_groups:12 + 2 * n_groups]
    v_refs = refs[12 + 2 * n_groups:12 + 3 * n_groups]
    mo_ref = refs[12 + 3 * n_groups]
    scratch = refs[13 + 3 * n_groups:]
    q_slabs = dict(zip([d for d in dilations if d != 1], scratch[:-5]))
    k_slab, v_slab, q_tmp, k_tmp, v_tmp = scratch[-5:]
    kv_w = wkv_ref.shape[1] // 2
    bw = (w_ref.shape[1] - mo_ref.shape[1]) // n_groups
    heads = range(bw // HEAD_DIM)
    pair = 2
    q_scale = (HEAD_DIM ** -0.5) * LOG2E

    for row0 in range(0, x_ref.shape[0], ROW_TILE):
        rows = slice(row0, row0 + ROW_TILE)
        x = x_ref[rows, :]
        xhat = x * lax.rsqrt(jnp.mean(x * x, axis=-1, keepdims=True) + EPS)
        xn = (xhat * gain_ref[...]).astype(BF16)
        kn = (xhat * kvg_ref[...]).astype(BF16)
        cos, sin = cos_ref[rows, :], sin_ref[rows, :]
        cos_s, sin_s = cos * q_scale, sin * q_scale

        def queries(gi, row0=row0, rows=rows, xn=xn, cos_s=cos_s, sin_s=sin_s):
            dil, q_ref = dilations[gi], q_refs[gi]
            for hd in heads:
                if hd % pair == 0:
                    col0 = gi * bw + hd * HEAD_DIM
                    qs = _dot(xn, w_ref[:, col0:col0 + pair * HEAD_DIM])
                q = _rope(_rms(qs[:, _head_cols(hd % pair)], qn_ref[gi]), cos_s, sin_s)
                if dil == 1:
                    q_ref[rows, _head_cols(hd)] = q.astype(q_ref.dtype)
                else:
                    q_slabs[dil][hd] = q
            if dil != 1:
                _store_by_residue(q_slabs[dil], q_tmp, [(dil, q_ref)], row0)

        def copies(slab, tmp, out_refs, row0=row0, rows=rows):
            for dil, out_ref in zip(dilations, out_refs):
                if dil == 1:
                    for hd in heads:
                        out_ref[rows, _head_cols(hd)] = slab[hd].astype(out_ref.dtype)
            _store_by_residue(slab, tmp, [(d, ref) for d, ref in zip(dilations, out_refs) if d != 1], row0)

        order = sorted(range(n_groups), key=lambda gi: -dilations[gi])
        queries(order[0])
        probs = _memory_probs(_dot(xn, w_ref[:, n_groups * bw:]), mqg_ref[...], mk_ref)
        for hd in heads:
            if hd % pair == 0:
                k = _dot(kn, wkv_ref[:, hd * HEAD_DIM:(hd + pair) * HEAD_DIM])
            k_slab[hd] = _rope(_rms(k[:, _head_cols(hd % pair)], kn_ref[...]), cos, sin)
        copies(k_slab, k_tmp, k_refs)
        for gi in order[1:-1]:
            queries(gi)
        _memory_output(probs, mv_ref, mo_ref, rows)
        for hd in heads:
            if hd % pair == 0:
                v = _dot(kn, wkv_ref[:, kv_w + hd * HEAD_DIM:kv_w + (hd + pair) * HEAD_DIM])
            v_slab[hd] = v[:, _head_cols(hd % pair)]
        copies(v_slab, v_tmp, v_refs)
        queries(order[-1])


def _residue_out(bn, seq, width, dil):
    tiles_per_seq = seq // ATTN_IN_TILE
    if dil == 1:
        return (jax.ShapeDtypeStruct((bn * seq, width), BF16),
                pl.BlockSpec((ATTN_IN_TILE, width), lambda i: (i, 0)))
    return (jax.ShapeDtypeStruct((bn, dil, seq // dil, width), BF16),
            pl.BlockSpec((None, dil, ATTN_IN_TILE // dil, width),
                         lambda i: (i // tiles_per_seq, 0, i % tiles_per_seq, 0)))


def _inproj_b(h, gain, kv_gain, w, w_kv, q_norm, k_norm, cos2, sin2, mq_gain, mk, mv, layer, bn, seq):
    t, dm = h.shape
    mt, mw = mk.shape[2], mk.shape[3]
    dilations = tuple(d for _, d in DILATED_GROUPS)
    n_groups = len(dilations)
    bw = (w.shape[1] - mw) // n_groups
    kv_w = w_kv.shape[1] // 2
    assert kv_w == bw
    tile = ATTN_IN_TILE
    tiles_per_seq = seq // tile
    row = lambda i: (i, 0)
    pos = lambda i: (i % tiles_per_seq, 0)
    mem = lambda i: (layer, i // tiles_per_seq, 0, 0)
    copies = [_residue_out(bn, seq, bw, dil) for dil in dilations] * 3
    slab = (bw // HEAD_DIM, ROW_TILE, HEAD_DIM)
    n_slabs = sum(1 for dil in dilations if dil != 1) + 5
    pipelined = (_nbytes((tile, dm), F32) + 3 * n_groups * _nbytes((tile, bw), BF16)
                 + 2 * _nbytes((tile, HEAD_DIM), F32) + _nbytes((tile, mw), BF16)
                 + 2 * _nbytes((mt, mw), BF16))
    resident = _nbytes(w.shape, BF16) + _nbytes(w_kv.shape, BF16) + n_slabs * _nbytes(slab, F32)
    return pl.pallas_call(
        functools.partial(_inproj_b_kernel, dilations=dilations),
        out_shape=tuple([shape for shape, _ in copies] + [jax.ShapeDtypeStruct((t, mw), BF16)]),
        grid=(t // tile,),
        in_specs=[
            pl.BlockSpec((tile, dm), row),
            _resident((1, dm)), _resident((1, dm)),
            _resident(w.shape), _resident(w_kv.shape),
            _resident((n_groups, 1, HEAD_DIM)), _resident((1, HEAD_DIM)),
            pl.BlockSpec((tile, HEAD_DIM), pos), pl.BlockSpec((tile, HEAD_DIM), pos),
            _resident((1, V7X_LANES)),
            pl.BlockSpec((None, None, mt, mw), mem),
            pl.BlockSpec((None, None, mt, mw), mem),
        ],
        out_specs=tuple([spec for _, spec in copies] + [pl.BlockSpec((tile, mw), row)]),
        scratch_shapes=[pltpu.VMEM(slab, F32)] * n_slabs,
        compiler_params=pltpu.CompilerParams(
            dimension_semantics=("parallel",),
            vmem_limit_bytes=_vmem_limit(pipelined, resident, 6 * _nbytes((ROW_TILE, bw), F32))),
        name="inproj_b",
    )(h, gain, kv_gain, w, w_kv, q_norm.reshape(n_groups, 1, HEAD_DIM), k_norm, cos2, sin2,
      mq_gain, mk, mv)


def _dilated_kernel(*refs, dilations):
    n_groups = len(dilations)
    q_refs = refs[:n_groups]
    k_refs = refs[n_groups:2 * n_groups]
    v_refs = refs[2 * n_groups:3 * n_groups]
    o_ref, og_ref, lse_ref, bias_ref = refs[3 * n_groups:]
    seq = o_ref.shape[0]
    blk = ATTN_BLOCK

    qi = lax.broadcasted_iota(jnp.int32, (blk, 2 * blk), 0)
    kj = lax.broadcasted_iota(jnp.int32, (blk, 2 * blk), 1)
    band = (kj >= qi) & (kj <= qi + blk)
    bias_ref[0] = jnp.where(band & (kj >= blk), 0.0, -jnp.inf)
    bias_ref[1] = jnp.where(band, 0.0, -jnp.inf)
    bias_ref[2] = jnp.where(kj <= qi, 0.0, -jnp.inf)

    for gi, dil in enumerate(dilations):
        q_ref, k_ref, v_ref = q_refs[gi], k_refs[gi], v_refs[gi]
        n_blocks = seq // (blk * dil)

        def blocks(step, carry, q_ref=q_ref, k_ref=k_ref, v_ref=v_ref, gi=gi, dil=dil, n_blocks=n_blocks):
            idxs = [step * ATTN_UNROLL + u for u in range(ATTN_UNROLL)]
            starts = [pl.multiple_of(idx * blk, blk) for idx in idxs]
            windows = [pl.ds(pl.multiple_of(jnp.maximum(start - blk, 0), blk), 2 * blk) for start in starts]
            scores = [lax.dot_general(q_ref[pl.ds(start, blk), :], k_ref[window, :], _NT,
                                      preferred_element_type=F32)
                      for start, window in zip(starts, windows)]
            soft = []
            for idx, s in zip(idxs, scores):
                n = lax.rem(idx, n_blocks)
                s = s + bias_ref[jnp.where(idx == 0, 2, jnp.minimum(n, 1))]
                m = jnp.max(s, axis=-1, keepdims=True)
                p = jnp.exp2(s - m)
                denom = jnp.sum(p, axis=-1, keepdims=True)
                soft.append((p.astype(BF16), m, denom))
            for idx, start, window, (p, m, denom) in zip(idxs, starts, windows, soft):
                o = _dot(p, v_ref[window, :]) / denom
                lse2 = jnp.broadcast_to(m + jnp.log(denom) * LOG2E, (blk, HEAD_DIM))
                if dil == 1:
                    out_rows = pl.ds(start, blk)
                else:
                    n = lax.rem(idx, n_blocks)
                    out_rows = pl.ds(n * (blk * dil) + lax.div(idx, n_blocks), blk, stride=dil)
                og_ref[gi, out_rows, :] = o
                lse_ref[gi, out_rows, :] = lse2
            return carry

        lax.fori_loop(0, dil * n_blocks // ATTN_UNROLL, blocks, 0)

    def merge(step, carry):
        rows = pl.ds(pl.multiple_of(step * MERGE_ROWS, MERGE_ROWS), MERGE_ROWS)
        lses = [lse_ref[gi, rows, :] for gi in range(n_groups)]
        top = functools.reduce(jnp.maximum, lses)
        ws = [jnp.exp2(l - top) for l in lses]
        acc = sum(w * og_ref[gi, rows, :] for gi, w in enumerate(ws))
        o_ref[rows, :] = (acc / sum(ws)).astype(o_ref.dtype)
        return carry

    lax.fori_loop(0, seq // MERGE_ROWS, merge, 0)


def _dilated_attention(qs, ks, vs, bn, seq):
    width = qs[0].shape[-1]
    heads = width // HEAD_DIM
    dilations = tuple(d for _, d in DILATED_GROUPS)
    n_groups = len(dilations)
    as_seq = lambda a: a.reshape(bn, seq, width)
    spec = pl.BlockSpec((None, seq, HEAD_DIM), lambda b, h: (b, 0, h))
    seq_bf16 = _nbytes((seq, HEAD_DIM), BF16)
    seq_f32 = _nbytes((seq, HEAD_DIM), F32)
    n_bias = 3
    scratch = 2 * n_groups * seq_f32 + _nbytes((n_bias, ATTN_BLOCK, 2 * ATTN_BLOCK), F32)
    staged = ATTN_UNROLL * (_nbytes((ATTN_BLOCK, 2 * ATTN_BLOCK), F32) + _nbytes((ATTN_BLOCK, 2 * ATTN_BLOCK), BF16)
                            + 2 * _nbytes((ATTN_BLOCK, HEAD_DIM), F32))
    staged = max(staged, 8 * _nbytes((MERGE_ROWS, HEAD_DIM), F32))
    out = pl.pallas_call(
        functools.partial(_dilated_kernel, dilations=dilations),
        out_shape=jax.ShapeDtypeStruct((bn, seq, width), BF16),
        grid=(bn, heads),
        in_specs=[spec] * (3 * n_groups),
        out_specs=spec,
        scratch_shapes=[
            pltpu.VMEM((n_groups, seq, HEAD_DIM), F32),
            pltpu.VMEM((n_groups, seq, HEAD_DIM), F32),
            pltpu.VMEM((n_bias, ATTN_BLOCK, 2 * ATTN_BLOCK), F32),
        ],
        compiler_params=pltpu.CompilerParams(
            dimension_semantics=("parallel", "parallel"),
            vmem_limit_bytes=_vmem_limit((3 * n_groups + 1) * seq_bf16, scratch, staged)),
        name="dilated_attention",
    )(*[as_seq(a) for a in (*qs, *ks, *vs)])
    return out.reshape(bn * seq, width)


def _rope_tables(seq):
    half = HEAD_DIM // 2
    inv = ROPE_THETA ** (-jnp.arange(half, dtype=F32) / half)
    ang = jnp.arange(seq).astype(F32)[:, None] * inv[None, :]
    cos, sin = jnp.cos(ang), jnp.sin(ang)
    return jnp.concatenate([cos, cos], axis=-1), jnp.concatenate([-sin, sin], axis=-1)


def kernel(x, mem, norm_mix, norm_ffn, a_w_in, a_lb_logits, a_onorm, b_w_in, b_qnorm, kv_norm, w_kv,
           b_knorm, mem_norm, w_mem_kv, mem_qnorm, mem_knorm, w_out, w_gate_up, w_down):
    bn, seq, dm = x.shape
    depth = norm_mix.shape[0]
    n_a = a_w_in.shape[0]
    max_dil = max(d for _, d in DILATED_GROUPS)
    assert seq % (ATTN_BLOCK * max_dil) == 0 and seq % HGRN_ROWS == 0 and seq % ROW_TILE == 0
    assert seq % (ATTN_BLOCK * ATTN_UNROLL) == 0 and HGRN_ROWS % CHUNK == 0 and ROW_TILE % max_dil == 0
    assert all(w == ATTN_BLOCK * d for w, d in DILATED_GROUPS)
    assert seq % ATTN_IN_TILE == 0 and ATTN_IN_TILE % ROW_TILE == 0
    assert depth - n_a == 1

    bf = lambda a: a.astype(BF16)
    row_vec = lambda a: a.reshape(1, -1)
    pair = lambda a: jnp.concatenate([a, a], axis=-1).reshape(1, V7X_LANES)
    cos2, sin2 = _rope_tables(seq)
    n_groups = len(DILATED_GROUPS)

    w_out_bf, w_gate_up_bf, w_down_bf = bf(w_out), bf(w_gate_up), bf(w_down)
    mk, mv = _mem_kv(mem, mem_norm, bf(w_mem_kv), mem_knorm)
    h = x.reshape(bn * seq, dm)
    for l in range(depth):
        gain = row_vec(norm_mix[l])
        if l < n_a:
            qs, lf, k, v, gate, mo = _inproj_a(h, gain, bf(a_w_in[l]), a_lb_logits, pair(mem_qnorm[l]),
                                               mk, mv, l, seq)
            o = _hgrn2(qs, lf, k, v, gate, row_vec(a_onorm[l]), bn, seq)
        else:
            j = l - n_a
            *copies, mo = _inproj_b(h, gain, row_vec(kv_norm), bf(b_w_in[j]), bf(w_kv), b_qnorm[j],
                                    row_vec(b_knorm), cos2, sin2, pair(mem_qnorm[l]), mk, mv, l, bn, seq)
            q_groups, ks, vs = (copies[i * n_groups:(i + 1) * n_groups] for i in range(3))
            o = _dilated_attention(q_groups, ks, vs, bn, seq)
        h = _mix_ffn(h, o, mo, w_out_bf, row_vec(norm_ffn[l]), w_gate_up_bf, w_down_bf, l)
    return h.reshape(bn, seq, dm)
```

```python
import functools
import math

import jax
import jax.numpy as jnp
from jax import lax
from jax.experimental import pallas as pl
from jax.experimental.pallas import tpu as pltpu

F32 = jnp.float32
BF16 = jnp.bfloat16

EPS = 1e-6
HEAD_DIM = 128
CHUNK = 64
MEM_HEAD_DIM = 64
DILATED_GROUPS = ((128, 1), (512, 4), (2048, 16))
ROPE_THETA = 10000.0
LOG2E = math.log2(math.e)

V7X_LANES = 128
V7X_VMEM_SCOPED_MAX_BYTES = 60000 * 1024

ROW_TILE = 256
ATTN_IN_TILE = 512
HGRN_IN_TILE = 512
HGRN_IN_SUB = 512
RESIDUE_STRIDE = 4
FFN_TILE = 512
FFN_CHUNK = 2816
HGRN_ROWS = 4096
ATTN_BLOCK = 128
ATTN_UNROLL = 16

_NT = (((1,), (1,)), ((), ()))
_TN = (((0,), (0,)), ((), ()))


def _vmem_limit(pipelined_bytes, resident_bytes, temp_bytes):
    need = 2 * pipelined_bytes + resident_bytes + temp_bytes
    return int(min(max(need, 16 * 1024 * 1024), V7X_VMEM_SCOPED_MAX_BYTES))


def _nbytes(shape, dtype):
    n = 1
    for s in shape:
        n *= s
    return n * jnp.dtype(dtype).itemsize


def _resident(shape):
    zeros = (0,) * len(shape)
    return pl.BlockSpec(shape, lambda *_: zeros, pipeline_mode=pl.Buffered(1))


def _dot(a, b):
    return jnp.dot(a, b, preferred_element_type=F32)


def _rms(x, gain):
    ms = jnp.mean(x * x, axis=-1, keepdims=True)
    return x * lax.rsqrt(ms + EPS) * gain


def _silu(x):
    return x * jax.nn.sigmoid(x)


def _head_cols(hd):
    return slice(hd * HEAD_DIM, (hd + 1) * HEAD_DIM)


def _rms_head_pairs(x, gain):
    lo = lax.broadcasted_iota(jnp.int32, x.shape, 1) < MEM_HEAD_DIM
    x2 = x * x
    s_lo = jnp.sum(jnp.where(lo, x2, 0.0), axis=-1, keepdims=True)
    s_hi = jnp.sum(jnp.where(lo, 0.0, x2), axis=-1, keepdims=True)
    ms = jnp.where(lo, s_lo, s_hi) * (1.0 / MEM_HEAD_DIM)
    return x * lax.rsqrt(ms + EPS) * gain


def _rope(x, cos2, sin2):
    return x * cos2 + pltpu.roll(x, HEAD_DIM // 2, axis=1) * sin2


def _store_by_residue(slab_ref, tmp_ref, dil_outs, row0):
    n_heads, tile_rows = slab_ref.shape[0], slab_ref.shape[1]
    base = RESIDUE_STRIDE
    two_hops = any(dil > base for dil, _ in dil_outs)
    if two_hops:
        part = tile_rows // base
        for r in range(base):
            for hd in range(n_heads):
                tmp_ref[hd, r * part:(r + 1) * part, :] = slab_ref[hd, pl.ds(r, part, stride=base), :]
    for dil, out_ref in dil_outs:
        rows = tile_rows // dil
        dst = slice(row0 // dil, row0 // dil + rows)
        for r in range(dil):
            for hd in range(n_heads):
                if dil < base or (dil == base and not two_hops):
                    piece = slab_ref[hd, pl.ds(r, rows, stride=dil), :]
                elif dil == base:
                    piece = tmp_ref[hd, r * rows:(r + 1) * rows, :]
                else:
                    assert dil % base == 0 and dil // base <= base
                    piece = tmp_ref[hd, pl.ds((r % base) * part + r // base, rows, stride=dil // base), :]
                out_ref[r, dst, _head_cols(hd)] = piece.astype(out_ref.dtype)


def _memory_probs(mq, qgain, mk_ref):
    scaled_gain = qgain * (MEM_HEAD_DIM ** -0.5)
    probs = []
    for t in range(mq.shape[1] // V7X_LANES):
        cols = slice(t * V7X_LANES, (t + 1) * V7X_LANES)
        qn = _rms_head_pairs(mq[:, cols], scaled_gain)
        lo = lax.broadcasted_iota(jnp.int32, qn.shape, 1) < MEM_HEAD_DIM
        for keep in (lo, jnp.logical_not(lo)):
            qh = jnp.where(keep, qn, 0.0).astype(BF16)
            s = lax.dot_general(qh, mk_ref[:, cols], _NT, preferred_element_type=F32)
            p = jnp.exp(s - jnp.max(s, axis=-1, keepdims=True))
            probs.append((p.astype(BF16), jnp.sum(p, axis=-1, keepdims=True)))
    return probs


def _memory_output(probs, mv_ref, mo_ref, rows=slice(None)):
    for t in range(mo_ref.shape[1] // V7X_LANES):
        cols = slice(t * V7X_LANES, (t + 1) * V7X_LANES)
        outs = [_dot(p, mv_ref[:, cols]) / denom for p, denom in probs[2 * t:2 * t + 2]]
        lo = lax.broadcasted_iota(jnp.int32, outs[0].shape, 1) < MEM_HEAD_DIM
        mo_ref[rows, cols] = jnp.where(lo, outs[0], outs[1]).astype(mo_ref.dtype)


def _mem_kv_kernel(mem_ref, gain_ref, w_ref, kgain_ref, mk_ref, mv_ref):
    mw = mk_ref.shape[1]
    mn = _rms(mem_ref[...], gain_ref[...]).astype(BF16)
    kv = _dot(mn, w_ref[...])
    for t in range(mw // V7X_LANES):
        cols = slice(t * V7X_LANES, (t + 1) * V7X_LANES)
        mk_ref[:, cols] = _rms_head_pairs(kv[:, cols], kgain_ref[...]).astype(mk_ref.dtype)
    mv_ref[...] = kv[:, mw:].astype(mv_ref.dtype)


def _mem_kv(mem, mem_norm, w_mem_kv, mem_knorm):
    bn, mt, dm = mem.shape
    depth = w_mem_kv.shape[0]
    mw = w_mem_kv.shape[2] // 2
    kgain = jnp.concatenate([mem_knorm, mem_knorm], axis=-1).reshape(depth, 1, V7X_LANES)
    out = jax.ShapeDtypeStruct((depth, bn, mt, mw), BF16)
    return pl.pallas_call(
        _mem_kv_kernel,
        out_shape=(out, out),
        grid=(depth, bn),
        in_specs=[
            pl.BlockSpec((None, mt, dm), lambda l, b: (b, 0, 0)),
            pl.BlockSpec((None, 1, dm), lambda l, b: (l, 0, 0)),
            pl.BlockSpec((None, dm, 2 * mw), lambda l, b: (l, 0, 0)),
            pl.BlockSpec((None, 1, V7X_LANES), lambda l, b: (l, 0, 0)),
        ],
        out_specs=(
            pl.BlockSpec((None, None, mt, mw), lambda l, b: (l, b, 0, 0)),
            pl.BlockSpec((None, None, mt, mw), lambda l, b: (l, b, 0, 0)),
        ),
        name="mem_kv",
    )(mem, mem_norm.reshape(depth, 1, dm), w_mem_kv, kgain)


def _inproj_a_kernel(x_ref, gain_ref, w_ref, lbl_ref, mqg_ref, mk_ref, mv_ref,
                     qs_ref, lf_ref, k_ref, v_ref, gate_ref, mo_ref, *, layer):
    aw = qs_ref.shape[1]
    lg = lbl_ref[...]
    e = jnp.exp(lg - jnp.max(lg, axis=0, keepdims=True))
    lb = jnp.sum(e[:layer + 1], axis=0, keepdims=True) / jnp.sum(e, axis=0, keepdims=True)
    for row0 in range(0, x_ref.shape[0], HGRN_IN_SUB):
        rows = slice(row0, row0 + HGRN_IN_SUB)
        xn = _rms(x_ref[rows, :], gain_ref[...]).astype(BF16)
        probs = _memory_probs(_dot(xn, w_ref[:, 4 * aw:]), mqg_ref[...], mk_ref)
        qs_ref[rows, :] = _silu(_dot(xn, w_ref[:, 0:aw])).astype(qs_ref.dtype)
        _memory_output(probs, mv_ref, mo_ref, rows)
        f = lb + (1.0 - lb) * jax.nn.sigmoid(_dot(xn, w_ref[:, aw:2 * aw]))
        lf_ref[rows, :] = jnp.log(f) * LOG2E
        k_ref[rows, :] = (1.0 - f).astype(k_ref.dtype)
        v_ref[rows, :] = _dot(xn, w_ref[:, 2 * aw:3 * aw]).astype(v_ref.dtype)
        gate_ref[rows, :] = _silu(_dot(xn, w_ref[:, 3 * aw:4 * aw])).astype(gate_ref.dtype)


def _inproj_a(h, gain, w, lb_logits, mq_gain, mk, mv, layer, seq):
    t, dm = h.shape
    mt, mw = mk.shape[2], mk.shape[3]
    aw = (w.shape[1] - mw) // 4
    tile = HGRN_IN_TILE
    tiles_per_seq = seq // tile
    row = lambda i: (i, 0)
    mem = lambda i: (layer, i // tiles_per_seq, 0, 0)
    wide = functools.partial(jax.ShapeDtypeStruct, (t, aw))
    pipelined = (_nbytes((tile, dm), F32) + _nbytes((tile, aw), F32)
                 + 4 * _nbytes((tile, aw), BF16) + _nbytes((tile, mw), BF16)
                 + 2 * _nbytes((mt, mw), BF16))
    return pl.pallas_call(
        functools.partial(_inproj_a_kernel, layer=layer),
        out_shape=(wide(BF16), wide(F32), wide(BF16), wide(BF16), wide(BF16),
                   jax.ShapeDtypeStruct((t, mw), BF16)),
        grid=(t // tile,),
        in_specs=[
            pl.BlockSpec((tile, dm), row),
            _resident((1, dm)),
            _resident(w.shape),
            _resident(lb_logits.shape),
            _resident((1, V7X_LANES)),
            pl.BlockSpec((None, None, mt, mw), mem),
            pl.BlockSpec((None, None, mt, mw), mem),
        ],
        out_specs=tuple([pl.BlockSpec((tile, aw), row)] * 5 + [pl.BlockSpec((tile, mw), row)]),
        compiler_params=pltpu.CompilerParams(
            dimension_semantics=("parallel",),
            vmem_limit_bytes=_vmem_limit(pipelined, _nbytes(w.shape, BF16),
                                         4 * _nbytes((ROW_TILE, aw), F32))),
        name="inproj_a",
    )(h, gain, w, lb_logits, mq_gain, mk, mv)


def _hgrn2_kernel(on_ref, qs_ref, lf_ref, k_ref, v_ref, gate_ref, o_ref, state_ref):
    @pl.when(pl.program_id(2) == 0)
    def _():
        state_ref[...] = jnp.zeros_like(state_ref)

    row = lax.broadcasted_iota(jnp.int32, (CHUNK, HEAD_DIM), 0)
    causal = (lax.broadcasted_iota(jnp.int32, (CHUNK, CHUNK), 0)
              >= lax.broadcasted_iota(jnp.int32, (CHUNK, CHUNK), 1))
    onorm = on_ref[...]

    chunks = [slice(c * CHUNK, (c + 1) * CHUNK) for c in range(qs_ref.shape[0] // CHUNK)]
    q_ins, vs, decays, atts, kvs = [], [], [], [], []
    for rows in chunks:
        b = lf_ref[rows, :]
        shift = 1
        while shift < CHUNK:
            b = b + jnp.where(row >= shift, pltpu.roll(b, shift, axis=0), 0.0)
            shift *= 2
        b_end = b[CHUNK - 1:CHUNK, :]
        k = k_ref[rows, :].astype(F32)
        q_in = (qs_ref[rows, :].astype(F32) * jnp.exp2(b)).astype(BF16)
        k_in = (k * jnp.exp2(-b)).astype(BF16)
        k_out = (k * jnp.exp2(b_end - b)).astype(BF16)
        v = v_ref[rows, :]
        q_ins.append(q_in)
        vs.append(v)
        decays.append(jnp.exp2(b_end))
        atts.append(lax.dot_general(q_in, k_in, _NT, preferred_element_type=F32))
        kvs.append(lax.dot_general(v, k_out, _TN, preferred_element_type=F32))

    state_t = state_ref[...]
    states = []
    for decay, kv in zip(decays, kvs):
        states.append(state_t.astype(BF16))
        state_t = state_t * decay + kv
    state_ref[...] = state_t

    for rows, q_in, v, att, state_in in zip(chunks, q_ins, vs, atts, states):
        att = jnp.where(causal, att, 0.0).astype(BF16)
        o = _dot(att, v) + lax.dot_general(q_in, state_in, _NT, preferred_element_type=F32)
        o_ref[rows, :] = (_rms(o, onorm) * gate_ref[rows, :].astype(F32)).astype(o_ref.dtype)


def _hgrn2(qs, lf, k, v, gate, onorm, bn, seq):
    t, aw = qs.shape
    heads = aw // HEAD_DIM
    step_rows = min(HGRN_ROWS, seq)
    steps = seq // step_rows
    spec = pl.BlockSpec((step_rows, HEAD_DIM), lambda b, h, s: (b * steps + s, h))
    return pl.pallas_call(
        _hgrn2_kernel,
        out_shape=jax.ShapeDtypeStruct((t, aw), BF16),
        grid=(bn, heads, steps),
        in_specs=[pl.BlockSpec((1, HEAD_DIM), lambda b, h, s: (0, h)), spec, spec, spec, spec, spec],
        out_specs=spec,
        scratch_shapes=[pltpu.VMEM((HEAD_DIM, HEAD_DIM), F32)],
        compiler_params=pltpu.CompilerParams(
            dimension_semantics=("parallel", "parallel", "arbitrary")),
        name="hgrn2",
    )(onorm, qs, lf, k, v, gate)


def _mix_ffn_kernel(h_ref, o_ref, mo_ref, wo_ref, nf_ref, wgu_ref, wd_ref, h_out):
    main_w = o_ref.shape[1]
    hidden = wd_ref.shape[0]
    h = h_ref[...] + _dot(o_ref[...], wo_ref[0:main_w, :]) + _dot(mo_ref[...], wo_ref[main_w:, :])
    hn = _rms(h, nf_ref[...]).astype(BF16)
    for c in range(hidden // FFN_CHUNK):
        cols = slice(c * FFN_CHUNK, (c + 1) * FFN_CHUNK)
        up_cols = slice(hidden + c * FFN_CHUNK, hidden + (c + 1) * FFN_CHUNK)
        act = (_silu(_dot(hn, wgu_ref[:, cols])) * _dot(hn, wgu_ref[:, up_cols])).astype(BF16)
        h = h + _dot(act, wd_ref[cols, :])
    h_out[...] = h


def _layer_resident(stacked, layer):
    zeros = (0,) * (stacked.ndim - 1)
    return pl.BlockSpec((None,) + stacked.shape[1:], lambda *_: (layer,) + zeros,
                        pipeline_mode=pl.Buffered(1))


def _mix_ffn(h, o, mo, w_out, norm_ffn, w_gate_up, w_down, layer):
    t, dm = h.shape
    main_w, mw = o.shape[1], mo.shape[1]
    row = lambda i: (i, 0)
    resident = sum(_nbytes(w.shape[1:], BF16) for w in (w_out, w_gate_up, w_down))
    tile = FFN_TILE
    pipelined = 2 * _nbytes((tile, dm), F32) + _nbytes((tile, main_w + mw), BF16)
    temps = 3 * _nbytes((tile, dm), F32) + 3 * _nbytes((tile, FFN_CHUNK), F32)
    return pl.pallas_call(
        _mix_ffn_kernel,
        out_shape=jax.ShapeDtypeStruct((t, dm), F32),
        grid=(t // tile,),
        in_specs=[
            pl.BlockSpec((tile, dm), row), pl.BlockSpec((tile, main_w), row),
            pl.BlockSpec((tile, mw), row),
            _layer_resident(w_out, layer), _resident((1, dm)), _layer_resident(w_gate_up, layer),
            _layer_resident(w_down, layer),
        ],
        out_specs=pl.BlockSpec((tile, dm), row),
        compiler_params=pltpu.CompilerParams(
            dimension_semantics=("parallel",),
            vmem_limit_bytes=_vmem_limit(pipelined, resident, temps)),
        name="mix_ffn",
    )(h, o, mo, w_out, norm_ffn, w_gate_up, w_down)


def _inproj_b_kernel(*refs, dilations):
    n_groups = len(dilations)
    (x_ref, gain_ref, kvg_ref, w_ref, wkv_ref, qn_ref, kn_ref, cos_ref, sin_ref,
     mqg_ref, mk_ref, mv_ref) = refs[:12]
    q_refs = refs[12:12 + n_groups]
    k_refs = refs[12 + n_groups:12 + 2 * n_groups]
    v_refs = refs[12 + 2 * n_groups:12 + 3 * n_groups]
    mo_ref = refs[12 + 3 * n_groups]
    scratch = refs[13 + 3 * n_groups:]
    q_slabs = dict(zip([d for d in dilations if d != 1], scratch[:-5]))
    k_slab, v_slab, q_tmp, k_tmp, v_tmp = scratch[-5:]
    kv_w = wkv_ref.shape[1] // 2
    bw = (w_ref.shape[1] - mo_ref.shape[1]) // n_groups
    heads = range(bw // HEAD_DIM)
    pair = 2
    q_scale = (HEAD_DIM ** -0.5) * LOG2E

    for row0 in range(0, x_ref.shape[0], ROW_TILE):
        rows = slice(row0, row0 + ROW_TILE)
        x = x_ref[rows, :]
        xhat = x * lax.rsqrt(jnp.mean(x * x, axis=-1, keepdims=True) + EPS)
        xn = (xhat * gain_ref[...]).astype(BF16)
        kn = (xhat * kvg_ref[...]).astype(BF16)
        cos, sin = cos_ref[rows, :], sin_ref[rows, :]
        cos_s, sin_s = cos * q_scale, sin * q_scale

        def queries(gi, row0=row0, rows=rows, xn=xn, cos_s=cos_s, sin_s=sin_s):
            dil, q_ref = dilations[gi], q_refs[gi]
            for hd in heads:
                if hd % pair == 0:
                    col0 = gi * bw + hd * HEAD_DIM
                    qs = _dot(xn, w_ref[:, col0:col0 + pair * HEAD_DIM])
                q = _rope(_rms(qs[:, _head_cols(hd % pair)], qn_ref[gi]), cos_s, sin_s)
                if dil == 1:
                    q_ref[rows, _head_cols(hd)] = q.astype(q_ref.dtype)
                else:
                    q_slabs[dil][hd] = q
            if dil != 1:
                _store_by_residue(q_slabs[dil], q_tmp, [(dil, q_ref)], row0)

        def copies(slab, tmp, out_refs, row0=row0, rows=rows):
            for dil, out_ref in zip(dilations, out_refs):
                if dil == 1:
                    for hd in heads:
                        out_ref[rows, _head_cols(hd)] = slab[hd].astype(out_ref.dtype)
            _store_by_residue(slab, tmp, [(d, ref) for d, ref in zip(dilations, out_refs) if d != 1], row0)

        order = sorted(range(n_groups), key=lambda gi: -dilations[gi])
        queries(order[0])
        probs = _memory_probs(_dot(xn, w_ref[:, n_groups * bw:]), mqg_ref[...], mk_ref)
        for hd in heads:
            if hd % pair == 0:
                k = _dot(kn, wkv_ref[:, hd * HEAD_DIM:(hd + pair) * HEAD_DIM])
            k_slab[hd] = _rope(_rms(k[:, _head_cols(hd % pair)], kn_ref[...]), cos, sin)
        copies(k_slab, k_tmp, k_refs)
        for gi in order[1:-1]:
            queries(gi)
        _memory_output(probs, mv_ref, mo_ref, rows)
        for hd in heads:
            if hd % pair == 0:
                v = _dot(kn, wkv_ref[:, kv_w + hd * HEAD_DIM:kv_w + (hd + pair) * HEAD_DIM])
            v_slab[hd] = v[:, _head_cols(hd % pair)]
        copies(v_slab, v_tmp, v_refs)
        queries(order[-1])


def _residue_out(bn, seq, width, dil):
    tiles_per_seq = seq // ATTN_IN_TILE
    if dil == 1:
        return (jax.ShapeDtypeStruct((bn * seq, width), BF16),
                pl.BlockSpec((ATTN_IN_TILE, width), lambda i: (i, 0)))
    return (jax.ShapeDtypeStruct((bn, dil, seq // dil, width), BF16),
            pl.BlockSpec((None, dil, ATTN_IN_TILE // dil, width),
                         lambda i: (i // tiles_per_seq, 0, i % tiles_per_seq, 0)))


def _inproj_b(h, gain, kv_gain, w, w_kv, q_norm, k_norm, cos2, sin2, mq_gain, mk, mv, layer, bn, seq):
    t, dm = h.shape
    mt, mw = mk.shape[2], mk.shape[3]
    dilations = tuple(d for _, d in DILATED_GROUPS)
    n_groups = len(dilations)
    bw = (w.shape[1] - mw) // n_groups
    kv_w = w_kv.shape[1] // 2
    assert kv_w == bw
    tile = ATTN_IN_TILE
    tiles_per_seq = seq // tile
    row = lambda i: (i, 0)
    pos = lambda i: (i % tiles_per_seq, 0)
    mem = lambda i: (layer, i // tiles_per_seq, 0, 0)
    copies = [_residue_out(bn, seq, bw, dil) for dil in dilations] * 3
    slab = (bw // HEAD_DIM, ROW_TILE, HEAD_DIM)
    n_slabs = sum(1 for dil in dilations if dil != 1) + 5
    pipelined = (_nbytes((tile, dm), F32) + 3 * n_groups * _nbytes((tile, bw), BF16)
                 + 2 * _nbytes((tile, HEAD_DIM), F32) + _nbytes((tile, mw), BF16)
                 + 2 * _nbytes((mt, mw), BF16))
    resident = _nbytes(w.shape, BF16) + _nbytes(w_kv.shape, BF16) + n_slabs * _nbytes(slab, F32)
    return pl.pallas_call(
        functools.partial(_inproj_b_kernel, dilations=dilations),
        out_shape=tuple([shape for shape, _ in copies] + [jax.ShapeDtypeStruct((t, mw), BF16)]),
        grid=(t // tile,),
        in_specs=[
            pl.BlockSpec((tile, dm), row),
            _resident((1, dm)), _resident((1, dm)),
            _resident(w.shape), _resident(w_kv.shape),
            _resident((n_groups, 1, HEAD_DIM)), _resident((1, HEAD_DIM)),
            pl.BlockSpec((tile, HEAD_DIM), pos), pl.BlockSpec((tile, HEAD_DIM), pos),
            _resident((1, V7X_LANES)),
            pl.BlockSpec((None, None, mt, mw), mem),
            pl.BlockSpec((None, None, mt, mw), mem),
        ],
        out_specs=tuple([spec for _, spec in copies] + [pl.BlockSpec((tile, mw), row)]),
        scratch_shapes=[pltpu.VMEM(slab, F32)] * n_slabs,
        compiler_params=pltpu.CompilerParams(
            dimension_semantics=("parallel",),
            vmem_limit_bytes=_vmem_limit(pipelined, resident, 6 * _nbytes((ROW_TILE, bw), F32))),
        name="inproj_b",
    )(h, gain, kv_gain, w, w_kv, q_norm.reshape(n_groups, 1, HEAD_DIM), k_norm, cos2, sin2,
      mq_gain, mk, mv)


def _dilated_kernel(*refs, dilations):
    n_groups = len(dilations)
    q_refs = refs[:n_groups]
    k_refs = refs[n_groups:2 * n_groups]
    v_refs = refs[2 * n_groups:3 * n_groups]
    o_ref, og_ref, lse_ref, bias_ref = refs[3 * n_groups:]
    seq = o_ref.shape[0]
    blk = ATTN_BLOCK

    qi = lax.broadcasted_iota(jnp.int32, (blk, 2 * blk), 0)
    kj = lax.broadcasted_iota(jnp.int32, (blk, 2 * blk), 1)
    band = (kj >= qi) & (kj <= qi + blk)
    bias_ref[0] = jnp.where(band & (kj >= blk), 0.0, -jnp.inf)
    bias_ref[1] = jnp.where(band, 0.0, -jnp.inf)
    bias_ref[2] = jnp.where(kj <= qi, 0.0, -jnp.inf)

    parked = [gi for gi, dil in enumerate(dilations) if dil != 1]
    (last,) = [gi for gi, dil in enumerate(dilations) if dil == 1]
    for gi in parked + [last]:
        dil = dilations[gi]
        q_ref, k_ref, v_ref = q_refs[gi], k_refs[gi], v_refs[gi]
        n_blocks = seq // (blk * dil)

        def blocks(step, carry, q_ref=q_ref, k_ref=k_ref, v_ref=v_ref, gi=gi, dil=dil, n_blocks=n_blocks):
            idxs = [step * ATTN_UNROLL + u for u in range(ATTN_UNROLL)]
            starts = [pl.multiple_of(idx * blk, blk) for idx in idxs]
            windows = [pl.ds(pl.multiple_of(jnp.maximum(start - blk, 0), blk), 2 * blk) for start in starts]
            scores = [lax.dot_general(q_ref[pl.ds(start, blk), :], k_ref[window, :], _NT,
                                      preferred_element_type=F32)
                      for start, window in zip(starts, windows)]
            soft = []
            for idx, s in zip(idxs, scores):
                n = lax.rem(idx, n_blocks)
                s = s + bias_ref[jnp.where(idx == 0, 2, jnp.minimum(n, 1))]
                m = jnp.max(s, axis=-1, keepdims=True)
                p = jnp.exp2(s - m)
                denom = jnp.sum(p, axis=-1, keepdims=True)
                soft.append((p.astype(BF16), m, denom))
            for idx, start, window, (p, m, denom) in zip(idxs, starts, windows, soft):
                o = _dot(p, v_ref[window, :]) / denom
                lse2 = jnp.broadcast_to(m + jnp.log(denom) * LOG2E, (blk, HEAD_DIM))
                if dil != 1:
                    n = lax.rem(idx, n_blocks)
                    out_rows = pl.ds(n * (blk * dil) + lax.div(idx, n_blocks), blk, stride=dil)
                    slot = parked.index(gi)
                    og_ref[slot, out_rows, :] = o
                    lse_ref[slot, out_rows, :] = lse2
                else:
                    rows = pl.ds(start, blk)
                    lses = [lse2] + [lse_ref[slot, rows, :] for slot in range(len(parked))]
                    outs = [o] + [og_ref[slot, rows, :] for slot in range(len(parked))]
                    top = functools.reduce(jnp.maximum, lses)
                    ws = [jnp.exp2(l - top) for l in lses]
                    acc = sum(w * og for w, og in zip(ws, outs))
                    o_ref[rows, :] = (acc / sum(ws)).astype(o_ref.dtype)
            return carry

        lax.fori_loop(0, dil * n_blocks // ATTN_UNROLL, blocks, 0)


def _dilated_attention(qs, ks, vs, bn, seq):
    width = qs[0].shape[-1]
    heads = width // HEAD_DIM
    dilations = tuple(d for _, d in DILATED_GROUPS)
    n_groups = len(dilations)
    as_seq = lambda a: a.reshape(bn, seq, width)
    spec = pl.BlockSpec((None, seq, HEAD_DIM), lambda b, h: (b, 0, h))
    seq_bf16 = _nbytes((seq, HEAD_DIM), BF16)
    seq_f32 = _nbytes((seq, HEAD_DIM), F32)
    n_bias = 3
    scratch = 2 * (n_groups - 1) * seq_f32 + _nbytes((n_bias, ATTN_BLOCK, 2 * ATTN_BLOCK), F32)
    staged = ATTN_UNROLL * (_nbytes((ATTN_BLOCK, 2 * ATTN_BLOCK), F32) + _nbytes((ATTN_BLOCK, 2 * ATTN_BLOCK), BF16)
                            + 2 * _nbytes((ATTN_BLOCK, HEAD_DIM), F32))
    out = pl.pallas_call(
        functools.partial(_dilated_kernel, dilations=dilations),
        out_shape=jax.ShapeDtypeStruct((bn, seq, width), BF16),
        grid=(bn, heads),
        in_specs=[spec] * (3 * n_groups),
        out_specs=spec,
        scratch_shapes=[
            pltpu.VMEM((n_groups - 1, seq, HEAD_DIM), F32),
            pltpu.VMEM((n_groups - 1, seq, HEAD_DIM), F32),
            pltpu.VMEM((n_bias, ATTN_BLOCK, 2 * ATTN_BLOCK), F32),
        ],
        compiler_params=pltpu.CompilerParams(
            dimension_semantics=("parallel", "parallel"),
            vmem_limit_bytes=_vmem_limit((3 * n_groups + 1) * seq_bf16, scratch, staged)),
        name="dilated_attention",
    )(*[as_seq(a) for a in (*qs, *ks, *vs)])
    return out.reshape(bn * seq, width)


def _rope_tables(seq):
    half = HEAD_DIM // 2
    inv = ROPE_THETA ** (-jnp.arange(half, dtype=F32) / half)
    ang = jnp.arange(seq).astype(F32)[:, None] * inv[None, :]
    cos, sin = jnp.cos(ang), jnp.sin(ang)
    return jnp.concatenate([cos, cos], axis=-1), jnp.concatenate([-sin, sin], axis=-1)


def kernel(x, mem, norm_mix, norm_ffn, a_w_in, a_lb_logits, a_onorm, b_w_in, b_qnorm, kv_norm, w_kv,
           b_knorm, mem_norm, w_mem_kv, mem_qnorm, mem_knorm, w_out, w_gate_up, w_down):
    bn, seq, dm = x.shape
    depth = norm_mix.shape[0]
    n_a = a_w_in.shape[0]
    max_dil = max(d for _, d in DILATED_GROUPS)
    assert seq % (ATTN_BLOCK * max_dil) == 0 and seq % min(HGRN_ROWS, seq) == 0 and seq % ROW_TILE == 0
    assert seq % HGRN_IN_TILE == 0 and HGRN_IN_TILE % HGRN_IN_SUB == 0 and seq % FFN_TILE == 0
    assert seq % (ATTN_BLOCK * ATTN_UNROLL) == 0 and HGRN_ROWS % CHUNK == 0 and ROW_TILE % max_dil == 0
    assert all(w == ATTN_BLOCK * d for w, d in DILATED_GROUPS)
    assert seq % ATTN_IN_TILE == 0 and ATTN_IN_TILE % ROW_TILE == 0
    assert depth - n_a == 1

    bf = lambda a: a.astype(BF16)
    row_vec = lambda a: a.reshape(1, -1)
    pair = lambda a: jnp.concatenate([a, a], axis=-1).reshape(1, V7X_LANES)
    cos2, sin2 = _rope_tables(seq)
    n_groups = len(DILATED_GROUPS)

    w_out_bf, w_gate_up_bf, w_down_bf = bf(w_out), bf(w_gate_up), bf(w_down)
    mk, mv = _mem_kv(mem, mem_norm, bf(w_mem_kv), mem_knorm)
    h = x.reshape(bn * seq, dm)
    for l in range(depth):
        gain = row_vec(norm_mix[l])
        if l < n_a:
            qs, lf, k, v, gate, mo = _inproj_a(h, gain, bf(a_w_in[l]), a_lb_logits, pair(mem_qnorm[l]),
                                               mk, mv, l, seq)
            o = _hgrn2(qs, lf, k, v, gate, row_vec(a_onorm[l]), bn, seq)
        else:
            j = l - n_a
            *copies, mo = _inproj_b(h, gain, row_vec(kv_norm), bf(b_w_in[j]), bf(w_kv), b_qnorm[j],
                                    row_vec(b_knorm), cos2, sin2, pair(mem_qnorm[l]), mk, mv, l, bn, seq)
            q_groups, ks, vs = (copies[i * n_groups:(i + 1) * n_groups] for i in range(3))
            o = _dilated_attention(q_groups, ks, vs, bn, seq)
        h = _mix_ffn(h, o, mo, w_out_bf, row_vec(norm_ffn[l]), w_gate_up_bf, w_down_bf, l)
    return h.reshape(bn, seq, dm)
```

```python
import functools
import math

import jax
import jax.numpy as jnp
from jax import lax
from jax.experimental import pallas as pl
from jax.experimental.pallas import tpu as pltpu

F32 = jnp.float32
BF16 = jnp.bfloat16

EPS = 1e-6
HEAD_DIM = 128
CHUNK = 64
MEM_HEAD_DIM = 64
DILATED_GROUPS = ((128, 1), (512, 4), (2048, 16))
ROPE_THETA = 10000.0
LOG2E = math.log2(math.e)

V7X_LANES = 128
V7X_VMEM_SCOPED_DEFAULT_BYTES = 16 * 1024 * 1024
V7X_VMEM_SCOPED_MAX_BYTES = 60000 * 1024
V7X_SINGLE_LOAD_STRIDE = 4

HGRN_IN_TILE = 512
HGRN_IN_SUB = 512
ATTN_IN_TILE = 512
ATTN_IN_SUB = 256
FFN_TILE = 512
HGRN_ROWS = 4096
ATTN_BLOCK = 128
ATTN_UNROLL = 16

_NT = (((1,), (1,)), ((), ()))
_TN = (((0,), (0,)), ((), ()))


def _vmem_limit(pipelined_bytes, resident_bytes, temp_bytes):
    need = 2 * pipelined_bytes + resident_bytes + temp_bytes
    return int(min(max(need, V7X_VMEM_SCOPED_DEFAULT_BYTES), V7X_VMEM_SCOPED_MAX_BYTES))


def _nbytes(shape, dtype):
    n = 1
    for s in shape:
        n *= s
    return n * jnp.dtype(dtype).itemsize


def _resident(shape):
    zeros = (0,) * len(shape)
    return pl.BlockSpec(shape, lambda *_: zeros, pipeline_mode=pl.Buffered(1))


def _dot(a, b):
    return jnp.dot(a, b, preferred_element_type=F32)


def _rms(x, gain):
    ms = jnp.mean(x * x, axis=-1, keepdims=True)
    return x * lax.rsqrt(ms + EPS) * gain


def _silu(x):
    return x * jax.nn.sigmoid(x)


def _head_cols(hd):
    return slice(hd * HEAD_DIM, (hd + 1) * HEAD_DIM)


def _rms_head_pairs(x, gain):
    lo = lax.broadcasted_iota(jnp.int32, x.shape, 1) < MEM_HEAD_DIM
    x2 = x * x
    s_lo = jnp.sum(jnp.where(lo, x2, 0.0), axis=-1, keepdims=True)
    s_hi = jnp.sum(jnp.where(lo, 0.0, x2), axis=-1, keepdims=True)
    ms = jnp.where(lo, s_lo, s_hi) * (1.0 / MEM_HEAD_DIM)
    return x * lax.rsqrt(ms + EPS) * gain


def _rope(x, cos2, sin2):
    return x * cos2 + pltpu.roll(x, HEAD_DIM // 2, axis=1) * sin2


def _store_by_residue(slab_ref, tmp_ref, dil_outs, row0):
    n_heads, tile_rows = slab_ref.shape[0], slab_ref.shape[1]
    base = V7X_SINGLE_LOAD_STRIDE
    two_hops = any(dil > base for dil, _ in dil_outs)
    if two_hops:
        part = tile_rows // base
        for r in range(base):
            for hd in range(n_heads):
                tmp_ref[hd, r * part:(r + 1) * part, :] = slab_ref[hd, pl.ds(r, part, stride=base), :]
    for dil, out_ref in dil_outs:
        rows = tile_rows // dil
        dst = slice(row0 // dil, row0 // dil + rows)
        for r in range(dil):
            for hd in range(n_heads):
                if dil < base or (dil == base and not two_hops):
                    piece = slab_ref[hd, pl.ds(r, rows, stride=dil), :]
                elif dil == base:
                    piece = tmp_ref[hd, r * rows:(r + 1) * rows, :]
                else:
                    assert dil % base == 0 and dil // base <= base
                    piece = tmp_ref[hd, pl.ds((r % base) * part + r // base, rows, stride=dil // base), :]
                out_ref[r, dst, _head_cols(hd)] = piece.astype(out_ref.dtype)


def _memory_probs(mq, qgain, mk_ref):
    scaled_gain = qgain * (MEM_HEAD_DIM ** -0.5)
    probs = []
    for t in range(mq.shape[1] // V7X_LANES):
        cols = slice(t * V7X_LANES, (t + 1) * V7X_LANES)
        qn = _rms_head_pairs(mq[:, cols], scaled_gain)
        lo = lax.broadcasted_iota(jnp.int32, qn.shape, 1) < MEM_HEAD_DIM
        for keep in (lo, jnp.logical_not(lo)):
            qh = jnp.where(keep, qn, 0.0).astype(BF16)
            s = lax.dot_general(qh, mk_ref[:, cols], _NT, preferred_element_type=F32)
            p = jnp.exp(s - jnp.max(s, axis=-1, keepdims=True))
            probs.append((p.astype(BF16), jnp.sum(p, axis=-1, keepdims=True)))
    return probs


def _memory_output(probs, mv_ref, mo_ref, rows=slice(None)):
    for t in range(mo_ref.shape[1] // V7X_LANES):
        cols = slice(t * V7X_LANES, (t + 1) * V7X_LANES)
        outs = [_dot(p, mv_ref[:, cols]) / denom for p, denom in probs[2 * t:2 * t + 2]]
        lo = lax.broadcasted_iota(jnp.int32, outs[0].shape, 1) < MEM_HEAD_DIM
        mo_ref[rows, cols] = jnp.where(lo, outs[0], outs[1]).astype(mo_ref.dtype)


def _mem_kv_kernel(mem_ref, gain_ref, w_ref, kgain_ref, mk_ref, mv_ref):
    mw = mk_ref.shape[1]
    mn = _rms(mem_ref[...], gain_ref[...]).astype(BF16)
    kv = _dot(mn, w_ref[...])
    for t in range(mw // V7X_LANES):
        cols = slice(t * V7X_LANES, (t + 1) * V7X_LANES)
        mk_ref[:, cols] = _rms_head_pairs(kv[:, cols], kgain_ref[...]).astype(mk_ref.dtype)
    mv_ref[...] = kv[:, mw:].astype(mv_ref.dtype)


def _mem_kv(mem, mem_norm, w_mem_kv, mem_knorm):
    bn, mt, dm = mem.shape
    depth = w_mem_kv.shape[0]
    mw = w_mem_kv.shape[2] // 2
    kgain = jnp.concatenate([mem_knorm, mem_knorm], axis=-1).reshape(depth, 1, V7X_LANES)
    out = jax.ShapeDtypeStruct((depth, bn, mt, mw), BF16)
    return pl.pallas_call(
        _mem_kv_kernel,
        out_shape=(out, out),
        grid=(depth, bn),
        in_specs=[
            pl.BlockSpec((None, mt, dm), lambda l, b: (b, 0, 0)),
            pl.BlockSpec((None, 1, dm), lambda l, b: (l, 0, 0)),
            pl.BlockSpec((None, dm, 2 * mw), lambda l, b: (l, 0, 0)),
            pl.BlockSpec((None, 1, V7X_LANES), lambda l, b: (l, 0, 0)),
        ],
        out_specs=(
            pl.BlockSpec((None, None, mt, mw), lambda l, b: (l, b, 0, 0)),
            pl.BlockSpec((None, None, mt, mw), lambda l, b: (l, b, 0, 0)),
        ),
        name="mem_kv",
    )(mem, mem_norm.reshape(depth, 1, dm), w_mem_kv, kgain)


def _inproj_a_kernel(x_ref, gain_ref, w_ref, lbl_ref, mqg_ref, mk_ref, mv_ref,
                     qs_ref, lf_ref, k_ref, v_ref, gate_ref, mo_ref, *, layer):
    aw = qs_ref.shape[1]
    lg = lbl_ref[...]
    e = jnp.exp(lg - jnp.max(lg, axis=0, keepdims=True))
    lb = jnp.sum(e[:layer + 1], axis=0, keepdims=True) / jnp.sum(e, axis=0, keepdims=True)
    for row0 in range(0, x_ref.shape[0], HGRN_IN_SUB):
        rows = slice(row0, row0 + HGRN_IN_SUB)
        xn = _rms(x_ref[rows, :], gain_ref[...]).astype(BF16)
        probs = _memory_probs(_dot(xn, w_ref[:, 4 * aw:]), mqg_ref[...], mk_ref)
        qs_ref[rows, :] = _silu(_dot(xn, w_ref[:, 0:aw])).astype(qs_ref.dtype)
        _memory_output(probs, mv_ref, mo_ref, rows)
        f = lb + (1.0 - lb) * jax.nn.sigmoid(_dot(xn, w_ref[:, aw:2 * aw]))
        lf_ref[rows, :] = jnp.log(f) * LOG2E
        k_ref[rows, :] = (1.0 - f).astype(k_ref.dtype)
        v_ref[rows, :] = _dot(xn, w_ref[:, 2 * aw:3 * aw]).astype(v_ref.dtype)
        gate_ref[rows, :] = _silu(_dot(xn, w_ref[:, 3 * aw:4 * aw])).astype(gate_ref.dtype)


def _inproj_a(h, gain, w, lb_logits, mq_gain, mk, mv, layer, seq):
    t, dm = h.shape
    mt, mw = mk.shape[2], mk.shape[3]
    aw = (w.shape[1] - mw) // 4
    tile = HGRN_IN_TILE
    tiles_per_seq = seq // tile
    row = lambda i: (i, 0)
    mem = lambda i: (layer, i // tiles_per_seq, 0, 0)
    wide = functools.partial(jax.ShapeDtypeStruct, (t, aw))
    pipelined = (_nbytes((tile, dm), F32) + _nbytes((tile, aw), F32)
                 + 4 * _nbytes((tile, aw), BF16) + _nbytes((tile, mw), BF16)
                 + 2 * _nbytes((mt, mw), BF16))
    return pl.pallas_call(
        functools.partial(_inproj_a_kernel, layer=layer),
        out_shape=(wide(BF16), wide(F32), wide(BF16), wide(BF16), wide(BF16),
                   jax.ShapeDtypeStruct((t, mw), BF16)),
        grid=(t // tile,),
        in_specs=[
            pl.BlockSpec((tile, dm), row),
            _resident((1, dm)),
            _resident(w.shape),
            _resident(lb_logits.shape),
            _resident((1, V7X_LANES)),
            pl.BlockSpec((None, None, mt, mw), mem),
            pl.BlockSpec((None, None, mt, mw), mem),
        ],
        out_specs=tuple([pl.BlockSpec((tile, aw), row)] * 5 + [pl.BlockSpec((tile, mw), row)]),
        compiler_params=pltpu.CompilerParams(
            dimension_semantics=("parallel",),
            vmem_limit_bytes=_vmem_limit(pipelined, _nbytes(w.shape, BF16),
                                         4 * _nbytes((HGRN_IN_SUB, aw), F32))),
        name="inproj_a",
    )(h, gain, w, lb_logits, mq_gain, mk, mv)


def _hgrn2_kernel(on_ref, qs_ref, lf_ref, k_ref, v_ref, gate_ref, o_ref, state_ref):
    @pl.when(pl.program_id(2) == 0)
    def _():
        state_ref[...] = jnp.zeros_like(state_ref)

    row = lax.broadcasted_iota(jnp.int32, (CHUNK, HEAD_DIM), 0)
    causal = (lax.broadcasted_iota(jnp.int32, (CHUNK, CHUNK), 0)
              >= lax.broadcasted_iota(jnp.int32, (CHUNK, CHUNK), 1))
    onorm = on_ref[...]

    chunks = [slice(c * CHUNK, (c + 1) * CHUNK) for c in range(qs_ref.shape[0] // CHUNK)]
    q_ins, vs, decays, atts, kvs = [], [], [], [], []
    for rows in chunks:
        b = lf_ref[rows, :]
        shift = 1
        while shift < CHUNK:
            b = b + jnp.where(row >= shift, pltpu.roll(b, shift, axis=0), 0.0)
            shift *= 2
        b_end = b[CHUNK - 1:CHUNK, :]
        k = k_ref[rows, :].astype(F32)
        q_in = (qs_ref[rows, :].astype(F32) * jnp.exp2(b)).astype(BF16)
        k_in = (k * jnp.exp2(-b)).astype(BF16)
        k_out = (k * jnp.exp2(b_end - b)).astype(BF16)
        v = v_ref[rows, :]
        q_ins.append(q_in)
        vs.append(v)
        decays.append(jnp.exp2(b_end))
        atts.append(lax.dot_general(q_in, k_in, _NT, preferred_element_type=F32))
        kvs.append(lax.dot_general(v, k_out, _TN, preferred_element_type=F32))

    state_t = state_ref[...]
    states = []
    for decay, kv in zip(decays, kvs):
        states.append(state_t.astype(BF16))
        state_t = state_t * decay + kv
    state_ref[...] = state_t

    for rows, q_in, v, att, state_in in zip(chunks, q_ins, vs, atts, states):
        att = jnp.where(causal, att, 0.0).astype(BF16)
        o = _dot(att, v) + lax.dot_general(q_in, state_in, _NT, preferred_element_type=F32)
        o_ref[rows, :] = (_rms(o, onorm) * gate_ref[rows, :].astype(F32)).astype(o_ref.dtype)


def _hgrn2(qs, lf, k, v, gate, onorm, bn, seq):
    t, aw = qs.shape
    heads = aw // HEAD_DIM
    step_rows = min(HGRN_ROWS, seq)
    steps = seq // step_rows
    spec = pl.BlockSpec((step_rows, HEAD_DIM), lambda b, h, s: (b * steps + s, h))
    return pl.pallas_call(
        _hgrn2_kernel,
        out_shape=jax.ShapeDtypeStruct((t, aw), BF16),
        grid=(bn, heads, steps),
        in_specs=[pl.BlockSpec((1, HEAD_DIM), lambda b, h, s: (0, h)), spec, spec, spec, spec, spec],
        out_specs=spec,
        scratch_shapes=[pltpu.VMEM((HEAD_DIM, HEAD_DIM), F32)],
        compiler_params=pltpu.CompilerParams(
            dimension_semantics=("parallel", "parallel", "arbitrary")),
        name="hgrn2",
    )(onorm, qs, lf, k, v, gate)


def _mix_ffn_kernel(h_ref, o_ref, mo_ref, wo_ref, nf_ref, wgu_ref, wd_ref, h_out):
    main_w = o_ref.shape[1]
    hidden = wd_ref.shape[0]
    h = h_ref[...] + _dot(o_ref[...], wo_ref[0:main_w, :]) + _dot(mo_ref[...], wo_ref[main_w:, :])
    hn = _rms(h, nf_ref[...]).astype(BF16)
    act = (_silu(_dot(hn, wgu_ref[:, 0:hidden])) * _dot(hn, wgu_ref[:, hidden:])).astype(BF16)
    h_out[...] = h + _dot(act, wd_ref[...])


def _layer_resident(stacked, layer):
    zeros = (0,) * (stacked.ndim - 1)
    return pl.BlockSpec((None,) + stacked.shape[1:], lambda *_: (layer,) + zeros,
                        pipeline_mode=pl.Buffered(1))


def _mix_ffn(h, o, mo, w_out, norm_ffn, w_gate_up, w_down, layer):
    t, dm = h.shape
    main_w, mw = o.shape[1], mo.shape[1]
    row = lambda i: (i, 0)
    resident = sum(_nbytes(w.shape[1:], BF16) for w in (w_out, w_gate_up, w_down))
    tile = FFN_TILE
    pipelined = 2 * _nbytes((tile, dm), F32) + _nbytes((tile, main_w + mw), BF16)
    temps = 3 * _nbytes((tile, dm), F32) + 3 * _nbytes((tile, w_down.shape[1]), F32)
    return pl.pallas_call(
        _mix_ffn_kernel,
        out_shape=jax.ShapeDtypeStruct((t, dm), F32),
        grid=(t // tile,),
        in_specs=[
            pl.BlockSpec((tile, dm), row), pl.BlockSpec((tile, main_w), row),
            pl.BlockSpec((tile, mw), row),
            _layer_resident(w_out, layer), _resident((1, dm)), _layer_resident(w_gate_up, layer),
            _layer_resident(w_down, layer),
        ],
        out_specs=pl.BlockSpec((tile, dm), row),
        compiler_params=pltpu.CompilerParams(
            dimension_semantics=("parallel",),
            vmem_limit_bytes=_vmem_limit(pipelined, resident, temps)),
        name="mix_ffn",
    )(h, o, mo, w_out, norm_ffn, w_gate_up, w_down)


def _inproj_b_kernel(*refs, dilations):
    n_groups = len(dilations)
    (x_ref, gain_ref, kvg_ref, w_ref, wkv_ref, qn_ref, kn_ref, cos_ref, sin_ref,
     mqg_ref, mk_ref, mv_ref) = refs[:12]
    q_refs = refs[12:12 + n_groups]
    k_refs = refs[12 + n_groups:12 + 2 * n_groups]
    v_refs = refs[12 + 2 * n_groups:12 + 3 * n_groups]
    mo_ref = refs[12 + 3 * n_groups]
    scratch = refs[13 + 3 * n_groups:]
    q_slabs = dict(zip([d for d in dilations if d != 1], scratch[:-5]))
    k_slab, v_slab, q_tmp, k_tmp, v_tmp = scratch[-5:]
    kv_w = wkv_ref.shape[1] // 2
    bw = (w_ref.shape[1] - mo_ref.shape[1]) // n_groups
    heads = range(bw // HEAD_DIM)
    pair = 2
    q_scale = (HEAD_DIM ** -0.5) * LOG2E

    for row0 in range(0, x_ref.shape[0], ATTN_IN_SUB):
        rows = slice(row0, row0 + ATTN_IN_SUB)
        x = x_ref[rows, :]
        xhat = x * lax.rsqrt(jnp.mean(x * x, axis=-1, keepdims=True) + EPS)
        xn = (xhat * gain_ref[...]).astype(BF16)
        kn = (xhat * kvg_ref[...]).astype(BF16)
        cos, sin = cos_ref[rows, :], sin_ref[rows, :]
        cos_s, sin_s = cos * q_scale, sin * q_scale

        def queries(gi, row0=row0, rows=rows, xn=xn, cos_s=cos_s, sin_s=sin_s):
            dil, q_ref = dilations[gi], q_refs[gi]
            for hd in heads:
                if hd % pair == 0:
                    col0 = gi * bw + hd * HEAD_DIM
                    qs = _dot(xn, w_ref[:, col0:col0 + pair * HEAD_DIM])
                q = _rope(_rms(qs[:, _head_cols(hd % pair)], qn_ref[gi]), cos_s, sin_s)
                if dil == 1:
                    q_ref[rows, _head_cols(hd)] = q.astype(q_ref.dtype)
                else:
                    q_slabs[dil][hd] = q
            if dil != 1:
                _store_by_residue(q_slabs[dil], q_tmp, [(dil, q_ref)], row0)

        def copies(slab, tmp, out_refs, row0=row0, rows=rows):
            for dil, out_ref in zip(dilations, out_refs):
                if dil == 1:
                    for hd in heads:
                        out_ref[rows, _head_cols(hd)] = slab[hd].astype(out_ref.dtype)
            _store_by_residue(slab, tmp, [(d, ref) for d, ref in zip(dilations, out_refs) if d != 1], row0)

        order = sorted(range(n_groups), key=lambda gi: -dilations[gi])
        queries(order[0])
        probs = _memory_probs(_dot(xn, w_ref[:, n_groups * bw:]), mqg_ref[...], mk_ref)
        for hd in heads:
            if hd % pair == 0:
                k = _dot(kn, wkv_ref[:, hd * HEAD_DIM:(hd + pair) * HEAD_DIM])
            k_slab[hd] = _rope(_rms(k[:, _head_cols(hd % pair)], kn_ref[...]), cos, sin)
        copies(k_slab, k_tmp, k_refs)
        for gi in order[1:-1]:
            queries(gi)
        _memory_output(probs, mv_ref, mo_ref, rows)
        for hd in heads:
            if hd % pair == 0:
                v = _dot(kn, wkv_ref[:, kv_w + hd * HEAD_DIM:kv_w + (hd + pair) * HEAD_DIM])
            v_slab[hd] = v[:, _head_cols(hd % pair)]
        copies(v_slab, v_tmp, v_refs)
        queries(order[-1])


def _residue_out(bn, seq, width, dil):
    tiles_per_seq = seq // ATTN_IN_TILE
    if dil == 1:
        return (jax.ShapeDtypeStruct((bn * seq, width), BF16),
                pl.BlockSpec((ATTN_IN_TILE, width), lambda i: (i, 0)))
    return (jax.ShapeDtypeStruct((bn, dil, seq // dil, width), BF16),
            pl.BlockSpec((None, dil, ATTN_IN_TILE // dil, width),
                         lambda i: (i // tiles_per_seq, 0, i % tiles_per_seq, 0)))


def _inproj_b(h, gain, kv_gain, w, w_kv, q_norm, k_norm, cos2, sin2, mq_gain, mk, mv, layer, bn, seq):
    t, dm = h.shape
    mt, mw = mk.shape[2], mk.shape[3]
    dilations = tuple(d for _, d in DILATED_GROUPS)
    n_groups = len(dilations)
    bw = (w.shape[1] - mw) // n_groups
    kv_w = w_kv.shape[1] // 2
    assert kv_w == bw
    tile = ATTN_IN_TILE
    tiles_per_seq = seq // tile
    row = lambda i: (i, 0)
    pos = lambda i: (i % tiles_per_seq, 0)
    mem = lambda i: (layer, i // tiles_per_seq, 0, 0)
    copies = [_residue_out(bn, seq, bw, dil) for dil in dilations] * 3
    slab = (bw // HEAD_DIM, ATTN_IN_SUB, HEAD_DIM)
    n_slabs = sum(1 for dil in dilations if dil != 1) + 5
    pipelined = (_nbytes((tile, dm), F32) + 3 * n_groups * _nbytes((tile, bw), BF16)
                 + 2 * _nbytes((tile, HEAD_DIM), F32) + _nbytes((tile, mw), BF16)
                 + 2 * _nbytes((mt, mw), BF16))
    resident = _nbytes(w.shape, BF16) + _nbytes(w_kv.shape, BF16) + n_slabs * _nbytes(slab, F32)
    return pl.pallas_call(
        functools.partial(_inproj_b_kernel, dilations=dilations),
        out_shape=tuple([shape for shape, _ in copies] + [jax.ShapeDtypeStruct((t, mw), BF16)]),
        grid=(t // tile,),
        in_specs=[
            pl.BlockSpec((tile, dm), row),
            _resident((1, dm)), _resident((1, dm)),
            _resident(w.shape), _resident(w_kv.shape),
            _resident((n_groups, 1, HEAD_DIM)), _resident((1, HEAD_DIM)),
            pl.BlockSpec((tile, HEAD_DIM), pos), pl.BlockSpec((tile, HEAD_DIM), pos),
            _resident((1, V7X_LANES)),
            pl.BlockSpec((None, None, mt, mw), mem),
            pl.BlockSpec((None, None, mt, mw), mem),
        ],
        out_specs=tuple([spec for _, spec in copies] + [pl.BlockSpec((tile, mw), row)]),
        scratch_shapes=[pltpu.VMEM(slab, F32)] * n_slabs,
        compiler_params=pltpu.CompilerParams(
            dimension_semantics=("parallel",),
            vmem_limit_bytes=_vmem_limit(pipelined, resident, 6 * _nbytes((ATTN_IN_SUB, bw), F32))),
        name="inproj_b",
    )(h, gain, kv_gain, w, w_kv, q_norm.reshape(n_groups, 1, HEAD_DIM), k_norm, cos2, sin2,
      mq_gain, mk, mv)


def _dilated_kernel(*refs, dilations):
    n_groups = len(dilations)
    q_refs = refs[:n_groups]
    k_refs = refs[n_groups:2 * n_groups]
    v_refs = refs[2 * n_groups:3 * n_groups]
    o_ref, og_ref, lse_ref, bias_ref = refs[3 * n_groups:]
    seq = o_ref.shape[0]
    blk = ATTN_BLOCK

    qi = lax.broadcasted_iota(jnp.int32, (blk, 2 * blk), 0)
    kj = lax.broadcasted_iota(jnp.int32, (blk, 2 * blk), 1)
    band = (kj >= qi) & (kj <= qi + blk)
    bias_ref[0] = jnp.where(band & (kj >= blk), 0.0, -jnp.inf)
    bias_ref[1] = jnp.where(band, 0.0, -jnp.inf)
    bias_ref[2] = jnp.where(kj <= qi, 0.0, -jnp.inf)

    parked = [gi for gi, dil in enumerate(dilations) if dil != 1]
    (last,) = [gi for gi, dil in enumerate(dilations) if dil == 1]
    for gi in parked + [last]:
        dil = dilations[gi]
        q_ref, k_ref, v_ref = q_refs[gi], k_refs[gi], v_refs[gi]
        n_blocks = seq // (blk * dil)

        def blocks(step, carry, q_ref=q_ref, k_ref=k_ref, v_ref=v_ref, gi=gi, dil=dil, n_blocks=n_blocks):
            idxs = [step * ATTN_UNROLL + u for u in range(ATTN_UNROLL)]
            starts = [pl.multiple_of(idx * blk, blk) for idx in idxs]
            windows = [pl.ds(pl.multiple_of(jnp.maximum(start - blk, 0), blk), 2 * blk) for start in starts]
            scores = [lax.dot_general(q_ref[pl.ds(start, blk), :], k_ref[window, :], _NT,
                                      preferred_element_type=F32)
                      for start, window in zip(starts, windows)]
            soft = []
            for idx, s in zip(idxs, scores):
                n = lax.rem(idx, n_blocks)
                s = s + bias_ref[jnp.where(idx == 0, 2, jnp.minimum(n, 1))]
                m = jnp.max(s, axis=-1, keepdims=True)
                p = jnp.exp2(s - m)
                denom = jnp.sum(p, axis=-1, keepdims=True)
                soft.append((p.astype(BF16), m, denom))
            for idx, start, window, (p, m, denom) in zip(idxs, starts, windows, soft):
                o = _dot(p, v_ref[window, :]) / denom
                lse2 = jnp.broadcast_to(m + jnp.log(denom) * LOG2E, (blk, HEAD_DIM))
                if dil != 1:
                    n = lax.rem(idx, n_blocks)
                    out_rows = pl.ds(n * (blk * dil) + lax.div(idx, n_blocks), blk, stride=dil)
                    slot = parked.index(gi)
                    og_ref[slot, out_rows, :] = o
                    lse_ref[slot, out_rows, :] = lse2
                else:
                    rows = pl.ds(start, blk)
                    lses = [lse2] + [lse_ref[slot, rows, :] for slot in range(len(parked))]
                    outs = [o] + [og_ref[slot, rows, :] for slot in range(len(parked))]
                    top = functools.reduce(jnp.maximum, lses)
                    ws = [jnp.exp2(l - top) for l in lses]
                    acc = sum(w * og for w, og in zip(ws, outs))
                    o_ref[rows, :] = (acc / sum(ws)).astype(o_ref.dtype)
            return carry

        lax.fori_loop(0, dil * n_blocks // ATTN_UNROLL, blocks, 0)


def _dilated_attention(qs, ks, vs, bn, seq):
    width = qs[0].shape[-1]
    heads = width // HEAD_DIM
    dilations = tuple(d for _, d in DILATED_GROUPS)
    n_groups = len(dilations)
    as_seq = lambda a: a.reshape(bn, seq, width)
    spec = pl.BlockSpec((None, seq, HEAD_DIM), lambda b, h: (b, 0, h))
    seq_bf16 = _nbytes((seq, HEAD_DIM), BF16)
    seq_f32 = _nbytes((seq, HEAD_DIM), F32)
    n_bias = 3
    scratch = 2 * (n_groups - 1) * seq_f32 + _nbytes((n_bias, ATTN_BLOCK, 2 * ATTN_BLOCK), F32)
    staged = ATTN_UNROLL * (_nbytes((ATTN_BLOCK, 2 * ATTN_BLOCK), F32) + _nbytes((ATTN_BLOCK, 2 * ATTN_BLOCK), BF16)
                            + 2 * _nbytes((ATTN_BLOCK, HEAD_DIM), F32))
    out = pl.pallas_call(
        functools.partial(_dilated_kernel, dilations=dilations),
        out_shape=jax.ShapeDtypeStruct((bn, seq, width), BF16),
        grid=(bn, heads),
        in_specs=[spec] * (3 * n_groups),
        out_specs=spec,
        scratch_shapes=[
            pltpu.VMEM((n_groups - 1, seq, HEAD_DIM), F32),
            pltpu.VMEM((n_groups - 1, seq, HEAD_DIM), F32),
            pltpu.VMEM((n_bias, ATTN_BLOCK, 2 * ATTN_BLOCK), F32),
        ],
        compiler_params=pltpu.CompilerParams(
            dimension_semantics=("parallel", "parallel"),
            vmem_limit_bytes=_vmem_limit((3 * n_groups + 1) * seq_bf16, scratch, staged)),
        name="dilated_attention",
    )(*[as_seq(a) for a in (*qs, *ks, *vs)])
    return out.reshape(bn * seq, width)


def _rope_tables(seq):
    half = HEAD_DIM // 2
    inv = ROPE_THETA ** (-jnp.arange(half, dtype=F32) / half)
    ang = jnp.arange(seq).astype(F32)[:, None] * inv[None, :]
    cos, sin = jnp.cos(ang), jnp.sin(ang)
    return jnp.concatenate([cos, cos], axis=-1), jnp.concatenate([-sin, sin], axis=-1)


def kernel(x, mem, norm_mix, norm_ffn, a_w_in, a_lb_logits, a_onorm, b_w_in, b_qnorm, kv_norm, w_kv,
           b_knorm, mem_norm, w_mem_kv, mem_qnorm, mem_knorm, w_out, w_gate_up, w_down):
    bn, seq, dm = x.shape
    depth = norm_mix.shape[0]
    n_a = a_w_in.shape[0]
    max_dil = max(d for _, d in DILATED_GROUPS)
    assert all(w == ATTN_BLOCK * d for w, d in DILATED_GROUPS)
    assert seq % (ATTN_BLOCK * max_dil) == 0 and seq % (ATTN_BLOCK * ATTN_UNROLL) == 0
    assert seq % min(HGRN_ROWS, seq) == 0 and HGRN_ROWS % CHUNK == 0 and HGRN_IN_SUB % CHUNK == 0
    assert seq % HGRN_IN_TILE == 0 and HGRN_IN_TILE % HGRN_IN_SUB == 0 and seq % FFN_TILE == 0
    assert seq % ATTN_IN_TILE == 0 and ATTN_IN_TILE % ATTN_IN_SUB == 0 and ATTN_IN_SUB % max_dil == 0
    assert depth - n_a == 1

    bf = lambda a: a.astype(BF16)
    row_vec = lambda a: a.reshape(1, -1)
    pair = lambda a: jnp.concatenate([a, a], axis=-1).reshape(1, V7X_LANES)
    cos2, sin2 = _rope_tables(seq)
    n_groups = len(DILATED_GROUPS)

    w_out_bf, w_gate_up_bf, w_down_bf = bf(w_out), bf(w_gate_up), bf(w_down)
    mk, mv = _mem_kv(mem, mem_norm, bf(w_mem_kv), mem_knorm)
    h = x.reshape(bn * seq, dm)
    for l in range(depth):
        gain = row_vec(norm_mix[l])
        if l < n_a:
            qs, lf, k, v, gate, mo = _inproj_a(h, gain, bf(a_w_in[l]), a_lb_logits, pair(mem_qnorm[l]),
                                               mk, mv, l, seq)
            o = _hgrn2(qs, lf, k, v, gate, row_vec(a_onorm[l]), bn, seq)
        else:
            j = l - n_a
            *copies, mo = _inproj_b(h, gain, row_vec(kv_norm), bf(b_w_in[j]), bf(w_kv), b_qnorm[j],
                                    row_vec(b_knorm), cos2, sin2, pair(mem_qnorm[l]), mk, mv, l, bn, seq)
            q_groups, ks, vs = (copies[i * n_groups:(i + 1) * n_groups] for i in range(3))
            o = _dilated_attention(q_groups, ks, vs, bn, seq)
        h = _mix_ffn(h, o, mo, w_out_bf, row_vec(norm_ffn[l]), w_gate_up_bf, w_down_bf, l)
    return h.reshape(bn, seq, dm)
```

```python
import functools
import math

import jax
import jax.numpy as jnp
from jax import lax
from jax.experimental import pallas as pl
from jax.experimental.pallas import tpu as pltpu

F32 = jnp.float32
BF16 = jnp.bfloat16

EPS = 1e-6
HEAD_DIM = 128
CHUNK = 64
MEM_HEAD_DIM = 64
DILATED_GROUPS = ((128, 1), (512, 4), (2048, 16))
ROPE_THETA = 10000.0
LOG2E = math.log2(math.e)

V7X_LANES = 128
V7X_VMEM_SCOPED_DEFAULT_BYTES = 16 * 1024 * 1024
V7X_VMEM_SCOPED_MAX_BYTES = 60000 * 1024
V7X_SINGLE_LOAD_STRIDE = 4

HGRN_IN_TILE = 512
HGRN_IN_SUB = 512
ATTN_IN_TILE = 512
ATTN_IN_SUB = 256
FFN_TILE = 512
HGRN_ROWS = 4096
ATTN_BLOCK = 128
ATTN_UNROLL = 16

_NT = (((1,), (1,)), ((), ()))
_TN = (((0,), (0,)), ((), ()))


def _vmem_limit(pipelined_bytes, resident_bytes, temp_bytes):
    need = 2 * pipelined_bytes + resident_bytes + temp_bytes
    return int(min(max(need, V7X_VMEM_SCOPED_DEFAULT_BYTES), V7X_VMEM_SCOPED_MAX_BYTES))


def _nbytes(shape, dtype):
    n = 1
    for s in shape:
        n *= s
    return n * jnp.dtype(dtype).itemsize


def _resident(shape):
    zeros = (0,) * len(shape)
    return pl.BlockSpec(shape, lambda *_: zeros, pipeline_mode=pl.Buffered(1))


def _dot(a, b):
    return jnp.dot(a, b, preferred_element_type=F32)


def _rms(x, gain):
    ms = jnp.mean(x * x, axis=-1, keepdims=True)
    return x * lax.rsqrt(ms + EPS) * gain


def _silu(x):
    return x * jax.nn.sigmoid(x)


def _head_cols(hd):
    return slice(hd * HEAD_DIM, (hd + 1) * HEAD_DIM)


def _rms_head_pairs(x, gain):
    lo = lax.broadcasted_iota(jnp.int32, x.shape, 1) < MEM_HEAD_DIM
    x2 = x * x
    s_lo = jnp.sum(jnp.where(lo, x2, 0.0), axis=-1, keepdims=True)
    s_hi = jnp.sum(jnp.where(lo, 0.0, x2), axis=-1, keepdims=True)
    ms = jnp.where(lo, s_lo, s_hi) * (1.0 / MEM_HEAD_DIM)
    return x * lax.rsqrt(ms + EPS) * gain


def _rope(x, cos2, sin2):
    return x * cos2 + pltpu.roll(x, HEAD_DIM // 2, axis=1) * sin2


def _store_by_residue(slab_ref, tmp_ref, dil_outs, row0):
    n_heads, tile_rows = slab_ref.shape[0], slab_ref.shape[1]
    base = V7X_SINGLE_LOAD_STRIDE
    two_hops = any(dil > base for dil, _ in dil_outs)
    if two_hops:
        part = tile_rows // base
        for r in range(base):
            for hd in range(n_heads):
                tmp_ref[hd, r * part:(r + 1) * part, :] = slab_ref[hd, pl.ds(r, part, stride=base), :]
    for dil, out_ref in dil_outs:
        rows = tile_rows // dil
        dst = slice(row0 // dil, row0 // dil + rows)
        for r in range(dil):
            for hd in range(n_heads):
                if dil < base or (dil == base and not two_hops):
                    piece = slab_ref[hd, pl.ds(r, rows, stride=dil), :]
                elif dil == base:
                    piece = tmp_ref[hd, r * rows:(r + 1) * rows, :]
                else:
                    assert dil % base == 0 and dil // base <= base
                    piece = tmp_ref[hd, pl.ds((r % base) * part + r // base, rows, stride=dil // base), :]
                out_ref[r, dst, _head_cols(hd)] = piece.astype(out_ref.dtype)


def _memory_probs(mq, qgain, mk_ref):
    scaled_gain = qgain * (MEM_HEAD_DIM ** -0.5)
    probs = []
    for t in range(mq.shape[1] // V7X_LANES):
        cols = slice(t * V7X_LANES, (t + 1) * V7X_LANES)
        qn = _rms_head_pairs(mq[:, cols], scaled_gain)
        lo = lax.broadcasted_iota(jnp.int32, qn.shape, 1) < MEM_HEAD_DIM
        for keep in (lo, jnp.logical_not(lo)):
            qh = jnp.where(keep, qn, 0.0).astype(BF16)
            s = lax.dot_general(qh, mk_ref[:, cols], _NT, preferred_element_type=F32)
            p = jnp.exp(s - jnp.max(s, axis=-1, keepdims=True))
            probs.append((p.astype(BF16), jnp.sum(p, axis=-1, keepdims=True)))
    return probs


def _memory_output(probs, mv_ref, mo_ref, rows=slice(None)):
    for t in range(mo_ref.shape[1] // V7X_LANES):
        cols = slice(t * V7X_LANES, (t + 1) * V7X_LANES)
        outs = [_dot(p, mv_ref[:, cols]) / denom for p, denom in probs[2 * t:2 * t + 2]]
        lo = lax.broadcasted_iota(jnp.int32, outs[0].shape, 1) < MEM_HEAD_DIM
        mo_ref[rows, cols] = jnp.where(lo, outs[0], outs[1]).astype(mo_ref.dtype)


def _mem_kv_kernel(mem_ref, gain_ref, w_ref, kgain_ref, mk_ref, mv_ref):
    mw = mk_ref.shape[1]
    mn = _rms(mem_ref[...], gain_ref[...]).astype(BF16)
    kv = _dot(mn, w_ref[...])
    for t in range(mw // V7X_LANES):
        cols = slice(t * V7X_LANES, (t + 1) * V7X_LANES)
        mk_ref[:, cols] = _rms_head_pairs(kv[:, cols], kgain_ref[...]).astype(mk_ref.dtype)
    mv_ref[...] = kv[:, mw:].astype(mv_ref.dtype)


def _mem_kv(mem, mem_norm, w_mem_kv, mem_knorm):
    bn, mt, dm = mem.shape
    depth = w_mem_kv.shape[0]
    mw = w_mem_kv.shape[2] // 2
    kgain = jnp.concatenate([mem_knorm, mem_knorm], axis=-1).reshape(depth, 1, V7X_LANES)
    out = jax.ShapeDtypeStruct((depth, bn, mt, mw), BF16)
    return pl.pallas_call(
        _mem_kv_kernel,
        out_shape=(out, out),
        grid=(depth, bn),
        in_specs=[
            pl.BlockSpec((None, mt, dm), lambda l, b: (b, 0, 0)),
            pl.BlockSpec((None, 1, dm), lambda l, b: (l, 0, 0)),
            pl.BlockSpec((None, dm, 2 * mw), lambda l, b: (l, 0, 0)),
            pl.BlockSpec((None, 1, V7X_LANES), lambda l, b: (l, 0, 0)),
        ],
        out_specs=(
            pl.BlockSpec((None, None, mt, mw), lambda l, b: (l, b, 0, 0)),
            pl.BlockSpec((None, None, mt, mw), lambda l, b: (l, b, 0, 0)),
        ),
        name="mem_kv",
    )(mem, mem_norm.reshape(depth, 1, dm), w_mem_kv, kgain)


def _inproj_a_kernel(x_ref, gain_ref, w_ref, lbl_ref, mqg_ref, mk_ref, mv_ref,
                     qs_ref, lf_ref, k_ref, v_ref, gate_ref, mo_ref, *, layer):
    aw = qs_ref.shape[1]
    lg = lbl_ref[...]
    e = jnp.exp(lg - jnp.max(lg, axis=0, keepdims=True))
    lb = jnp.sum(e[:layer + 1], axis=0, keepdims=True) / jnp.sum(e, axis=0, keepdims=True)
    for row0 in range(0, x_ref.shape[0], HGRN_IN_SUB):
        rows = slice(row0, row0 + HGRN_IN_SUB)
        xn = _rms(x_ref[rows, :], gain_ref[...]).astype(BF16)
        probs = _memory_probs(_dot(xn, w_ref[:, 4 * aw:]), mqg_ref[...], mk_ref)
        qs_ref[rows, :] = _silu(_dot(xn, w_ref[:, 0:aw])).astype(qs_ref.dtype)
        _memory_output(probs, mv_ref, mo_ref, rows)
        f = lb + (1.0 - lb) * jax.nn.sigmoid(_dot(xn, w_ref[:, aw:2 * aw]))
        lf_ref[rows, :] = jnp.log(f) * LOG2E
        k_ref[rows, :] = (1.0 - f).astype(k_ref.dtype)
        v_ref[rows, :] = _dot(xn, w_ref[:, 2 * aw:3 * aw]).astype(v_ref.dtype)
        gate_ref[rows, :] = _silu(_dot(xn, w_ref[:, 3 * aw:4 * aw])).astype(gate_ref.dtype)


def _inproj_a(h, gain, w, lb_logits, mq_gain, mk, mv, layer, seq):
    t, dm = h.shape
    mt, mw = mk.shape[2], mk.shape[3]
    aw = (w.shape[1] - mw) // 4
    tile = HGRN_IN_TILE
    tiles_per_seq = seq // tile
    row = lambda i: (i, 0)
    mem = lambda i: (layer, i // tiles_per_seq, 0, 0)
    wide = functools.partial(jax.ShapeDtypeStruct, (t, aw))
    pipelined = (_nbytes((tile, dm), F32) + _nbytes((tile, aw), F32)
                 + 4 * _nbytes((tile, aw), BF16) + _nbytes((tile, mw), BF16)
                 + 2 * _nbytes((mt, mw), BF16))
    return pl.pallas_call(
        functools.partial(_inproj_a_kernel, layer=layer),
        out_shape=(wide(BF16), wide(F32), wide(BF16), wide(BF16), wide(BF16),
                   jax.ShapeDtypeStruct((t, mw), BF16)),
        grid=(t // tile,),
        in_specs=[
            pl.BlockSpec((tile, dm), row),
            _resident((1, dm)),
            _resident(w.shape),
            _resident(lb_logits.shape),
            _resident((1, V7X_LANES)),
            pl.BlockSpec((None, None, mt, mw), mem),
            pl.BlockSpec((None, None, mt, mw), mem),
        ],
        out_specs=tuple([pl.BlockSpec((tile, aw), row)] * 5 + [pl.BlockSpec((tile, mw), row)]),
        compiler_params=pltpu.CompilerParams(
            dimension_semantics=("parallel",),
            vmem_limit_bytes=_vmem_limit(pipelined, _nbytes(w.shape, BF16),
                                         4 * _nbytes((HGRN_IN_SUB, aw), F32))),
        name="inproj_a",
    )(h, gain, w, lb_logits, mq_gain, mk, mv)


def _hgrn2_kernel(on_ref, qs_ref, lf_ref, k_ref, v_ref, gate_ref, o_ref, state_ref):
    @pl.when(pl.program_id(2) == 0)
    def _():
        state_ref[...] = jnp.zeros_like(state_ref)

    row = lax.broadcasted_iota(jnp.int32, (CHUNK, HEAD_DIM), 0)
    causal = (lax.broadcasted_iota(jnp.int32, (CHUNK, CHUNK), 0)
              >= lax.broadcasted_iota(jnp.int32, (CHUNK, CHUNK), 1))
    onorm = on_ref[...]

    chunks = [slice(c * CHUNK, (c + 1) * CHUNK) for c in range(qs_ref.shape[0] // CHUNK)]
    q_ins, vs, decays, atts, kvs = [], [], [], [], []
    for rows in chunks:
        b = lf_ref[rows, :]
        shift = 1
        while shift < CHUNK:
            b = b + jnp.where(row >= shift, pltpu.roll(b, shift, axis=0), 0.0)
            shift *= 2
        b_end = b[CHUNK - 1:CHUNK, :]
        k = k_ref[rows, :].astype(F32)
        q_in = (qs_ref[rows, :].astype(F32) * jnp.exp2(b)).astype(BF16)
        k_in = (k * jnp.exp2(-b)).astype(BF16)
        k_out = (k * jnp.exp2(b_end - b)).astype(BF16)
        v = v_ref[rows, :]
        q_ins.append(q_in)
        vs.append(v)
        decays.append(jnp.exp2(b_end))
        atts.append(lax.dot_general(q_in, k_in, _NT, preferred_element_type=F32))
        kvs.append(lax.dot_general(v, k_out, _TN, preferred_element_type=F32))

    state_t = state_ref[...]
    states = []
    for decay, kv in zip(decays, kvs):
        states.append(state_t.astype(BF16))
        state_t = state_t * decay + kv
    state_ref[...] = state_t

    for rows, q_in, v, att, state_in in zip(chunks, q_ins, vs, atts, states):
        att = jnp.where(causal, att, 0.0).astype(BF16)
        o = _dot(att, v) + lax.dot_general(q_in, state_in, _NT, preferred_element_type=F32)
        o_ref[rows, :] = (_rms(o, onorm) * gate_ref[rows, :].astype(F32)).astype(o_ref.dtype)


def _hgrn2(qs, lf, k, v, gate, onorm, bn, seq):
    t, aw = qs.shape
    heads = aw // HEAD_DIM
    step_rows = min(HGRN_ROWS, seq)
    steps = seq // step_rows
    spec = pl.BlockSpec((step_rows, HEAD_DIM), lambda b, h, s: (b * steps + s, h))
    return pl.pallas_call(
        _hgrn2_kernel,
        out_shape=jax.ShapeDtypeStruct((t, aw), BF16),
        grid=(bn, heads, steps),
        in_specs=[pl.BlockSpec((1, HEAD_DIM), lambda b, h, s: (0, h)), spec, spec, spec, spec, spec],
        out_specs=spec,
        scratch_shapes=[pltpu.VMEM((HEAD_DIM, HEAD_DIM), F32)],
        compiler_params=pltpu.CompilerParams(
            dimension_semantics=("parallel", "parallel", "arbitrary")),
        name="hgrn2",
    )(onorm, qs, lf, k, v, gate)


def _mix_ffn_kernel(h_ref, o_ref, mo_ref, wo_ref, nf_ref, wgu_ref, wd_ref, h_out):
    main_w = o_ref.shape[1]
    hidden = wd_ref.shape[0]
    h = h_ref[...] + _dot(o_ref[...], wo_ref[0:main_w, :]) + _dot(mo_ref[...], wo_ref[main_w:, :])
    hn = _rms(h, nf_ref[...]).astype(BF16)
    act = (_silu(_dot(hn, wgu_ref[:, 0:hidden])) * _dot(hn, wgu_ref[:, hidden:])).astype(BF16)
    h_out[...] = h + _dot(act, wd_ref[...])


def _layer_resident(stacked, layer):
    zeros = (0,) * (stacked.ndim - 1)
    return pl.BlockSpec((None,) + stacked.shape[1:], lambda *_: (layer,) + zeros,
                        pipeline_mode=pl.Buffered(1))


def _mix_ffn(h, o, mo, w_out, norm_ffn, w_gate_up, w_down, layer):
    t, dm = h.shape
    main_w, mw = o.shape[1], mo.shape[1]
    row = lambda i: (i, 0)
    resident = sum(_nbytes(w.shape[1:], BF16) for w in (w_out, w_gate_up, w_down))
    tile = FFN_TILE
    pipelined = 2 * _nbytes((tile, dm), F32) + _nbytes((tile, main_w + mw), BF16)
    temps = 3 * _nbytes((tile, dm), F32) + 3 * _nbytes((tile, w_down.shape[1]), F32)
    return pl.pallas_call(
        _mix_ffn_kernel,
        out_shape=jax.ShapeDtypeStruct((t, dm), F32),
        grid=(t // tile,),
        in_specs=[
            pl.BlockSpec((tile, dm), row), pl.BlockSpec((tile, main_w), row),
            pl.BlockSpec((tile, mw), row),
            _layer_resident(w_out, layer), _resident((1, dm)), _layer_resident(w_gate_up, layer),
            _layer_resident(w_down, layer),
        ],
        out_specs=pl.BlockSpec((tile, dm), row),
        compiler_params=pltpu.CompilerParams(
            dimension_semantics=("parallel",),
            vmem_limit_bytes=_vmem_limit(pipelined, resident, temps)),
        name="mix_ffn",
    )(h, o, mo, w_out, norm_ffn, w_gate_up, w_down)


def _inproj_b_kernel(*refs, dilations):
    n_groups = len(dilations)
    (x_ref, gain_ref, kvg_ref, w_ref, wkv_ref, qn_ref, kn_ref, cos_ref, sin_ref,
     mqg_ref, mk_ref, mv_ref) = refs[:12]
    q_refs = refs[12:12 + n_groups]
    k_refs = refs[12 + n_groups:12 + 2 * n_groups]
    v_refs = refs[12 + 2 * n_groups:12 + 3 * n_groups]
    mo_ref = refs[12 + 3 * n_groups]
    scratch = refs[13 + 3 * n_groups:]
    q_slabs = dict(zip([d for d in dilations if d != 1], scratch[:-5]))
    k_slab, v_slab, q_tmp, k_tmp, v_tmp = scratch[-5:]
    kv_w = wkv_ref.shape[1] // 2
    bw = (w_ref.shape[1] - mo_ref.shape[1]) // n_groups
    heads = range(bw // HEAD_DIM)
    pair = 2
    q_scale = (HEAD_DIM ** -0.5) * LOG2E

    for row0 in range(0, x_ref.shape[0], ATTN_IN_SUB):
        rows = slice(row0, row0 + ATTN_IN_SUB)
        x = x_ref[rows, :]
        xhat = x * lax.rsqrt(jnp.mean(x * x, axis=-1, keepdims=True) + EPS)
        xn = (xhat * gain_ref[...]).astype(BF16)
        kn = (xhat * kvg_ref[...]).astype(BF16)
        cos, sin = cos_ref[rows, :], sin_ref[rows, :]
        cos_s, sin_s = cos * q_scale, sin * q_scale

        def queries(gi, row0=row0, rows=rows, xn=xn, cos_s=cos_s, sin_s=sin_s):
            dil, q_ref = dilations[gi], q_refs[gi]
            for hd in heads:
                if hd % pair == 0:
                    col0 = gi * bw + hd * HEAD_DIM
                    qs = _dot(xn, w_ref[:, col0:col0 + pair * HEAD_DIM])
                q = _rope(_rms(qs[:, _head_cols(hd % pair)], qn_ref[gi]), cos_s, sin_s)
                if dil == 1:
                    q_ref[rows, _head_cols(hd)] = q.astype(q_ref.dtype)
                else:
                    q_slabs[dil][hd] = q
            if dil != 1:
                _store_by_residue(q_slabs[dil], q_tmp, [(dil, q_ref)], row0)

        def copies(slab, tmp, out_refs, row0=row0, rows=rows):
            for dil, out_ref in zip(dilations, out_refs):
                if dil == 1:
                    for hd in heads:
                        out_ref[rows, _head_cols(hd)] = slab[hd].astype(out_ref.dtype)
            _store_by_residue(slab, tmp, [(d, ref) for d, ref in zip(dilations, out_refs) if d != 1], row0)

        order = sorted(range(n_groups), key=lambda gi: -dilations[gi])
        queries(order[0])
        probs = _memory_probs(_dot(xn, w_ref[:, n_groups * bw:]), mqg_ref[...], mk_ref)
        for hd in heads:
            if hd % pair == 0:
                k = _dot(kn, wkv_ref[:, hd * HEAD_DIM:(hd + pair) * HEAD_DIM])
            k_slab[hd] = _rope(_rms(k[:, _head_cols(hd % pair)], kn_ref[...]), cos, sin)
        copies(k_slab, k_tmp, k_refs)
        for gi in order[1:-1]:
            queries(gi)
        _memory_output(probs, mv_ref, mo_ref, rows)
        for hd in heads:
            if hd % pair == 0:
                v = _dot(kn, wkv_ref[:, kv_w + hd * HEAD_DIM:kv_w + (hd + pair) * HEAD_DIM])
            v_slab[hd] = v[:, _head_cols(hd % pair)]
        copies(v_slab, v_tmp, v_refs)
        queries(order[-1])


def _residue_out(bn, seq, width, dil):
    tiles_per_seq = seq // ATTN_IN_TILE
    if dil == 1:
        return (jax.ShapeDtypeStruct((bn * seq, width), BF16),
                pl.BlockSpec((ATTN_IN_TILE, width), lambda i: (i, 0)))
    return (jax.ShapeDtypeStruct((bn, dil, seq // dil, width), BF16),
            pl.BlockSpec((None, dil, ATTN_IN_TILE // dil, width),
                         lambda i: (i // tiles_per_seq, 0, i % tiles_per_seq, 0)))


def _inproj_b(h, gain, kv_gain, w, w_kv, q_norm, k_norm, cos2, sin2, mq_gain, mk, mv, layer, bn, seq):
    t, dm = h.shape
    mt, mw = mk.shape[2], mk.shape[3]
    dilations = tuple(d for _, d in DILATED_GROUPS)
    n_groups = len(dilations)
    bw = (w.shape[1] - mw) // n_groups
    kv_w = w_kv.shape[1] // 2
    assert kv_w == bw
    tile = ATTN_IN_TILE
    tiles_per_seq = seq // tile
    row = lambda i: (i, 0)
    pos = lambda i: (i % tiles_per_seq, 0)
    mem = lambda i: (layer, i // tiles_per_seq, 0, 0)
    copies = [_residue_out(bn, seq, bw, dil) for dil in dilations] * 3
    slab = (bw // HEAD_DIM, ATTN_IN_SUB, HEAD_DIM)
    n_slabs = sum(1 for dil in dilations if dil != 1) + 5
    pipelined = (_nbytes((tile, dm), F32) + 3 * n_groups * _nbytes((tile, bw), BF16)
                 + 2 * _nbytes((tile, HEAD_DIM), F32) + _nbytes((tile, mw), BF16)
                 + 2 * _nbytes((mt, mw), BF16))
    resident = _nbytes(w.shape, BF16) + _nbytes(w_kv.shape, BF16) + n_slabs * _nbytes(slab, F32)
    return pl.pallas_call(
        functools.partial(_inproj_b_kernel, dilations=dilations),
        out_shape=tuple([shape for shape, _ in copies] + [jax.ShapeDtypeStruct((t, mw), BF16)]),
        grid=(t // tile,),
        in_specs=[
            pl.BlockSpec((tile, dm), row),
            _resident((1, dm)), _resident((1, dm)),
            _resident(w.shape), _resident(w_kv.shape),
            _resident((n_groups, 1, HEAD_DIM)), _resident((1, HEAD_DIM)),
            pl.BlockSpec((tile, HEAD_DIM), pos), pl.BlockSpec((tile, HEAD_DIM), pos),
            _resident((1, V7X_LANES)),
            pl.BlockSpec((None, None, mt, mw), mem),
            pl.BlockSpec((None, None, mt, mw), mem),
        ],
        out_specs=tuple([spec for _, spec in copies] + [pl.BlockSpec((tile, mw), row)]),
        scratch_shapes=[pltpu.VMEM(slab, F32)] * n_slabs,
        compiler_params=pltpu.CompilerParams(
            dimension_semantics=("parallel",),
            vmem_limit_bytes=_vmem_limit(pipelined, resident, 6 * _nbytes((ATTN_IN_SUB, bw), F32))),
        name="inproj_b",
    )(h, gain, kv_gain, w, w_kv, q_norm.reshape(n_groups, 1, HEAD_DIM), k_norm, cos2, sin2,
      mq_gain, mk, mv)


def _dilated_kernel(*refs, dilations):
    n_groups = len(dilations)
    q_refs = refs[:n_groups]
    k_refs = refs[n_groups:2 * n_groups]
    v_refs = refs[2 * n_groups:3 * n_groups]
    o_ref, og_ref, lse_ref, bias_ref, ones_ref = refs[3 * n_groups:]
    ones_ref[...] = jnp.ones_like(ones_ref)
    seq = o_ref.shape[0]
    blk = ATTN_BLOCK

    qi = lax.broadcasted_iota(jnp.int32, (blk, 2 * blk), 0)
    kj = lax.broadcasted_iota(jnp.int32, (blk, 2 * blk), 1)
    band = (kj >= qi) & (kj <= qi + blk)
    bias_ref[0] = jnp.where(band & (kj >= blk), 0.0, -jnp.inf)
    bias_ref[1] = jnp.where(band, 0.0, -jnp.inf)
    bias_ref[2] = jnp.where(kj <= qi, 0.0, -jnp.inf)

    parked = [gi for gi, dil in enumerate(dilations) if dil != 1]
    (last,) = [gi for gi, dil in enumerate(dilations) if dil == 1]
    for gi in parked + [last]:
        dil = dilations[gi]
        q_ref, k_ref, v_ref = q_refs[gi], k_refs[gi], v_refs[gi]
        n_blocks = seq // (blk * dil)

        def blocks(step, carry, q_ref=q_ref, k_ref=k_ref, v_ref=v_ref, gi=gi, dil=dil, n_blocks=n_blocks):
            idxs = [step * ATTN_UNROLL + u for u in range(ATTN_UNROLL)]
            starts = [pl.multiple_of(idx * blk, blk) for idx in idxs]
            windows = [pl.ds(pl.multiple_of(jnp.maximum(start - blk, 0), blk), 2 * blk) for start in starts]
            scores = [lax.dot_general(q_ref[pl.ds(start, blk), :], k_ref[window, :], _NT,
                                      preferred_element_type=F32)
                      for start, window in zip(starts, windows)]
            soft = []
            for idx, s in zip(idxs, scores):
                n = lax.rem(idx, n_blocks)
                s = s + bias_ref[jnp.where(idx == 0, 2, jnp.minimum(n, 1))]
                m = jnp.max(s, axis=-1, keepdims=True)
                p = jnp.exp2(s - m).astype(BF16)
                soft.append((p, m))
            for idx, start, window, (p, m) in zip(idxs, starts, windows, soft):
                denom = _dot(p, ones_ref[...])
                o = _dot(p, v_ref[window, :]) / denom
                lse2 = m + jnp.log(denom) * LOG2E
                if dil != 1:
                    n = lax.rem(idx, n_blocks)
                    out_rows = pl.ds(n * (blk * dil) + lax.div(idx, n_blocks), blk, stride=dil)
                    slot = parked.index(gi)
                    og_ref[slot, out_rows, :] = o
                    lse_ref[slot, out_rows, :] = lse2
                else:
                    rows = pl.ds(start, blk)
                    lses = [lse2] + [lse_ref[slot, rows, :] for slot in range(len(parked))]
                    outs = [o] + [og_ref[slot, rows, :] for slot in range(len(parked))]
                    top = functools.reduce(jnp.maximum, lses)
                    ws = [jnp.exp2(l - top) for l in lses]
                    acc = sum(w * og for w, og in zip(ws, outs))
                    o_ref[rows, :] = (acc / sum(ws)).astype(o_ref.dtype)
            return carry

        lax.fori_loop(0, dil * n_blocks // ATTN_UNROLL, blocks, 0)


def _dilated_attention(qs, ks, vs, bn, seq):
    width = qs[0].shape[-1]
    heads = width // HEAD_DIM
    dilations = tuple(d for _, d in DILATED_GROUPS)
    n_groups = len(dilations)
    as_seq = lambda a: a.reshape(bn, seq, width)
    spec = pl.BlockSpec((None, seq, HEAD_DIM), lambda b, h: (b, 0, h))
    seq_bf16 = _nbytes((seq, HEAD_DIM), BF16)
    seq_f32 = _nbytes((seq, HEAD_DIM), F32)
    n_bias = 3
    scratch = (2 * (n_groups - 1) * seq_f32 + _nbytes((n_bias, ATTN_BLOCK, 2 * ATTN_BLOCK), F32)
               + _nbytes((2 * ATTN_BLOCK, HEAD_DIM), BF16))
    staged = ATTN_UNROLL * (_nbytes((ATTN_BLOCK, 2 * ATTN_BLOCK), F32) + _nbytes((ATTN_BLOCK, 2 * ATTN_BLOCK), BF16)
                            + 2 * _nbytes((ATTN_BLOCK, HEAD_DIM), F32))
    out = pl.pallas_call(
        functools.partial(_dilated_kernel, dilations=dilations),
        out_shape=jax.ShapeDtypeStruct((bn, seq, width), BF16),
        grid=(bn, heads),
        in_specs=[spec] * (3 * n_groups),
        out_specs=spec,
        scratch_shapes=[
            pltpu.VMEM((n_groups - 1, seq, HEAD_DIM), F32),
            pltpu.VMEM((n_groups - 1, seq, HEAD_DIM), F32),
            pltpu.VMEM((n_bias, ATTN_BLOCK, 2 * ATTN_BLOCK), F32),
            pltpu.VMEM((2 * ATTN_BLOCK, HEAD_DIM), BF16),
        ],
        compiler_params=pltpu.CompilerParams(
            dimension_semantics=("parallel", "parallel"),
            vmem_limit_bytes=_vmem_limit((3 * n_groups + 1) * seq_bf16, scratch, staged)),
        name="dilated_attention",
    )(*[as_seq(a) for a in (*qs, *ks, *vs)])
    return out.reshape(bn * seq, width)


def _rope_tables(seq):
    half = HEAD_DIM // 2
    inv = ROPE_THETA ** (-jnp.arange(half, dtype=F32) / half)
    ang = jnp.arange(seq).astype(F32)[:, None] * inv[None, :]
    cos, sin = jnp.cos(ang), jnp.sin(ang)
    return jnp.concatenate([cos, cos], axis=-1), jnp.concatenate([-sin, sin], axis=-1)


def kernel(x, mem, norm_mix, norm_ffn, a_w_in, a_lb_logits, a_onorm, b_w_in, b_qnorm, kv_norm, w_kv,
           b_knorm, mem_norm, w_mem_kv, mem_qnorm, mem_knorm, w_out, w_gate_up, w_down):
    bn, seq, dm = x.shape
    depth = norm_mix.shape[0]
    n_a = a_w_in.shape[0]
    max_dil = max(d for _, d in DILATED_GROUPS)
    assert all(w == ATTN_BLOCK * d for w, d in DILATED_GROUPS)
    assert seq % (ATTN_BLOCK * max_dil) == 0 and seq % (ATTN_BLOCK * ATTN_UNROLL) == 0
    assert seq % min(HGRN_ROWS, seq) == 0 and HGRN_ROWS % CHUNK == 0 and HGRN_IN_SUB % CHUNK == 0
    assert seq % HGRN_IN_TILE == 0 and HGRN_IN_TILE % HGRN_IN_SUB == 0 and seq % FFN_TILE == 0
    assert seq % ATTN_IN_TILE == 0 and ATTN_IN_TILE % ATTN_IN_SUB == 0 and ATTN_IN_SUB % max_dil == 0
    assert depth - n_a == 1

    bf = lambda a: a.astype(BF16)
    row_vec = lambda a: a.reshape(1, -1)
    pair = lambda a: jnp.concatenate([a, a], axis=-1).reshape(1, V7X_LANES)
    cos2, sin2 = _rope_tables(seq)
    n_groups = len(DILATED_GROUPS)

    w_out_bf, w_gate_up_bf, w_down_bf = bf(w_out), bf(w_gate_up), bf(w_down)
    mk, mv = _mem_kv(mem, mem_norm, bf(w_mem_kv), mem_knorm)
    h = x.reshape(bn * seq, dm)
    for l in range(depth):
        gain = row_vec(norm_mix[l])
        if l < n_a:
            qs, lf, k, v, gate, mo = _inproj_a(h, gain, bf(a_w_in[l]), a_lb_logits, pair(mem_qnorm[l]),
                                               mk, mv, l, seq)
            o = _hgrn2(qs, lf, k, v, gate, row_vec(a_onorm[l]), bn, seq)
        else:
            j = l - n_a
            *copies, mo = _inproj_b(h, gain, row_vec(kv_norm), bf(b_w_in[j]), bf(w_kv), b_qnorm[j],
                                    row_vec(b_knorm), cos2, sin2, pair(mem_qnorm[l]), mk, mv, l, bn, seq)
            q_groups, ks, vs = (copies[i * n_groups:(i + 1) * n_groups] for i in range(3))
            o = _dilated_attention(q_groups, ks, vs, bn, seq)
        h = _mix_ffn(h, o, mo, w_out_bf, row_vec(norm_ffn[l]), w_gate_up_bf, w_down_bf, l)
    return h.reshape(bn, seq, dm)
```

```python
import functools
import math

import jax
import jax.numpy as jnp
from jax import lax
from jax.experimental import pallas as pl
from jax.experimental.pallas import tpu as pltpu

F32 = jnp.float32
BF16 = jnp.bfloat16

EPS = 1e-6
HEAD_DIM = 128
CHUNK = 64
MEM_HEAD_DIM = 64
DILATED_GROUPS = ((128, 1), (512, 4), (2048, 16))
ROPE_THETA = 10000.0
LOG2E = math.log2(math.e)

V7X_LANES = 128
V7X_VMEM_SCOPED_DEFAULT_BYTES = 16 * 1024 * 1024
V7X_VMEM_SCOPED_MAX_BYTES = 60000 * 1024
V7X_SINGLE_LOAD_STRIDE = 4

HGRN_IN_TILE = 512
HGRN_IN_SUB = 512
ATTN_IN_TILE = 512
ATTN_IN_SUB = 256
FFN_TILE = 512
HGRN_ROWS = 4096
ATTN_BLOCK = 128
ATTN_UNROLL = 16

_NT = (((1,), (1,)), ((), ()))
_TN = (((0,), (0,)), ((), ()))


def _vmem_limit(pipelined_bytes, resident_bytes, temp_bytes):
    need = 2 * pipelined_bytes + resident_bytes + temp_bytes
    return int(min(max(need, V7X_VMEM_SCOPED_DEFAULT_BYTES), V7X_VMEM_SCOPED_MAX_BYTES))


def _nbytes(shape, dtype):
    n = 1
    for s in shape:
        n *= s
    return n * jnp.dtype(dtype).itemsize


def _resident(shape):
    zeros = (0,) * len(shape)
    return pl.BlockSpec(shape, lambda *_: zeros, pipeline_mode=pl.Buffered(1))


def _dot(a, b):
    return jnp.dot(a, b, preferred_element_type=F32)


def _rms(x, gain):
    ms = jnp.mean(x * x, axis=-1, keepdims=True)
    return x * lax.rsqrt(ms + EPS) * gain


def _silu(x):
    return x * jax.nn.sigmoid(x)


def _head_cols(hd):
    return slice(hd * HEAD_DIM, (hd + 1) * HEAD_DIM)


def _store_heads(ref, rows, value):
    for hd in range(ref.shape[0]):
        ref[hd, rows, :] = value[:, _head_cols(hd)].astype(ref.dtype)


def _rms_head_pairs(x, gain):
    lo = lax.broadcasted_iota(jnp.int32, x.shape, 1) < MEM_HEAD_DIM
    x2 = x * x
    s_lo = jnp.sum(jnp.where(lo, x2, 0.0), axis=-1, keepdims=True)
    s_hi = jnp.sum(jnp.where(lo, 0.0, x2), axis=-1, keepdims=True)
    ms = jnp.where(lo, s_lo, s_hi) * (1.0 / MEM_HEAD_DIM)
    return x * lax.rsqrt(ms + EPS) * gain


def _rope(x, cos2, sin2):
    return x * cos2 + pltpu.roll(x, HEAD_DIM // 2, axis=1) * sin2


def _store_by_residue(slab_ref, tmp_ref, dil_outs, row0):
    n_heads, tile_rows = slab_ref.shape[0], slab_ref.shape[1]
    base = V7X_SINGLE_LOAD_STRIDE
    two_hops = any(dil > base for dil, _ in dil_outs)
    if two_hops:
        part = tile_rows // base
        for r in range(base):
            for hd in range(n_heads):
                tmp_ref[hd, r * part:(r + 1) * part, :] = slab_ref[hd, pl.ds(r, part, stride=base), :]
    for dil, out_ref in dil_outs:
        rows = tile_rows // dil
        dst = slice(row0 // dil, row0 // dil + rows)
        for r in range(dil):
            for hd in range(n_heads):
                if dil < base or (dil == base and not two_hops):
                    piece = slab_ref[hd, pl.ds(r, rows, stride=dil), :]
                elif dil == base:
                    piece = tmp_ref[hd, r * rows:(r + 1) * rows, :]
                else:
                    assert dil % base == 0 and dil // base <= base
                    piece = tmp_ref[hd, pl.ds((r % base) * part + r // base, rows, stride=dil // base), :]
                out_ref[hd, r, dst, :] = piece.astype(out_ref.dtype)


def _memory_probs(mq, qgain, mk_ref):
    scaled_gain = qgain * (MEM_HEAD_DIM ** -0.5)
    probs = []
    for t in range(mq.shape[1] // V7X_LANES):
        cols = slice(t * V7X_LANES, (t + 1) * V7X_LANES)
        qn = _rms_head_pairs(mq[:, cols], scaled_gain)
        lo = lax.broadcasted_iota(jnp.int32, qn.shape, 1) < MEM_HEAD_DIM
        for keep in (lo, jnp.logical_not(lo)):
            qh = jnp.where(keep, qn, 0.0).astype(BF16)
            s = lax.dot_general(qh, mk_ref[:, cols], _NT, preferred_element_type=F32)
            p = jnp.exp(s - jnp.max(s, axis=-1, keepdims=True))
            probs.append((p.astype(BF16), jnp.sum(p, axis=-1, keepdims=True)))
    return probs


def _memory_output(probs, mv_ref, mo_ref, rows=slice(None)):
    for t in range(mo_ref.shape[1] // V7X_LANES):
        cols = slice(t * V7X_LANES, (t + 1) * V7X_LANES)
        outs = [_dot(p, mv_ref[:, cols]) / denom for p, denom in probs[2 * t:2 * t + 2]]
        lo = lax.broadcasted_iota(jnp.int32, outs[0].shape, 1) < MEM_HEAD_DIM
        mo_ref[rows, cols] = jnp.where(lo, outs[0], outs[1]).astype(mo_ref.dtype)


def _mem_kv_kernel(mem_ref, gain_ref, w_ref, kgain_ref, mk_ref, mv_ref):
    mw = mk_ref.shape[1]
    mn = _rms(mem_ref[...], gain_ref[...]).astype(BF16)
    kv = _dot(mn, w_ref[...])
    for t in range(mw // V7X_LANES):
        cols = slice(t * V7X_LANES, (t + 1) * V7X_LANES)
        mk_ref[:, cols] = _rms_head_pairs(kv[:, cols], kgain_ref[...]).astype(mk_ref.dtype)
    mv_ref[...] = kv[:, mw:].astype(mv_ref.dtype)


def _mem_kv(mem, mem_norm, w_mem_kv, mem_knorm):
    bn, mt, dm = mem.shape
    depth = w_mem_kv.shape[0]
    mw = w_mem_kv.shape[2] // 2
    kgain = jnp.concatenate([mem_knorm, mem_knorm], axis=-1).reshape(depth, 1, V7X_LANES)
    out = jax.ShapeDtypeStruct((depth, bn, mt, mw), BF16)
    return pl.pallas_call(
        _mem_kv_kernel,
        out_shape=(out, out),
        grid=(depth, bn),
        in_specs=[
            pl.BlockSpec((None, mt, dm), lambda l, b: (b, 0, 0)),
            pl.BlockSpec((None, 1, dm), lambda l, b: (l, 0, 0)),
            pl.BlockSpec((None, dm, 2 * mw), lambda l, b: (l, 0, 0)),
            pl.BlockSpec((None, 1, V7X_LANES), lambda l, b: (l, 0, 0)),
        ],
        out_specs=(
            pl.BlockSpec((None, None, mt, mw), lambda l, b: (l, b, 0, 0)),
            pl.BlockSpec((None, None, mt, mw), lambda l, b: (l, b, 0, 0)),
        ),
        name="mem_kv",
    )(mem, mem_norm.reshape(depth, 1, dm), w_mem_kv, kgain)


def _inproj_a_kernel(x_ref, gain_ref, w_ref, lbl_ref, mqg_ref, mk_ref, mv_ref,
                     qs_ref, lf_ref, k_ref, v_ref, gate_ref, mo_ref, *, layer):
    aw = qs_ref.shape[0] * HEAD_DIM
    lg = lbl_ref[...]
    e = jnp.exp(lg - jnp.max(lg, axis=0, keepdims=True))
    lb = jnp.sum(e[:layer + 1], axis=0, keepdims=True) / jnp.sum(e, axis=0, keepdims=True)
    for row0 in range(0, x_ref.shape[0], HGRN_IN_SUB):
        rows = slice(row0, row0 + HGRN_IN_SUB)
        xn = _rms(x_ref[rows, :], gain_ref[...]).astype(BF16)
        probs = _memory_probs(_dot(xn, w_ref[:, 4 * aw:]), mqg_ref[...], mk_ref)
        _store_heads(qs_ref, rows, _silu(_dot(xn, w_ref[:, 0:aw])))
        _memory_output(probs, mv_ref, mo_ref, rows)
        f = lb + (1.0 - lb) * jax.nn.sigmoid(_dot(xn, w_ref[:, aw:2 * aw]))
        _store_heads(lf_ref, rows, jnp.log(f) * LOG2E)
        _store_heads(k_ref, rows, 1.0 - f)
        _store_heads(v_ref, rows, _dot(xn, w_ref[:, 2 * aw:3 * aw]))
        _store_heads(gate_ref, rows, _silu(_dot(xn, w_ref[:, 3 * aw:4 * aw])))


def _inproj_a(h, gain, w, lb_logits, mq_gain, mk, mv, layer, seq):
    t, dm = h.shape
    mt, mw = mk.shape[2], mk.shape[3]
    aw = (w.shape[1] - mw) // 4
    tile = HGRN_IN_TILE
    tiles_per_seq = seq // tile
    row = lambda i: (i, 0)
    mem = lambda i: (layer, i // tiles_per_seq, 0, 0)
    heads = aw // HEAD_DIM
    wide = functools.partial(jax.ShapeDtypeStruct, (heads, t, HEAD_DIM))
    pipelined = (_nbytes((tile, dm), F32) + _nbytes((tile, aw), F32)
                 + 4 * _nbytes((tile, aw), BF16) + _nbytes((tile, mw), BF16)
                 + 2 * _nbytes((mt, mw), BF16))
    return pl.pallas_call(
        functools.partial(_inproj_a_kernel, layer=layer),
        out_shape=(wide(BF16), wide(F32), wide(BF16), wide(BF16), wide(BF16),
                   jax.ShapeDtypeStruct((t, mw), BF16)),
        grid=(t // tile,),
        in_specs=[
            pl.BlockSpec((tile, dm), row),
            _resident((1, dm)),
            _resident(w.shape),
            _resident(lb_logits.shape),
            _resident((1, V7X_LANES)),
            pl.BlockSpec((None, None, mt, mw), mem),
            pl.BlockSpec((None, None, mt, mw), mem),
        ],
        out_specs=tuple([pl.BlockSpec((heads, tile, HEAD_DIM), lambda i: (0, i, 0))] * 5
                        + [pl.BlockSpec((tile, mw), row)]),
        compiler_params=pltpu.CompilerParams(
            dimension_semantics=("parallel",),
            vmem_limit_bytes=_vmem_limit(pipelined, _nbytes(w.shape, BF16),
                                         4 * _nbytes((HGRN_IN_SUB, aw), F32))),
        name="inproj_a",
    )(h, gain, w, lb_logits, mq_gain, mk, mv)


def _hgrn2_kernel(on_ref, qs_ref, lf_ref, k_ref, v_ref, gate_ref, o_ref, state_ref):
    @pl.when(pl.program_id(2) == 0)
    def _():
        state_ref[...] = jnp.zeros_like(state_ref)

    row = lax.broadcasted_iota(jnp.int32, (CHUNK, HEAD_DIM), 0)
    causal = (lax.broadcasted_iota(jnp.int32, (CHUNK, CHUNK), 0)
              >= lax.broadcasted_iota(jnp.int32, (CHUNK, CHUNK), 1))
    onorm = on_ref[...]

    chunks = [slice(c * CHUNK, (c + 1) * CHUNK) for c in range(qs_ref.shape[0] // CHUNK)]
    q_ins, vs, decays, atts, kvs = [], [], [], [], []
    for rows in chunks:
        b = lf_ref[rows, :]
        shift = 1
        while shift < CHUNK:
            b = b + jnp.where(row >= shift, pltpu.roll(b, shift, axis=0), 0.0)
            shift *= 2
        b_end = b[CHUNK - 1:CHUNK, :]
        k = k_ref[rows, :].astype(F32)
        q_in = (qs_ref[rows, :].astype(F32) * jnp.exp2(b)).astype(BF16)
        k_in = (k * jnp.exp2(-b)).astype(BF16)
        k_out = (k * jnp.exp2(b_end - b)).astype(BF16)
        v = v_ref[rows, :]
        q_ins.append(q_in)
        vs.append(v)
        decays.append(jnp.exp2(b_end))
        atts.append(lax.dot_general(q_in, k_in, _NT, preferred_element_type=F32))
        kvs.append(lax.dot_general(v, k_out, _TN, preferred_element_type=F32))

    state_t = state_ref[...]
    states = []
    for decay, kv in zip(decays, kvs):
        states.append(state_t.astype(BF16))
        state_t = state_t * decay + kv
    state_ref[...] = state_t

    for rows, q_in, v, att, state_in in zip(chunks, q_ins, vs, atts, states):
        att = jnp.where(causal, att, 0.0).astype(BF16)
        o = _dot(att, v) + lax.dot_general(q_in, state_in, _NT, preferred_element_type=F32)
        o_ref[rows, :] = (_rms(o, onorm) * gate_ref[rows, :].astype(F32)).astype(o_ref.dtype)


def _hgrn2(qs, lf, k, v, gate, onorm, bn, seq):
    heads, t, _ = qs.shape
    step_rows = min(HGRN_ROWS, seq)
    steps = seq // step_rows
    spec = pl.BlockSpec((None, step_rows, HEAD_DIM), lambda b, h, s: (h, b * steps + s, 0))
    return pl.pallas_call(
        _hgrn2_kernel,
        out_shape=jax.ShapeDtypeStruct((t, heads * HEAD_DIM), BF16),
        grid=(bn, heads, steps),
        in_specs=[pl.BlockSpec((1, HEAD_DIM), lambda b, h, s: (0, h)), spec, spec, spec, spec, spec],
        out_specs=pl.BlockSpec((step_rows, HEAD_DIM), lambda b, h, s: (b * steps + s, h)),
        scratch_shapes=[pltpu.VMEM((HEAD_DIM, HEAD_DIM), F32)],
        compiler_params=pltpu.CompilerParams(
            dimension_semantics=("parallel", "parallel", "arbitrary")),
        name="hgrn2",
    )(onorm, qs, lf, k, v, gate)


def _mix_ffn_kernel(h_ref, o_ref, mo_ref, wo_ref, nf_ref, wgu_ref, wd_ref, h_out):
    main_w = o_ref.shape[1]
    hidden = wd_ref.shape[0]
    h = h_ref[...] + _dot(o_ref[...], wo_ref[0:main_w, :]) + _dot(mo_ref[...], wo_ref[main_w:, :])
    hn = _rms(h, nf_ref[...]).astype(BF16)
    act = (_silu(_dot(hn, wgu_ref[:, 0:hidden])) * _dot(hn, wgu_ref[:, hidden:])).astype(BF16)
    h_out[...] = h + _dot(act, wd_ref[...])


def _layer_resident(stacked, layer):
    zeros = (0,) * (stacked.ndim - 1)
    return pl.BlockSpec((None,) + stacked.shape[1:], lambda *_: (layer,) + zeros,
                        pipeline_mode=pl.Buffered(1))


def _mix_ffn(h, o, mo, w_out, norm_ffn, w_gate_up, w_down, layer):
    t, dm = h.shape
    main_w, mw = o.shape[1], mo.shape[1]
    row = lambda i: (i, 0)
    resident = sum(_nbytes(w.shape[1:], BF16) for w in (w_out, w_gate_up, w_down))
    tile = FFN_TILE
    pipelined = 2 * _nbytes((tile, dm), F32) + _nbytes((tile, main_w + mw), BF16)
    temps = 3 * _nbytes((tile, dm), F32) + 3 * _nbytes((tile, w_down.shape[1]), F32)
    return pl.pallas_call(
        _mix_ffn_kernel,
        out_shape=jax.ShapeDtypeStruct((t, dm), F32),
        grid=(t // tile,),
        in_specs=[
            pl.BlockSpec((tile, dm), row), pl.BlockSpec((tile, main_w), row),
            pl.BlockSpec((tile, mw), row),
            _layer_resident(w_out, layer), _resident((1, dm)), _layer_resident(w_gate_up, layer),
            _layer_resident(w_down, layer),
        ],
        out_specs=pl.BlockSpec((tile, dm), row),
        compiler_params=pltpu.CompilerParams(
            dimension_semantics=("parallel",),
            vmem_limit_bytes=_vmem_limit(pipelined, resident, temps)),
        name="mix_ffn",
    )(h, o, mo, w_out, norm_ffn, w_gate_up, w_down)


def _inproj_b_kernel(*refs, dilations):
    n_groups = len(dilations)
    (x_ref, gain_ref, kvg_ref, w_ref, wkv_ref, qn_ref, kn_ref, cos_ref, sin_ref,
     mqg_ref, mk_ref, mv_ref) = refs[:12]
    q_refs = refs[12:12 + n_groups]
    k_refs = refs[12 + n_groups:12 + 2 * n_groups]
    v_refs = refs[12 + 2 * n_groups:12 + 3 * n_groups]
    mo_ref = refs[12 + 3 * n_groups]
    scratch = refs[13 + 3 * n_groups:]
    q_slabs = dict(zip([d for d in dilations if d != 1], scratch[:-5]))
    k_slab, v_slab, q_tmp, k_tmp, v_tmp = scratch[-5:]
    kv_w = wkv_ref.shape[1] // 2
    bw = (w_ref.shape[1] - mo_ref.shape[1]) // n_groups
    heads = range(bw // HEAD_DIM)
    pair = 2
    q_scale = (HEAD_DIM ** -0.5) * LOG2E

    for row0 in range(0, x_ref.shape[0], ATTN_IN_SUB):
        rows = slice(row0, row0 + ATTN_IN_SUB)
        x = x_ref[rows, :]
        xhat = x * lax.rsqrt(jnp.mean(x * x, axis=-1, keepdims=True) + EPS)
        xn = (xhat * gain_ref[...]).astype(BF16)
        kn = (xhat * kvg_ref[...]).astype(BF16)
        cos, sin = cos_ref[rows, :], sin_ref[rows, :]
        cos_s, sin_s = cos * q_scale, sin * q_scale

        def queries(gi, row0=row0, rows=rows, xn=xn, cos_s=cos_s, sin_s=sin_s):
            dil, q_ref = dilations[gi], q_refs[gi]
            for hd in heads:
                if hd % pair == 0:
                    col0 = gi * bw + hd * HEAD_DIM
                    qs = _dot(xn, w_ref[:, col0:col0 + pair * HEAD_DIM])
                q = _rope(_rms(qs[:, _head_cols(hd % pair)], qn_ref[gi]), cos_s, sin_s)
                if dil == 1:
                    q_ref[hd, rows, :] = q.astype(q_ref.dtype)
                else:
                    q_slabs[dil][hd] = q
            if dil != 1:
                _store_by_residue(q_slabs[dil], q_tmp, [(dil, q_ref)], row0)

        def copies(slab, tmp, out_refs, row0=row0, rows=rows):
            for dil, out_ref in zip(dilations, out_refs):
                if dil == 1:
                    for hd in heads:
                        out_ref[hd, rows, :] = slab[hd].astype(out_ref.dtype)
            _store_by_residue(slab, tmp, [(d, ref) for d, ref in zip(dilations, out_refs) if d != 1], row0)

        order = sorted(range(n_groups), key=lambda gi: -dilations[gi])
        queries(order[0])
        probs = _memory_probs(_dot(xn, w_ref[:, n_groups * bw:]), mqg_ref[...], mk_ref)
        for hd in heads:
            if hd % pair == 0:
                k = _dot(kn, wkv_ref[:, hd * HEAD_DIM:(hd + pair) * HEAD_DIM])
            k_slab[hd] = _rope(_rms(k[:, _head_cols(hd % pair)], kn_ref[...]), cos, sin)
        copies(k_slab, k_tmp, k_refs)
        for gi in order[1:-1]:
            queries(gi)
        _memory_output(probs, mv_ref, mo_ref, rows)
        for hd in heads:
            if hd % pair == 0:
                v = _dot(kn, wkv_ref[:, kv_w + hd * HEAD_DIM:kv_w + (hd + pair) * HEAD_DIM])
            v_slab[hd] = v[:, _head_cols(hd % pair)]
        copies(v_slab, v_tmp, v_refs)
        queries(order[-1])


def _residue_out(bn, seq, width, dil):
    tiles_per_seq = seq // ATTN_IN_TILE
    heads = width // HEAD_DIM
    if dil == 1:
        return (jax.ShapeDtypeStruct((bn, heads, seq, HEAD_DIM), BF16),
                pl.BlockSpec((None, heads, ATTN_IN_TILE, HEAD_DIM),
                             lambda i: (i // tiles_per_seq, 0, i % tiles_per_seq, 0)))
    return (jax.ShapeDtypeStruct((bn, heads, dil, seq // dil, HEAD_DIM), BF16),
            pl.BlockSpec((None, heads, dil, ATTN_IN_TILE // dil, HEAD_DIM),
                         lambda i: (i // tiles_per_seq, 0, 0, i % tiles_per_seq, 0)))


def _inproj_b(h, gain, kv_gain, w, w_kv, q_norm, k_norm, cos2, sin2, mq_gain, mk, mv, layer, bn, seq):
    t, dm = h.shape
    mt, mw = mk.shape[2], mk.shape[3]
    dilations = tuple(d for _, d in DILATED_GROUPS)
    n_groups = len(dilations)
    bw = (w.shape[1] - mw) // n_groups
    kv_w = w_kv.shape[1] // 2
    assert kv_w == bw
    tile = ATTN_IN_TILE
    tiles_per_seq = seq // tile
    row = lambda i: (i, 0)
    pos = lambda i: (i % tiles_per_seq, 0)
    mem = lambda i: (layer, i // tiles_per_seq, 0, 0)
    copies = [_residue_out(bn, seq, bw, dil) for dil in dilations] * 3
    slab = (bw // HEAD_DIM, ATTN_IN_SUB, HEAD_DIM)
    n_slabs = sum(1 for dil in dilations if dil != 1) + 5
    pipelined = (_nbytes((tile, dm), F32) + 3 * n_groups * _nbytes((tile, bw), BF16)
                 + 2 * _nbytes((tile, HEAD_DIM), F32) + _nbytes((tile, mw), BF16)
                 + 2 * _nbytes((mt, mw), BF16))
    resident = _nbytes(w.shape, BF16) + _nbytes(w_kv.shape, BF16) + n_slabs * _nbytes(slab, F32)
    return pl.pallas_call(
        functools.partial(_inproj_b_kernel, dilations=dilations),
        out_shape=tuple([shape for shape, _ in copies] + [jax.ShapeDtypeStruct((t, mw), BF16)]),
        grid=(t // tile,),
        in_specs=[
            pl.BlockSpec((tile, dm), row),
            _resident((1, dm)), _resident((1, dm)),
            _resident(w.shape), _resident(w_kv.shape),
            _resident((n_groups, 1, HEAD_DIM)), _resident((1, HEAD_DIM)),
            pl.BlockSpec((tile, HEAD_DIM), pos), pl.BlockSpec((tile, HEAD_DIM), pos),
            _resident((1, V7X_LANES)),
            pl.BlockSpec((None, None, mt, mw), mem),
            pl.BlockSpec((None, None, mt, mw), mem),
        ],
        out_specs=tuple([spec for _, spec in copies] + [pl.BlockSpec((tile, mw), row)]),
        scratch_shapes=[pltpu.VMEM(slab, F32)] * n_slabs,
        compiler_params=pltpu.CompilerParams(
            dimension_semantics=("parallel",),
            vmem_limit_bytes=_vmem_limit(pipelined, resident, 6 * _nbytes((ATTN_IN_SUB, bw), F32))),
        name="inproj_b",
    )(h, gain, kv_gain, w, w_kv, q_norm.reshape(n_groups, 1, HEAD_DIM), k_norm, cos2, sin2,
      mq_gain, mk, mv)


def _dilated_kernel(*refs, dilations):
    n_groups = len(dilations)
    q_refs = refs[:n_groups]
    k_refs = refs[n_groups:2 * n_groups]
    v_refs = refs[2 * n_groups:3 * n_groups]
    o_ref, og_ref, lse_ref, bias_ref = refs[3 * n_groups:]
    seq = o_ref.shape[0]
    blk = ATTN_BLOCK

    qi = lax.broadcasted_iota(jnp.int32, (blk, 2 * blk), 0)
    kj = lax.broadcasted_iota(jnp.int32, (blk, 2 * blk), 1)
    band = (kj >= qi) & (kj <= qi + blk)
    bias_ref[0] = jnp.where(band & (kj >= blk), 0.0, -jnp.inf)
    bias_ref[1] = jnp.where(band, 0.0, -jnp.inf)
    bias_ref[2] = jnp.where(kj <= qi, 0.0, -jnp.inf)

    parked = [gi for gi, dil in enumerate(dilations) if dil != 1]
    (last,) = [gi for gi, dil in enumerate(dilations) if dil == 1]
    for gi in parked + [last]:
        dil = dilations[gi]
        q_ref, k_ref, v_ref = q_refs[gi], k_refs[gi], v_refs[gi]
        n_blocks = seq // (blk * dil)

        def blocks(step, carry, q_ref=q_ref, k_ref=k_ref, v_ref=v_ref, gi=gi, dil=dil, n_blocks=n_blocks):
            idxs = [step * ATTN_UNROLL + u for u in range(ATTN_UNROLL)]
            starts = [pl.multiple_of(idx * blk, blk) for idx in idxs]
            windows = [pl.ds(pl.multiple_of(jnp.maximum(start - blk, 0), blk), 2 * blk) for start in starts]
            scores = [lax.dot_general(q_ref[pl.ds(start, blk), :], k_ref[window, :], _NT,
                                      preferred_element_type=F32)
                      for start, window in zip(starts, windows)]
            soft = []
            for idx, s in zip(idxs, scores):
                n = lax.rem(idx, n_blocks)
                s = s + bias_ref[jnp.where(idx == 0, 2, jnp.minimum(n, 1))]
                m = jnp.max(s, axis=-1, keepdims=True)
                p = jnp.exp2(s - m)
                denom = jnp.sum(p, axis=-1, keepdims=True)
                soft.append((p.astype(BF16), m, denom))
            for idx, start, window, (p, m, denom) in zip(idxs, starts, windows, soft):
                o = _dot(p, v_ref[window, :]) / denom
                lse2 = jnp.broadcast_to(m + jnp.log(denom) * LOG2E, (blk, HEAD_DIM))
                if dil != 1:
                    n = lax.rem(idx, n_blocks)
                    out_rows = pl.ds(n * (blk * dil) + lax.div(idx, n_blocks), blk, stride=dil)
                    slot = parked.index(gi)
                    og_ref[slot, out_rows, :] = o
                    lse_ref[slot, out_rows, :] = lse2
                else:
                    rows = pl.ds(start, blk)
                    lses = [lse2] + [lse_ref[slot, rows, :] for slot in range(len(parked))]
                    outs = [o] + [og_ref[slot, rows, :] for slot in range(len(parked))]
                    top = functools.reduce(jnp.maximum, lses)
                    ws = [jnp.exp2(l - top) for l in lses]
                    acc = sum(w * og for w, og in zip(ws, outs))
                    o_ref[rows, :] = (acc / sum(ws)).astype(o_ref.dtype)
            return carry

        lax.fori_loop(0, dil * n_blocks // ATTN_UNROLL, blocks, 0)


def _dilated_attention(qs, ks, vs, bn, seq):
    heads = qs[0].shape[1]
    width = heads * HEAD_DIM
    dilations = tuple(d for _, d in DILATED_GROUPS)
    n_groups = len(dilations)
    as_seq = lambda a: a.reshape(bn, heads, seq, HEAD_DIM)
    spec = pl.BlockSpec((None, None, seq, HEAD_DIM), lambda b, h: (b, h, 0, 0))
    seq_bf16 = _nbytes((seq, HEAD_DIM), BF16)
    seq_f32 = _nbytes((seq, HEAD_DIM), F32)
    n_bias = 3
    scratch = 2 * (n_groups - 1) * seq_f32 + _nbytes((n_bias, ATTN_BLOCK, 2 * ATTN_BLOCK), F32)
    staged = ATTN_UNROLL * (_nbytes((ATTN_BLOCK, 2 * ATTN_BLOCK), F32) + _nbytes((ATTN_BLOCK, 2 * ATTN_BLOCK), BF16)
                            + 2 * _nbytes((ATTN_BLOCK, HEAD_DIM), F32))
    out = pl.pallas_call(
        functools.partial(_dilated_kernel, dilations=dilations),
        out_shape=jax.ShapeDtypeStruct((bn, seq, width), BF16),
        grid=(bn, heads),
        in_specs=[spec] * (3 * n_groups),
        out_specs=pl.BlockSpec((None, seq, HEAD_DIM), lambda b, h: (b, 0, h)),
        scratch_shapes=[
            pltpu.VMEM((n_groups - 1, seq, HEAD_DIM), F32),
            pltpu.VMEM((n_groups - 1, seq, HEAD_DIM), F32),
            pltpu.VMEM((n_bias, ATTN_BLOCK, 2 * ATTN_BLOCK), F32),
        ],
        compiler_params=pltpu.CompilerParams(
            dimension_semantics=("parallel", "parallel"),
            vmem_limit_bytes=_vmem_limit((3 * n_groups + 1) * seq_bf16, scratch, staged)),
        name="dilated_attention",
    )(*[as_seq(a) for a in (*qs, *ks, *vs)])
    return out.reshape(bn * seq, width)


def _rope_tables(seq):
    half = HEAD_DIM // 2
    inv = ROPE_THETA ** (-jnp.arange(half, dtype=F32) / half)
    ang = jnp.arange(seq).astype(F32)[:, None] * inv[None, :]
    cos, sin = jnp.cos(ang), jnp.sin(ang)
    return jnp.concatenate([cos, cos], axis=-1), jnp.concatenate([-sin, sin], axis=-1)


def kernel(x, mem, norm_mix, norm_ffn, a_w_in, a_lb_logits, a_onorm, b_w_in, b_qnorm, kv_norm, w_kv,
           b_knorm, mem_norm, w_mem_kv, mem_qnorm, mem_knorm, w_out, w_gate_up, w_down):
    bn, seq, dm = x.shape
    depth = norm_mix.shape[0]
    n_a = a_w_in.shape[0]
    max_dil = max(d for _, d in DILATED_GROUPS)
    assert all(w == ATTN_BLOCK * d for w, d in DILATED_GROUPS)
    assert seq % (ATTN_BLOCK * max_dil) == 0 and seq % (ATTN_BLOCK * ATTN_UNROLL) == 0
    assert seq % min(HGRN_ROWS, seq) == 0 and HGRN_ROWS % CHUNK == 0 and HGRN_IN_SUB % CHUNK == 0
    assert seq % HGRN_IN_TILE == 0 and HGRN_IN_TILE % HGRN_IN_SUB == 0 and seq % FFN_TILE == 0
    assert seq % ATTN_IN_TILE == 0 and ATTN_IN_TILE % ATTN_IN_SUB == 0 and ATTN_IN_SUB % max_dil == 0
    assert depth - n_a == 1

    bf = lambda a: a.astype(BF16)
    row_vec = lambda a: a.reshape(1, -1)
    pair = lambda a: jnp.concatenate([a, a], axis=-1).reshape(1, V7X_LANES)
    cos2, sin2 = _rope_tables(seq)
    n_groups = len(DILATED_GROUPS)

    w_out_bf, w_gate_up_bf, w_down_bf = bf(w_out), bf(w_gate_up), bf(w_down)
    mk, mv = _mem_kv(mem, mem_norm, bf(w_mem_kv), mem_knorm)
    h = x.reshape(bn * seq, dm)
    for l in range(depth):
        gain = row_vec(norm_mix[l])
        if l < n_a:
            qs, lf, k, v, gate, mo = _inproj_a(h, gain, bf(a_w_in[l]), a_lb_logits, pair(mem_qnorm[l]),
                                               mk, mv, l, seq)
            o = _hgrn2(qs, lf, k, v, gate, row_vec(a_onorm[l]), bn, seq)
        else:
            j = l - n_a
            *copies, mo = _inproj_b(h, gain, row_vec(kv_norm), bf(b_w_in[j]), bf(w_kv), b_qnorm[j],
                                    row_vec(b_knorm), cos2, sin2, pair(mem_qnorm[l]), mk, mv, l, bn, seq)
            q_groups, ks, vs = (copies[i * n_groups:(i + 1) * n_groups] for i in range(3))
            o = _dilated_attention(q_groups, ks, vs, bn, seq)
        h = _mix_ffn(h, o, mo, w_out_bf, row_vec(norm_ffn[l]), w_gate_up_bf, w_down_bf, l)
    return h.reshape(bn, seq, dm)
```

```python
import functools
import math

import jax
import jax.numpy as jnp
from jax import lax
from jax.experimental import pallas as pl
from jax.experimental.pallas import tpu as pltpu

F32 = jnp.float32
BF16 = jnp.bfloat16

EPS = 1e-6
HEAD_DIM = 128
CHUNK = 64
MEM_HEAD_DIM = 64
DILATED_GROUPS = ((128, 1), (512, 4), (2048, 16))
ROPE_THETA = 10000.0
LOG2E = math.log2(math.e)

V7X_LANES = 128
V7X_VMEM_SCOPED_DEFAULT_BYTES = 16 * 1024 * 1024
V7X_VMEM_SCOPED_MAX_BYTES = 60000 * 1024
V7X_SINGLE_LOAD_STRIDE = 4

HGRN_IN_TILE = 512
HGRN_IN_SUB = 512
ATTN_IN_TILE = 512
ATTN_IN_SUB = 256
FFN_TILE = 512
HGRN_ROWS = 4096
ATTN_BLOCK = 128
ATTN_UNROLL = 16

_NT = (((1,), (1,)), ((), ()))
_TN = (((0,), (0,)), ((), ()))


def _vmem_limit(pipelined_bytes, resident_bytes, temp_bytes):
    need = 2 * pipelined_bytes + resident_bytes + temp_bytes
    return int(min(max(need, V7X_VMEM_SCOPED_DEFAULT_BYTES), V7X_VMEM_SCOPED_MAX_BYTES))


def _nbytes(shape, dtype):
    n = 1
    for s in shape:
        n *= s
    return n * jnp.dtype(dtype).itemsize


def _resident(shape):
    zeros = (0,) * len(shape)
    return pl.BlockSpec(shape, lambda *_: zeros, pipeline_mode=pl.Buffered(1))


def _dot(a, b):
    return jnp.dot(a, b, preferred_element_type=F32)


def _rms(x, gain):
    ms = jnp.mean(x * x, axis=-1, keepdims=True)
    return x * lax.rsqrt(ms + EPS) * gain


def _silu(x):
    return x * jax.nn.sigmoid(x)


def _head_cols(hd):
    return slice(hd * HEAD_DIM, (hd + 1) * HEAD_DIM)


def _rms_head_pairs(x, gain):
    lo = lax.broadcasted_iota(jnp.int32, x.shape, 1) < MEM_HEAD_DIM
    x2 = x * x
    s_lo = jnp.sum(jnp.where(lo, x2, 0.0), axis=-1, keepdims=True)
    s_hi = jnp.sum(jnp.where(lo, 0.0, x2), axis=-1, keepdims=True)
    ms = jnp.where(lo, s_lo, s_hi) * (1.0 / MEM_HEAD_DIM)
    return x * lax.rsqrt(ms + EPS) * gain


def _rope(x, cos2, sin2):
    return x * cos2 + pltpu.roll(x, HEAD_DIM // 2, axis=1) * sin2


def _store_by_residue(slab_ref, tmp_ref, dil_outs, row0):
    n_heads, tile_rows = slab_ref.shape[0], slab_ref.shape[1]
    base = V7X_SINGLE_LOAD_STRIDE
    two_hops = any(dil > base for dil, _ in dil_outs)
    if two_hops:
        part = tile_rows // base
        for r in range(base):
            for hd in range(n_heads):
                tmp_ref[hd, r * part:(r + 1) * part, :] = slab_ref[hd, pl.ds(r, part, stride=base), :]
    for dil, out_ref in dil_outs:
        rows = tile_rows // dil
        dst = slice(row0 // dil, row0 // dil + rows)
        for r in range(dil):
            for hd in range(n_heads):
                if dil < base or (dil == base and not two_hops):
                    piece = slab_ref[hd, pl.ds(r, rows, stride=dil), :]
                elif dil == base:
                    piece = tmp_ref[hd, r * rows:(r + 1) * rows, :]
                else:
                    assert dil % base == 0 and dil // base <= base
                    piece = tmp_ref[hd, pl.ds((r % base) * part + r // base, rows, stride=dil // base), :]
                out_ref[r, dst, _head_cols(hd)] = piece.astype(out_ref.dtype)


def _memory_probs(mq, qgain, mk_ref):
    scaled_gain = qgain * (MEM_HEAD_DIM ** -0.5)
    probs = []
    for t in range(mq.shape[1] // V7X_LANES):
        cols = slice(t * V7X_LANES, (t + 1) * V7X_LANES)
        qn = _rms_head_pairs(mq[:, cols], scaled_gain)
        lo = lax.broadcasted_iota(jnp.int32, qn.shape, 1) < MEM_HEAD_DIM
        for keep in (lo, jnp.logical_not(lo)):
            qh = jnp.where(keep, qn, 0.0).astype(BF16)
            s = lax.dot_general(qh, mk_ref[:, cols], _NT, preferred_element_type=F32)
            p = jnp.exp(s - jnp.max(s, axis=-1, keepdims=True))
            probs.append((p.astype(BF16), jnp.sum(p, axis=-1, keepdims=True)))
    return probs


def _memory_output(probs, mv_ref, mo_ref, rows=slice(None)):
    for t in range(mo_ref.shape[1] // V7X_LANES):
        cols = slice(t * V7X_LANES, (t + 1) * V7X_LANES)
        outs = [_dot(p, mv_ref[:, cols]) / denom for p, denom in probs[2 * t:2 * t + 2]]
        lo = lax.broadcasted_iota(jnp.int32, outs[0].shape, 1) < MEM_HEAD_DIM
        mo_ref[rows, cols] = jnp.where(lo, outs[0], outs[1]).astype(mo_ref.dtype)


def _mem_kv_kernel(mem_ref, gain_ref, w_ref, kgain_ref, mk_ref, mv_ref):
    mw = mk_ref.shape[1]
    mn = _rms(mem_ref[...], gain_ref[...]).astype(BF16)
    kv = _dot(mn, w_ref[...])
    for t in range(mw // V7X_LANES):
        cols = slice(t * V7X_LANES, (t + 1) * V7X_LANES)
        mk_ref[:, cols] = _rms_head_pairs(kv[:, cols], kgain_ref[...]).astype(mk_ref.dtype)
    mv_ref[...] = kv[:, mw:].astype(mv_ref.dtype)


def _mem_kv(mem, mem_norm, w_mem_kv, mem_knorm):
    bn, mt, dm = mem.shape
    depth = w_mem_kv.shape[0]
    mw = w_mem_kv.shape[2] // 2
    kgain = jnp.concatenate([mem_knorm, mem_knorm], axis=-1).reshape(depth, 1, V7X_LANES)
    out = jax.ShapeDtypeStruct((depth, bn, mt, mw), BF16)
    return pl.pallas_call(
        _mem_kv_kernel,
        out_shape=(out, out),
        grid=(depth, bn),
        in_specs=[
            pl.BlockSpec((None, mt, dm), lambda l, b: (b, 0, 0)),
            pl.BlockSpec((None, 1, dm), lambda l, b: (l, 0, 0)),
            pl.BlockSpec((None, dm, 2 * mw), lambda l, b: (l, 0, 0)),
            pl.BlockSpec((None, 1, V7X_LANES), lambda l, b: (l, 0, 0)),
        ],
        out_specs=(
            pl.BlockSpec((None, None, mt, mw), lambda l, b: (l, b, 0, 0)),
            pl.BlockSpec((None, None, mt, mw), lambda l, b: (l, b, 0, 0)),
        ),
        name="mem_kv",
    )(mem, mem_norm.reshape(depth, 1, dm), w_mem_kv, kgain)


def _inproj_a_kernel(x_ref, gain_ref, w_ref, lbl_ref, mqg_ref, mk_ref, mv_ref,
                     qs_ref, lf_ref, k_ref, v_ref, gate_ref, mo_ref, *, layer):
    aw = qs_ref.shape[1]
    lg = lbl_ref[...]
    e = jnp.exp(lg - jnp.max(lg, axis=0, keepdims=True))
    lb = jnp.sum(e[:layer + 1], axis=0, keepdims=True) / jnp.sum(e, axis=0, keepdims=True)
    for row0 in range(0, x_ref.shape[0], HGRN_IN_SUB):
        rows = slice(row0, row0 + HGRN_IN_SUB)
        xn = _rms(x_ref[rows, :], gain_ref[...]).astype(BF16)
        probs = _memory_probs(_dot(xn, w_ref[:, 4 * aw:]), mqg_ref[...], mk_ref)
        qs_ref[rows, :] = _silu(_dot(xn, w_ref[:, 0:aw])).astype(qs_ref.dtype)
        _memory_output(probs, mv_ref, mo_ref, rows)
        f = lb + (1.0 - lb) * jax.nn.sigmoid(_dot(xn, w_ref[:, aw:2 * aw]))
        lf_ref[rows, :] = jnp.log(f) * LOG2E
        k_ref[rows, :] = (1.0 - f).astype(k_ref.dtype)
        v_ref[rows, :] = _dot(xn, w_ref[:, 2 * aw:3 * aw]).astype(v_ref.dtype)
        gate_ref[rows, :] = _silu(_dot(xn, w_ref[:, 3 * aw:4 * aw])).astype(gate_ref.dtype)


def _inproj_a(h, gain, w, lb_logits, mq_gain, mk, mv, layer, seq):
    t, dm = h.shape
    mt, mw = mk.shape[2], mk.shape[3]
    aw = (w.shape[1] - mw) // 4
    tile = HGRN_IN_TILE
    tiles_per_seq = seq // tile
    row = lambda i: (i, 0)
    mem = lambda i: (layer, i // tiles_per_seq, 0, 0)
    wide = functools.partial(jax.ShapeDtypeStruct, (t, aw))
    pipelined = (_nbytes((tile, dm), F32) + _nbytes((tile, aw), F32)
                 + 4 * _nbytes((tile, aw), BF16) + _nbytes((tile, mw), BF16)
                 + 2 * _nbytes((mt, mw), BF16))
    return pl.pallas_call(
        functools.partial(_inproj_a_kernel, layer=layer),
        out_shape=(wide(BF16), wide(F32), wide(BF16), wide(BF16), wide(BF16),
                   jax.ShapeDtypeStruct((t, mw), BF16)),
        grid=(t // tile,),
        in_specs=[
            pl.BlockSpec((tile, dm), row),
            _resident((1, dm)),
            _resident(w.shape),
            _resident(lb_logits.shape),
            _resident((1, V7X_LANES)),
            pl.BlockSpec((None, None, mt, mw), mem),
            pl.BlockSpec((None, None, mt, mw), mem),
        ],
        out_specs=tuple([pl.BlockSpec((tile, aw), row)] * 5 + [pl.BlockSpec((tile, mw), row)]),
        compiler_params=pltpu.CompilerParams(
            dimension_semantics=("parallel",),
            vmem_limit_bytes=_vmem_limit(pipelined, _nbytes(w.shape, BF16),
                                         4 * _nbytes((HGRN_IN_SUB, aw), F32))),
        name="inproj_a",
    )(h, gain, w, lb_logits, mq_gain, mk, mv)


def _hgrn2_kernel(on_ref, qs_ref, lf_ref, k_ref, v_ref, gate_ref, o_ref, state_ref):
    @pl.when(pl.program_id(2) == 0)
    def _():
        state_ref[...] = jnp.zeros_like(state_ref)

    row = lax.broadcasted_iota(jnp.int32, (CHUNK, HEAD_DIM), 0)
    causal = (lax.broadcasted_iota(jnp.int32, (CHUNK, CHUNK), 0)
              >= lax.broadcasted_iota(jnp.int32, (CHUNK, CHUNK), 1))
    onorm = on_ref[...]

    chunks = [slice(c * CHUNK, (c + 1) * CHUNK) for c in range(qs_ref.shape[0] // CHUNK)]
    q_ins, vs, decays, atts, kvs = [], [], [], [], []
    for rows in chunks:
        b = lf_ref[rows, :]
        shift = 1
        while shift < CHUNK:
            b = b + jnp.where(row >= shift, pltpu.roll(b, shift, axis=0), 0.0)
            shift *= 2
        b_end = b[CHUNK - 1:CHUNK, :]
        k = k_ref[rows, :].astype(F32)
        q_in = (qs_ref[rows, :].astype(F32) * jnp.exp2(b)).astype(BF16)
        k_in = (k * jnp.exp2(-b)).astype(BF16)
        k_out = (k * jnp.exp2(b_end - b)).astype(BF16)
        v = v_ref[rows, :]
        q_ins.append(q_in)
        vs.append(v)
        decays.append(jnp.exp2(b_end))
        atts.append(lax.dot_general(q_in, k_in, _NT, preferred_element_type=F32))
        kvs.append(lax.dot_general(v, k_out, _TN, preferred_element_type=F32))

    state_t = state_ref[...]
    states = []
    for decay, kv in zip(decays, kvs):
        states.append(state_t.astype(BF16))
        state_t = state_t * decay + kv
    state_ref[...] = state_t

    for rows, q_in, v, att, state_in in zip(chunks, q_ins, vs, atts, states):
        att = jnp.where(causal, att, 0.0).astype(BF16)
        o = _dot(att, v) + lax.dot_general(q_in, state_in, _NT, preferred_element_type=F32)
        o_ref[rows, :] = (_rms(o, onorm) * gate_ref[rows, :].astype(F32)).astype(o_ref.dtype)


def _hgrn2(qs, lf, k, v, gate, onorm, bn, seq):
    t, aw = qs.shape
    heads = aw // HEAD_DIM
    step_rows = min(HGRN_ROWS, seq)
    steps = seq // step_rows
    spec = pl.BlockSpec((step_rows, HEAD_DIM), lambda b, h, s: (b * steps + s, h))
    return pl.pallas_call(
        _hgrn2_kernel,
        out_shape=jax.ShapeDtypeStruct((t, aw), BF16),
        grid=(bn, heads, steps),
        in_specs=[pl.BlockSpec((1, HEAD_DIM), lambda b, h, s: (0, h)), spec, spec, spec, spec, spec],
        out_specs=spec,
        scratch_shapes=[pltpu.VMEM((HEAD_DIM, HEAD_DIM), F32)],
        compiler_params=pltpu.CompilerParams(
            dimension_semantics=("parallel", "parallel", "arbitrary")),
        name="hgrn2",
    )(onorm, qs, lf, k, v, gate)


def _mix_ffn_kernel(h_ref, o_ref, mo_ref, wo_ref, nf_ref, wgu_ref, wd_ref, h_out):
    main_w = o_ref.shape[1]
    hidden = wd_ref.shape[0]
    h = h_ref[...] + _dot(o_ref[...], wo_ref[0:main_w, :]) + _dot(mo_ref[...], wo_ref[main_w:, :])
    hn = _rms(h, nf_ref[...]).astype(BF16)
    act = (_silu(_dot(hn, wgu_ref[:, 0:hidden])) * _dot(hn, wgu_ref[:, hidden:])).astype(BF16)
    h_out[...] = h + _dot(act, wd_ref[...])


def _layer_resident(stacked, layer):
    zeros = (0,) * (stacked.ndim - 1)
    return pl.BlockSpec((None,) + stacked.shape[1:], lambda *_: (layer,) + zeros,
                        pipeline_mode=pl.Buffered(1))


def _mix_ffn(h, o, mo, w_out, norm_ffn, w_gate_up, w_down, layer):
    t, dm = h.shape
    main_w, mw = o.shape[1], mo.shape[1]
    row = lambda i: (i, 0)
    resident = sum(_nbytes(w.shape[1:], BF16) for w in (w_out, w_gate_up, w_down))
    tile = FFN_TILE
    pipelined = 2 * _nbytes((tile, dm), F32) + _nbytes((tile, main_w + mw), BF16)
    temps = 3 * _nbytes((tile, dm), F32) + 3 * _nbytes((tile, w_down.shape[1]), F32)
    return pl.pallas_call(
        _mix_ffn_kernel,
        out_shape=jax.ShapeDtypeStruct((t, dm), F32),
        grid=(t // tile,),
        in_specs=[
            pl.BlockSpec((tile, dm), row), pl.BlockSpec((tile, main_w), row),
            pl.BlockSpec((tile, mw), row),
            _layer_resident(w_out, layer), _resident((1, dm)), _layer_resident(w_gate_up, layer),
            _layer_resident(w_down, layer),
        ],
        out_specs=pl.BlockSpec((tile, dm), row),
        compiler_params=pltpu.CompilerParams(
            dimension_semantics=("parallel",),
            vmem_limit_bytes=_vmem_limit(pipelined, resident, temps)),
        name="mix_ffn",
    )(h, o, mo, w_out, norm_ffn, w_gate_up, w_down)


def _inproj_b_kernel(*refs, dilations):
    n_groups = len(dilations)
    (x_ref, gain_ref, kvg_ref, w_ref, wkv_ref, qn_ref, kn_ref, cos_ref, sin_ref,
     mqg_ref, mk_ref, mv_ref) = refs[:12]
    q_refs = refs[12:12 + n_groups]
    k_refs = refs[12 + n_groups:12 + 2 * n_groups]
    v_refs = refs[12 + 2 * n_groups:12 + 3 * n_groups]
    mo_ref = refs[12 + 3 * n_groups]
    scratch = refs[13 + 3 * n_groups:]
    q_slabs = dict(zip([d for d in dilations if d != 1], scratch[:-5]))
    k_slab, v_slab, q_tmp, k_tmp, v_tmp = scratch[-5:]
    kv_w = wkv_ref.shape[1] // 2
    bw = (w_ref.shape[1] - mo_ref.shape[1]) // n_groups
    heads = range(bw // HEAD_DIM)
    pair = 2
    q_scale = (HEAD_DIM ** -0.5) * LOG2E

    for row0 in range(0, x_ref.shape[0], ATTN_IN_SUB):
        rows = slice(row0, row0 + ATTN_IN_SUB)
        x = x_ref[rows, :]
        xhat = x * lax.rsqrt(jnp.mean(x * x, axis=-1, keepdims=True) + EPS)
        xn = (xhat * gain_ref[...]).astype(BF16)
        kn = (xhat * kvg_ref[...]).astype(BF16)
        cos, sin = cos_ref[rows, :], sin_ref[rows, :]
        cos_s, sin_s = cos * q_scale, sin * q_scale

        def queries(gi, row0=row0, rows=rows, xn=xn, cos_s=cos_s, sin_s=sin_s):
            dil, q_ref = dilations[gi], q_refs[gi]
            for hd in heads:
                if hd % pair == 0:
                    col0 = gi * bw + hd * HEAD_DIM
                    qs = _dot(xn, w_ref[:, col0:col0 + pair * HEAD_DIM])
                q = _rope(_rms(qs[:, _head_cols(hd % pair)], qn_ref[gi]), cos_s, sin_s)
                if dil == 1:
                    q_ref[rows, _head_cols(hd)] = q.astype(q_ref.dtype)
                else:
                    q_slabs[dil][hd] = q
            if dil != 1:
                _store_by_residue(q_slabs[dil], q_tmp, [(dil, q_ref)], row0)

        def copies(slab, tmp, out_refs, row0=row0, rows=rows):
            for dil, out_ref in zip(dilations, out_refs):
                if dil == 1:
                    for hd in heads:
                        out_ref[rows, _head_cols(hd)] = slab[hd].astype(out_ref.dtype)
            _store_by_residue(slab, tmp, [(d, ref) for d, ref in zip(dilations, out_refs) if d != 1], row0)

        order = sorted(range(n_groups), key=lambda gi: -dilations[gi])
        queries(order[0])
        probs = _memory_probs(_dot(xn, w_ref[:, n_groups * bw:]), mqg_ref[...], mk_ref)
        for hd in heads:
            if hd % pair == 0:
                k = _dot(kn, wkv_ref[:, hd * HEAD_DIM:(hd + pair) * HEAD_DIM])
            k_slab[hd] = _rope(_rms(k[:, _head_cols(hd % pair)], kn_ref[...]), cos, sin)
        copies(k_slab, k_tmp, k_refs)
        for gi in order[1:-1]:
            queries(gi)
        _memory_output(probs, mv_ref, mo_ref, rows)
        for hd in heads:
            if hd % pair == 0:
                v = _dot(kn, wkv_ref[:, kv_w + hd * HEAD_DIM:kv_w + (hd + pair) * HEAD_DIM])
            v_slab[hd] = v[:, _head_cols(hd % pair)]
        copies(v_slab, v_tmp, v_refs)
        queries(order[-1])


def _residue_out(bn, seq, width, dil):
    tiles_per_seq = seq // ATTN_IN_TILE
    if dil == 1:
        return (jax.ShapeDtypeStruct((bn * seq, width), BF16),
                pl.BlockSpec((ATTN_IN_TILE, width), lambda i: (i, 0)))
    return (jax.ShapeDtypeStruct((bn, dil, seq // dil, width), BF16),
            pl.BlockSpec((None, dil, ATTN_IN_TILE // dil, width),
                         lambda i: (i // tiles_per_seq, 0, i % tiles_per_seq, 0)))


def _inproj_b(h, gain, kv_gain, w, w_kv, q_norm, k_norm, cos2, sin2, mq_gain, mk, mv, layer, bn, seq):
    t, dm = h.shape
    mt, mw = mk.shape[2], mk.shape[3]
    dilations = tuple(d for _, d in DILATED_GROUPS)
    n_groups = len(dilations)
    bw = (w.shape[1] - mw) // n_groups
    kv_w = w_kv.shape[1] // 2
    assert kv_w == bw
    tile = ATTN_IN_TILE
    tiles_per_seq = seq // tile
    row = lambda i: (i, 0)
    pos = lambda i: (i % tiles_per_seq, 0)
    mem = lambda i: (layer, i // tiles_per_seq, 0, 0)
    copies = [_residue_out(bn, seq, bw, dil) for dil in dilations] * 3
    slab = (bw // HEAD_DIM, ATTN_IN_SUB, HEAD_DIM)
    n_slabs = sum(1 for dil in dilations if dil != 1) + 5
    pipelined = (_nbytes((tile, dm), F32) + 3 * n_groups * _nbytes((tile, bw), BF16)
                 + 2 * _nbytes((tile, HEAD_DIM), F32) + _nbytes((tile, mw), BF16)
                 + 2 * _nbytes((mt, mw), BF16))
    resident = _nbytes(w.shape, BF16) + _nbytes(w_kv.shape, BF16) + n_slabs * _nbytes(slab, F32)
    return pl.pallas_call(
        functools.partial(_inproj_b_kernel, dilations=dilations),
        out_shape=tuple([shape for shape, _ in copies] + [jax.ShapeDtypeStruct((t, mw), BF16)]),
        grid=(t // tile,),
        in_specs=[
            pl.BlockSpec((tile, dm), row),
            _resident((1, dm)), _resident((1, dm)),
            _resident(w.shape), _resident(w_kv.shape),
            _resident((n_groups, 1, HEAD_DIM)), _resident((1, HEAD_DIM)),
            pl.BlockSpec((tile, HEAD_DIM), pos), pl.BlockSpec((tile, HEAD_DIM), pos),
            _resident((1, V7X_LANES)),
            pl.BlockSpec((None, None, mt, mw), mem),
            pl.BlockSpec((None, None, mt, mw), mem),
        ],
        out_specs=tuple([spec for _, spec in copies] + [pl.BlockSpec((tile, mw), row)]),
        scratch_shapes=[pltpu.VMEM(slab, F32)] * n_slabs,
        compiler_params=pltpu.CompilerParams(
            dimension_semantics=("parallel",),
            vmem_limit_bytes=_vmem_limit(pipelined, resident, 6 * _nbytes((ATTN_IN_SUB, bw), F32))),
        name="inproj_b",
    )(h, gain, kv_gain, w, w_kv, q_norm.reshape(n_groups, 1, HEAD_DIM), k_norm, cos2, sin2,
      mq_gain, mk, mv)


def _dilated_kernel(*refs, dilations):
    n_groups = len(dilations)
    q_refs = refs[:n_groups]
    k_refs = refs[n_groups:2 * n_groups]
    v_refs = refs[2 * n_groups:3 * n_groups]
    o_ref, og_ref, lse_ref, bias_ref = refs[3 * n_groups:]
    seq = o_ref.shape[0]
    blk = ATTN_BLOCK

    qi = lax.broadcasted_iota(jnp.int32, (blk, 2 * blk), 0)
    kj = lax.broadcasted_iota(jnp.int32, (blk, 2 * blk), 1)
    band = (kj >= qi) & (kj <= qi + blk)
    bias_ref[0] = jnp.where(band & (kj >= blk), 0.0, -jnp.inf)
    bias_ref[1] = jnp.where(band, 0.0, -jnp.inf)
    bias_ref[2] = jnp.where(kj <= qi, 0.0, -jnp.inf)

    parked = [gi for gi, dil in enumerate(dilations) if dil != 1]
    (last,) = [gi for gi, dil in enumerate(dilations) if dil == 1]
    for gi in parked + [last]:
        dil = dilations[gi]
        q_ref, k_ref, v_ref = q_refs[gi], k_refs[gi], v_refs[gi]
        n_blocks = seq // (blk * dil)

        def blocks(step, carry, q_ref=q_ref, k_ref=k_ref, v_ref=v_ref, gi=gi, dil=dil, n_blocks=n_blocks):
            idxs = [step * ATTN_UNROLL + u for u in range(ATTN_UNROLL)]
            starts = [pl.multiple_of(idx * blk, blk) for idx in idxs]
            windows = [pl.ds(pl.multiple_of(jnp.maximum(start - blk, 0), blk), 2 * blk) for start in starts]
            scores = [lax.dot_general(q_ref[pl.ds(start, blk), :], k_ref[window, :], _NT,
                                      preferred_element_type=F32)
                      for start, window in zip(starts, windows)]
            soft = []
            for idx, s in zip(idxs, scores):
                n = lax.rem(idx, n_blocks)
                s = s + bias_ref[jnp.where(idx == 0, 2, jnp.minimum(n, 1))]
                m = jnp.max(s, axis=-1, keepdims=True)
                p = jnp.exp2(s - m)
                denom = jnp.sum(p, axis=-1, keepdims=True)
                soft.append((p.astype(BF16), m, denom))
            for idx, start, window, (p, m, denom) in zip(idxs, starts, windows, soft):
                o = _dot(p, v_ref[window, :]) / denom
                lse2 = jnp.broadcast_to(m + jnp.log(denom) * LOG2E, (blk, HEAD_DIM))
                if dil != 1:
                    n = lax.rem(idx, n_blocks)
                    out_rows = pl.ds(n * (blk * dil) + lax.div(idx, n_blocks), blk, stride=dil)
                    slot = parked.index(gi)
                    og_ref[slot, out_rows, :] = o
                    lse_ref[slot, out_rows, :] = lse2
                else:
                    rows = pl.ds(start, blk)
                    lses = [lse2] + [lse_ref[slot, rows, :] for slot in range(len(parked))]
                    outs = [o] + [og_ref[slot, rows, :] for slot in range(len(parked))]
                    top = functools.reduce(jnp.maximum, lses)
                    ws = [jnp.exp2(l - top) for l in lses]
                    acc = sum(w * og for w, og in zip(ws, outs))
                    o_ref[rows, :] = (acc / sum(ws)).astype(o_ref.dtype)
            return carry

        lax.fori_loop(0, dil * n_blocks // ATTN_UNROLL, blocks, 0)


def _dilated_attention(qs, ks, vs, bn, seq):
    width = qs[0].shape[-1]
    heads = width // HEAD_DIM
    dilations = tuple(d for _, d in DILATED_GROUPS)
    n_groups = len(dilations)
    as_seq = lambda a: a.reshape(bn, seq, width)
    spec = pl.BlockSpec((None, seq, HEAD_DIM), lambda b, h: (b, 0, h))
    seq_bf16 = _nbytes((seq, HEAD_DIM), BF16)
    seq_f32 = _nbytes((seq, HEAD_DIM), F32)
    n_bias = 3
    scratch = 2 * (n_groups - 1) * seq_f32 + _nbytes((n_bias, ATTN_BLOCK, 2 * ATTN_BLOCK), F32)
    staged = ATTN_UNROLL * (_nbytes((ATTN_BLOCK, 2 * ATTN_BLOCK), F32) + _nbytes((ATTN_BLOCK, 2 * ATTN_BLOCK), BF16)
                            + 2 * _nbytes((ATTN_BLOCK, HEAD_DIM), F32))
    out = pl.pallas_call(
        functools.partial(_dilated_kernel, dilations=dilations),
        out_shape=jax.ShapeDtypeStruct((bn, seq, width), BF16),
        grid=(bn, heads),
        in_specs=[spec] * (3 * n_groups),
        out_specs=spec,
        scratch_shapes=[
            pltpu.VMEM((n_groups - 1, seq, HEAD_DIM), F32),
            pltpu.VMEM((n_groups - 1, seq, HEAD_DIM), F32),
            pltpu.VMEM((n_bias, ATTN_BLOCK, 2 * ATTN_BLOCK), F32),
        ],
        compiler_params=pltpu.CompilerParams(
            dimension_semantics=("parallel", "parallel"),
            vmem_limit_bytes=_vmem_limit((3 * n_groups + 1) * seq_bf16, scratch, staged)),
        name="dilated_attention",
    )(*[as_seq(a) for a in (*qs, *ks, *vs)])
    return out.reshape(bn * seq, width)


def _rope_tables(seq):
    half = HEAD_DIM // 2
    inv = ROPE_THETA ** (-jnp.arange(half, dtype=F32) / half)
    ang = jnp.arange(seq).astype(F32)[:, None] * inv[None, :]
    cos, sin = jnp.cos(ang), jnp.sin(ang)
    return jnp.concatenate([cos, cos], axis=-1), jnp.concatenate([-sin, sin], axis=-1)


def kernel(x, mem, norm_mix, norm_ffn, a_w_in, a_lb_logits, a_onorm, b_w_in, b_qnorm, kv_norm, w_kv,
           b_knorm, mem_norm, w_mem_kv, mem_qnorm, mem_knorm, w_out, w_gate_up, w_down):
    bn, seq, dm = x.shape
    depth = norm_mix.shape[0]
    n_a = a_w_in.shape[0]
    max_dil = max(d for _, d in DILATED_GROUPS)
    assert all(w == ATTN_BLOCK * d for w, d in DILATED_GROUPS)
    assert seq % (ATTN_BLOCK * max_dil) == 0 and seq % (ATTN_BLOCK * ATTN_UNROLL) == 0
    assert seq % min(HGRN_ROWS, seq) == 0 and HGRN_ROWS % CHUNK == 0 and HGRN_IN_SUB % CHUNK == 0
    assert seq % HGRN_IN_TILE == 0 and HGRN_IN_TILE % HGRN_IN_SUB == 0 and seq % FFN_TILE == 0
    assert seq % ATTN_IN_TILE == 0 and ATTN_IN_TILE % ATTN_IN_SUB == 0 and ATTN_IN_SUB % max_dil == 0
    assert depth - n_a == 1

    bf = lambda a: a.astype(BF16)
    row_vec = lambda a: a.reshape(1, -1)
    pair = lambda a: jnp.concatenate([a, a], axis=-1).reshape(1, V7X_LANES)
    cos2, sin2 = _rope_tables(seq)
    n_groups = len(DILATED_GROUPS)

    w_out_bf, w_gate_up_bf, w_down_bf = bf(w_out), bf(w_gate_up), bf(w_down)
    mk, mv = _mem_kv(mem, mem_norm, bf(w_mem_kv), mem_knorm)
    h = x.reshape(bn * seq, dm)
    for l in range(depth):
        gain = row_vec(norm_mix[l])
        if l < n_a:
            qs, lf, k, v, gate, mo = _inproj_a(h, gain, bf(a_w_in[l]), a_lb_logits, pair(mem_qnorm[l]),
                                               mk, mv, l, seq)
            o = _hgrn2(qs, lf, k, v, gate, row_vec(a_onorm[l]), bn, seq)
        else:
            j = l - n_a
            *copies, mo = _inproj_b(h, gain, row_vec(kv_norm), bf(b_w_in[j]), bf(w_kv), b_qnorm[j],
                                    row_vec(b_knorm), cos2, sin2, pair(mem_qnorm[l]), mk, mv, l, bn, seq)
            q_groups, ks, vs = (copies[i * n_groups:(i + 1) * n_groups] for i in range(3))
            o = _dilated_attention(q_groups, ks, vs, bn, seq)
        h = _mix_ffn(h, o, mo, w_out_bf, row_vec(norm_ffn[l]), w_gate_up_bf, w_down_bf, l)
    return h.reshape(bn, seq, dm)
```

```python
import functools
import math

import jax
import jax.numpy as jnp
from jax import lax
from jax.experimental import pallas as pl
from jax.experimental.pallas import tpu as pltpu

F32 = jnp.float32
BF16 = jnp.bfloat16

EPS = 1e-6
HEAD_DIM = 128
CHUNK = 64
MEM_HEAD_DIM = 64
DILATED_GROUPS = ((128, 1), (512, 4), (2048, 16))
ROPE_THETA = 10000.0
LOG2E = math.log2(math.e)

V7X_LANES = 128
V7X_VMEM_SCOPED_DEFAULT_BYTES = 16 * 1024 * 1024
V7X_VMEM_SCOPED_MAX_BYTES = 60000 * 1024
V7X_SINGLE_LOAD_STRIDE = 4

HGRN_IN_TILE = 512
HGRN_IN_SUB = 512
ATTN_IN_TILE = 512
ATTN_IN_SUB = 256
FFN_TILE = 512
HGRN_ROWS = 4096
ATTN_BLOCK = 128
ATTN_UNROLL = 16

_NT = (((1,), (1,)), ((), ()))
_TN = (((0,), (0,)), ((), ()))


def _vmem_limit(pipelined_bytes, resident_bytes, temp_bytes):
    need = 2 * pipelined_bytes + resident_bytes + temp_bytes
    return int(min(max(need, V7X_VMEM_SCOPED_DEFAULT_BYTES), V7X_VMEM_SCOPED_MAX_BYTES))


def _nbytes(shape, dtype):
    n = 1
    for s in shape:
        n *= s
    return n * jnp.dtype(dtype).itemsize


def _resident(shape):
    zeros = (0,) * len(shape)
    return pl.BlockSpec(shape, lambda *_: zeros, pipeline_mode=pl.Buffered(1))


def _dot(a, b):
    return jnp.dot(a, b, preferred_element_type=F32)


def _rms(x, gain):
    ms = jnp.mean(x * x, axis=-1, keepdims=True)
    return x * lax.rsqrt(ms + EPS) * gain


def _silu(x):
    return x * jax.nn.sigmoid(x)


def _head_cols(hd):
    return slice(hd * HEAD_DIM, (hd + 1) * HEAD_DIM)


def _rms_head_pairs(x, gain):
    lo = lax.broadcasted_iota(jnp.int32, x.shape, 1) < MEM_HEAD_DIM
    x2 = x * x
    s_lo = jnp.sum(jnp.where(lo, x2, 0.0), axis=-1, keepdims=True)
    s_hi = jnp.sum(jnp.where(lo, 0.0, x2), axis=-1, keepdims=True)
    ms = jnp.where(lo, s_lo, s_hi) * (1.0 / MEM_HEAD_DIM)
    return x * lax.rsqrt(ms + EPS) * gain


def _rope(x, cos2, sin2):
    return x * cos2 + pltpu.roll(x, HEAD_DIM // 2, axis=1) * sin2


def _store_by_residue(slab_ref, tmp_ref, dil_outs, row0):
    n_heads, tile_rows = slab_ref.shape[0], slab_ref.shape[1]
    base = V7X_SINGLE_LOAD_STRIDE
    two_hops = any(dil > base for dil, _ in dil_outs)
    if two_hops:
        part = tile_rows // base
        for r in range(base):
            for hd in range(n_heads):
                tmp_ref[hd, r * part:(r + 1) * part, :] = slab_ref[hd, pl.ds(r, part, stride=base), :]
    for dil, out_ref in dil_outs:
        rows = tile_rows // dil
        dst = slice(row0 // dil, row0 // dil + rows)
        for r in range(dil):
            for hd in range(n_heads):
                if dil < base or (dil == base and not two_hops):
                    piece = slab_ref[hd, pl.ds(r, rows, stride=dil), :]
                elif dil == base:
                    piece = tmp_ref[hd, r * rows:(r + 1) * rows, :]
                else:
                    assert dil % base == 0 and dil // base <= base
                    piece = tmp_ref[hd, pl.ds((r % base) * part + r // base, rows, stride=dil // base), :]
                out_ref[r, dst, _head_cols(hd)] = piece.astype(out_ref.dtype)


def _memory_probs(mq, qgain, mk_ref):
    scaled_gain = qgain * (MEM_HEAD_DIM ** -0.5)
    probs = []
    for t in range(mq.shape[1] // V7X_LANES):
        cols = slice(t * V7X_LANES, (t + 1) * V7X_LANES)
        qn = _rms_head_pairs(mq[:, cols], scaled_gain)
        lo = lax.broadcasted_iota(jnp.int32, qn.shape, 1) < MEM_HEAD_DIM
        for keep in (lo, jnp.logical_not(lo)):
            qh = jnp.where(keep, qn, 0.0).astype(BF16)
            s = lax.dot_general(qh, mk_ref[:, cols], _NT, preferred_element_type=F32)
            p = jnp.exp(s - jnp.max(s, axis=-1, keepdims=True))
            probs.append((p.astype(BF16), jnp.sum(p, axis=-1, keepdims=True)))
    return probs


def _memory_output(probs, mv_ref, mo_ref, rows=slice(None)):
    for t in range(mo_ref.shape[1] // V7X_LANES):
        cols = slice(t * V7X_LANES, (t + 1) * V7X_LANES)
        outs = [_dot(p, mv_ref[:, cols]) / denom for p, denom in probs[2 * t:2 * t + 2]]
        lo = lax.broadcasted_iota(jnp.int32, outs[0].shape, 1) < MEM_HEAD_DIM
        mo_ref[rows, cols] = jnp.where(lo, outs[0], outs[1]).astype(mo_ref.dtype)


def _mem_kv_kernel(mem_ref, gain_ref, w_ref, kgain_ref, mk_ref, mv_ref):
    mw = mk_ref.shape[1]
    mn = _rms(mem_ref[...], gain_ref[...]).astype(BF16)
    kv = _dot(mn, w_ref[...])
    for t in range(mw // V7X_LANES):
        cols = slice(t * V7X_LANES, (t + 1) * V7X_LANES)
        mk_ref[:, cols] = _rms_head_pairs(kv[:, cols], kgain_ref[...]).astype(mk_ref.dtype)
    mv_ref[...] = kv[:, mw:].astype(mv_ref.dtype)


def _mem_kv(mem, mem_norm, w_mem_kv, mem_knorm):
    bn, mt, dm = mem.shape
    depth = w_mem_kv.shape[0]
    mw = w_mem_kv.shape[2] // 2
    kgain = jnp.concatenate([mem_knorm, mem_knorm], axis=-1).reshape(depth, 1, V7X_LANES)
    out = jax.ShapeDtypeStruct((depth, bn, mt, mw), BF16)
    return pl.pallas_call(
        _mem_kv_kernel,
        out_shape=(out, out),
        grid=(depth, bn),
        in_specs=[
            pl.BlockSpec((None, mt, dm), lambda l, b: (b, 0, 0)),
            pl.BlockSpec((None, 1, dm), lambda l, b: (l, 0, 0)),
            pl.BlockSpec((None, dm, 2 * mw), lambda l, b: (l, 0, 0)),
            pl.BlockSpec((None, 1, V7X_LANES), lambda l, b: (l, 0, 0)),
        ],
        out_specs=(
            pl.BlockSpec((None, None, mt, mw), lambda l, b: (l, b, 0, 0)),
            pl.BlockSpec((None, None, mt, mw), lambda l, b: (l, b, 0, 0)),
        ),
        name="mem_kv",
    )(mem, mem_norm.reshape(depth, 1, dm), w_mem_kv, kgain)


def _inproj_a_kernel(x_ref, gain_ref, w_ref, lbl_ref, mqg_ref, mk_ref, mv_ref,
                     qs_ref, lf_ref, k_ref, v_ref, gate_ref, mo_ref, *, layer):
    aw = qs_ref.shape[1]
    lg = lbl_ref[...]
    e = jnp.exp(lg - jnp.max(lg, axis=0, keepdims=True))
    lb = jnp.sum(e[:layer + 1], axis=0, keepdims=True) / jnp.sum(e, axis=0, keepdims=True)
    for row0 in range(0, x_ref.shape[0], HGRN_IN_SUB):
        rows = slice(row0, row0 + HGRN_IN_SUB)
        xn = _rms(x_ref[rows, :], gain_ref[...]).astype(BF16)
        probs = _memory_probs(_dot(xn, w_ref[:, 4 * aw:]), mqg_ref[...], mk_ref)
        qs_ref[rows, :] = _silu(_dot(xn, w_ref[:, 0:aw])).astype(qs_ref.dtype)
        _memory_output(probs, mv_ref, mo_ref, rows)
        f = lb + (1.0 - lb) * jax.nn.sigmoid(_dot(xn, w_ref[:, aw:2 * aw]))
        lf_ref[rows, :] = jnp.log(f) * LOG2E
        k_ref[rows, :] = (1.0 - f).astype(k_ref.dtype)
        v_ref[rows, :] = _dot(xn, w_ref[:, 2 * aw:3 * aw]).astype(v_ref.dtype)
        gate_ref[rows, :] = _silu(_dot(xn, w_ref[:, 3 * aw:4 * aw])).astype(gate_ref.dtype)


def _inproj_a(h, gain, w, lb_logits, mq_gain, mk, mv, layer, seq):
    t, dm = h.shape
    mt, mw = mk.shape[2], mk.shape[3]
    aw = (w.shape[1] - mw) // 4
    tile = HGRN_IN_TILE
    tiles_per_seq = seq // tile
    row = lambda i: (i, 0)
    mem = lambda i: (layer, i // tiles_per_seq, 0, 0)
    wide = functools.partial(jax.ShapeDtypeStruct, (t, aw))
    pipelined = (_nbytes((tile, dm), F32) + _nbytes((tile, aw), F32)
                 + 4 * _nbytes((tile, aw), BF16) + _nbytes((tile, mw), BF16)
                 + 2 * _nbytes((mt, mw), BF16))
    return pl.pallas_call(
        functools.partial(_inproj_a_kernel, layer=layer),
        out_shape=(wide(BF16), wide(F32), wide(BF16), wide(BF16), wide(BF16),
                   jax.ShapeDtypeStruct((t, mw), BF16)),
        grid=(t // tile,),
        in_specs=[
            pl.BlockSpec((tile, dm), row),
            _resident((1, dm)),
            _resident(w.shape),
            _resident(lb_logits.shape),
            _resident((1, V7X_LANES)),
            pl.BlockSpec((None, None, mt, mw), mem),
            pl.BlockSpec((None, None, mt, mw), mem),
        ],
        out_specs=tuple([pl.BlockSpec((tile, aw), row)] * 5 + [pl.BlockSpec((tile, mw), row)]),
        compiler_params=pltpu.CompilerParams(
            dimension_semantics=("parallel",),
            vmem_limit_bytes=_vmem_limit(pipelined, _nbytes(w.shape, BF16),
                                         4 * _nbytes((HGRN_IN_SUB, aw), F32))),
        name="inproj_a",
    )(h, gain, w, lb_logits, mq_gain, mk, mv)


def _hgrn2_kernel(on_ref, qs_ref, lf_ref, k_ref, v_ref, gate_ref, o_ref, state_ref):
    @pl.when(pl.program_id(2) == 0)
    def _():
        state_ref[...] = jnp.zeros_like(state_ref)

    row = lax.broadcasted_iota(jnp.int32, (CHUNK, HEAD_DIM), 0)
    causal = (lax.broadcasted_iota(jnp.int32, (CHUNK, CHUNK), 0)
              >= lax.broadcasted_iota(jnp.int32, (CHUNK, CHUNK), 1))
    onorm = on_ref[...]

    chunks = [slice(c * CHUNK, (c + 1) * CHUNK) for c in range(qs_ref.shape[0] // CHUNK)]
    q_ins, vs, decays, atts, kvs = [], [], [], [], []
    for rows in chunks:
        b = lf_ref[rows, :]
        shift = 1
        while shift < CHUNK:
            b = b + jnp.where(row >= shift, pltpu.roll(b, shift, axis=0), 0.0)
            shift *= 2
        b_end = b[CHUNK - 1:CHUNK, :]
        k = k_ref[rows, :].astype(F32)
        q_in = (qs_ref[rows, :].astype(F32) * jnp.exp2(b)).astype(BF16)
        k_in = (k * jnp.exp2(-b)).astype(BF16)
        k_out = (k * jnp.exp2(b_end - b)).astype(BF16)
        v = v_ref[rows, :]
        q_ins.append(q_in)
        vs.append(v)
        decays.append(jnp.exp2(b_end))
        atts.append(lax.dot_general(q_in, k_in, _NT, preferred_element_type=F32))
        kvs.append(lax.dot_general(v, k_out, _TN, preferred_element_type=F32))

    state_t = state_ref[...]
    states = []
    for decay, kv in zip(decays, kvs):
        states.append(state_t.astype(BF16))
        state_t = state_t * decay + kv
    state_ref[...] = state_t

    for rows, q_in, v, att, state_in in zip(chunks, q_ins, vs, atts, states):
        att = jnp.where(causal, att, 0.0).astype(BF16)
        o = _dot(att, v) + lax.dot_general(q_in, state_in, _NT, preferred_element_type=F32)
        o_ref[rows, :] = (_rms(o, onorm) * gate_ref[rows, :].astype(F32)).astype(o_ref.dtype)


def _hgrn2(qs, lf, k, v, gate, onorm, bn, seq):
    t, aw = qs.shape
    heads = aw // HEAD_DIM
    step_rows = min(HGRN_ROWS, seq)
    steps = seq // step_rows
    spec = pl.BlockSpec((step_rows, HEAD_DIM), lambda b, h, s: (b * steps + s, h))
    return pl.pallas_call(
        _hgrn2_kernel,
        out_shape=jax.ShapeDtypeStruct((t, aw), BF16),
        grid=(bn, heads, steps),
        in_specs=[pl.BlockSpec((1, HEAD_DIM), lambda b, h, s: (0, h)), spec, spec, spec, spec, spec],
        out_specs=spec,
        scratch_shapes=[pltpu.VMEM((HEAD_DIM, HEAD_DIM), F32)],
        compiler_params=pltpu.CompilerParams(
            dimension_semantics=("parallel", "parallel", "arbitrary")),
        name="hgrn2",
    )(onorm, qs, lf, k, v, gate)


def _mix_ffn_kernel(h_ref, o_ref, mo_ref, wo_ref, nf_ref, wgu_ref, wd_ref, h_out):
    main_w = o_ref.shape[1]
    hidden = wd_ref.shape[0]
    h = h_ref[...] + _dot(o_ref[...], wo_ref[0:main_w, :]) + _dot(mo_ref[...], wo_ref[main_w:, :])
    hn = _rms(h, nf_ref[...]).astype(BF16)
    act = (_silu(_dot(hn, wgu_ref[:, 0:hidden])) * _dot(hn, wgu_ref[:, hidden:])).astype(BF16)
    h_out[...] = h + _dot(act, wd_ref[...])


def _layer_resident(stacked, layer):
    zeros = (0,) * (stacked.ndim - 1)
    return pl.BlockSpec((None,) + stacked.shape[1:], lambda *_: (layer,) + zeros,
                        pipeline_mode=pl.Buffered(1))


def _mix_ffn(h, o, mo, w_out, norm_ffn, w_gate_up, w_down, layer):
    t, dm = h.shape
    main_w, mw = o.shape[1], mo.shape[1]
    row = lambda i: (i, 0)
    resident = sum(_nbytes(w.shape[1:], BF16) for w in (w_out, w_gate_up, w_down))
    tile = FFN_TILE
    pipelined = 2 * _nbytes((tile, dm), F32) + _nbytes((tile, main_w + mw), BF16)
    temps = 3 * _nbytes((tile, dm), F32) + 3 * _nbytes((tile, w_down.shape[1]), F32)
    return pl.pallas_call(
        _mix_ffn_kernel,
        out_shape=jax.ShapeDtypeStruct((t, dm), F32),
        grid=(t // tile,),
        in_specs=[
            pl.BlockSpec((tile, dm), row), pl.BlockSpec((tile, main_w), row),
            pl.BlockSpec((tile, mw), row),
            _layer_resident(w_out, layer), _resident((1, dm)), _layer_resident(w_gate_up, layer),
            _layer_resident(w_down, layer),
        ],
        out_specs=pl.BlockSpec((tile, dm), row),
        compiler_params=pltpu.CompilerParams(
            dimension_semantics=("parallel",),
            vmem_limit_bytes=_vmem_limit(pipelined, resident, temps)),
        name="mix_ffn",
    )(h, o, mo, w_out, norm_ffn, w_gate_up, w_down)


def _inproj_b_kernel(*refs, dilations):
    n_groups = len(dilations)
    (x_ref, gain_ref, kvg_ref, w_ref, wkv_ref, qn_ref, kn_ref, cos_ref, sin_ref,
     mqg_ref, mk_ref, mv_ref) = refs[:12]
    q_refs = refs[12:12 + n_groups]
    k_refs = refs[12 + n_groups:12 + 2 * n_groups]
    v_refs = refs[12 + 2 * n_groups:12 + 3 * n_groups]
    mo_ref = refs[12 + 3 * n_groups]
    scratch = refs[13 + 3 * n_groups:]
    q_slabs = dict(zip([d for d in dilations if d != 1], scratch[:-5]))
    k_slab, v_slab, q_tmp, k_tmp, v_tmp = scratch[-5:]
    kv_w = wkv_ref.shape[1] // 2
    bw = (w_ref.shape[1] - mo_ref.shape[1]) // n_groups
    heads = range(bw // HEAD_DIM)
    pair = 2
    q_scale = (HEAD_DIM ** -0.5) * LOG2E

    for row0 in range(0, x_ref.shape[0], ATTN_IN_SUB):
        rows = slice(row0, row0 + ATTN_IN_SUB)
        x = x_ref[rows, :]
        xhat = x * lax.rsqrt(jnp.mean(x * x, axis=-1, keepdims=True) + EPS)
        xn = (xhat * gain_ref[...]).astype(BF16)
        kn = (xhat * kvg_ref[...]).astype(BF16)
        cos, sin = cos_ref[rows, :], sin_ref[rows, :]
        cos_s, sin_s = cos * q_scale, sin * q_scale

        def queries(gi, row0=row0, rows=rows, xn=xn, cos_s=cos_s, sin_s=sin_s):
            dil, q_ref = dilations[gi], q_refs[gi]
            for hd in heads:
                if hd % pair == 0:
                    col0 = gi * bw + hd * HEAD_DIM
                    qs = _dot(xn, w_ref[:, col0:col0 + pair * HEAD_DIM])
                q = _rope(_rms(qs[:, _head_cols(hd % pair)], qn_ref[gi]), cos_s, sin_s)
                if dil == 1:
                    q_ref[rows, _head_cols(hd)] = q.astype(q_ref.dtype)
                else:
                    q_slabs[dil][hd] = q
            if dil != 1:
                _store_by_residue(q_slabs[dil], q_tmp, [(dil, q_ref)], row0)

        def copies(slab, tmp, out_refs, row0=row0, rows=rows):
            for dil, out_ref in zip(dilations, out_refs):
                if dil == 1:
                    for hd in heads:
                        out_ref[rows, _head_cols(hd)] = slab[hd].astype(out_ref.dtype)
            _store_by_residue(slab, tmp, [(d, ref) for d, ref in zip(dilations, out_refs) if d != 1], row0)

        order = sorted(range(n_groups), key=lambda gi: -dilations[gi])
        queries(order[0])
        probs = _memory_probs(_dot(xn, w_ref[:, n_groups * bw:]), mqg_ref[...], mk_ref)
        for hd in heads:
            if hd % pair == 0:
                k = _dot(kn, wkv_ref[:, hd * HEAD_DIM:(hd + pair) * HEAD_DIM])
            k_slab[hd] = _rope(_rms(k[:, _head_cols(hd % pair)], kn_ref[...]), cos, sin)
        copies(k_slab, k_tmp, k_refs)
        for gi in order[1:-1]:
            queries(gi)
        _memory_output(probs, mv_ref, mo_ref, rows)
        for hd in heads:
            if hd % pair == 0:
                v = _dot(kn, wkv_ref[:, kv_w + hd * HEAD_DIM:kv_w + (hd + pair) * HEAD_DIM])
            v_slab[hd] = v[:, _head_cols(hd % pair)]
        copies(v_slab, v_tmp, v_refs)
        queries(order[-1])


def _residue_out(bn, seq, width, dil):
    tiles_per_seq = seq // ATTN_IN_TILE
    if dil == 1:
        return (jax.ShapeDtypeStruct((bn * seq, width), BF16),
                pl.BlockSpec((ATTN_IN_TILE, width), lambda i: (i, 0)))
    return (jax.ShapeDtypeStruct((bn, dil, seq // dil, width), BF16),
            pl.BlockSpec((None, dil, ATTN_IN_TILE // dil, width),
                         lambda i: (i // tiles_per_seq, 0, i % tiles_per_seq, 0)))


def _inproj_b(h, gain, kv_gain, w, w_kv, q_norm, k_norm, cos2, sin2, mq_gain, mk, mv, layer, bn, seq):
    t, dm = h.shape
    mt, mw = mk.shape[2], mk.shape[3]
    dilations = tuple(d for _, d in DILATED_GROUPS)
    n_groups = len(dilations)
    bw = (w.shape[1] - mw) // n_groups
    kv_w = w_kv.shape[1] // 2
    assert kv_w == bw
    tile = ATTN_IN_TILE
    tiles_per_seq = seq // tile
    row = lambda i: (i, 0)
    pos = lambda i: (i % tiles_per_seq, 0)
    mem = lambda i: (layer, i // tiles_per_seq, 0, 0)
    copies = [_residue_out(bn, seq, bw, dil) for dil in dilations] * 3
    slab = (bw // HEAD_DIM, ATTN_IN_SUB, HEAD_DIM)
    n_slabs = sum(1 for dil in dilations if dil != 1) + 5
    pipelined = (_nbytes((tile, dm), F32) + 3 * n_groups * _nbytes((tile, bw), BF16)
                 + 2 * _nbytes((tile, HEAD_DIM), F32) + _nbytes((tile, mw), BF16)
                 + 2 * _nbytes((mt, mw), BF16))
    resident = _nbytes(w.shape, BF16) + _nbytes(w_kv.shape, BF16) + n_slabs * _nbytes(slab, F32)
    return pl.pallas_call(
        functools.partial(_inproj_b_kernel, dilations=dilations),
        out_shape=tuple([shape for shape, _ in copies] + [jax.ShapeDtypeStruct((t, mw), BF16)]),
        grid=(t // tile,),
        in_specs=[
            pl.BlockSpec((tile, dm), row),
            _resident((1, dm)), _resident((1, dm)),
            _resident(w.shape), _resident(w_kv.shape),
            _resident((n_groups, 1, HEAD_DIM)), _resident((1, HEAD_DIM)),
            pl.BlockSpec((tile, HEAD_DIM), pos), pl.BlockSpec((tile, HEAD_DIM), pos),
            _resident((1, V7X_LANES)),
            pl.BlockSpec((None, None, mt, mw), mem),
            pl.BlockSpec((None, None, mt, mw), mem),
        ],
        out_specs=tuple([spec for _, spec in copies] + [pl.BlockSpec((tile, mw), row)]),
        scratch_shapes=[pltpu.VMEM(slab, F32)] * n_slabs,
        compiler_params=pltpu.CompilerParams(
            dimension_semantics=("parallel",),
            vmem_limit_bytes=_vmem_limit(pipelined, resident, 6 * _nbytes((ATTN_IN_SUB, bw), F32))),
        name="inproj_b",
    )(h, gain, kv_gain, w, w_kv, q_norm.reshape(n_groups, 1, HEAD_DIM), k_norm, cos2, sin2,
      mq_gain, mk, mv)


def _dilated_kernel(*refs, dilations):
    n_groups = len(dilations)
    q_refs = refs[:n_groups]
    k_refs = refs[n_groups:2 * n_groups]
    v_refs = refs[2 * n_groups:3 * n_groups]
    o_ref, og_ref, lse_ref, bias_ref = refs[3 * n_groups:]
    seq = o_ref.shape[0]
    blk = ATTN_BLOCK

    qi = lax.broadcasted_iota(jnp.int32, (blk, 2 * blk), 0)
    kj = lax.broadcasted_iota(jnp.int32, (blk, 2 * blk), 1)
    band = (kj >= qi) & (kj <= qi + blk)
    bias_ref[0] = jnp.where(band & (kj >= blk), 0.0, -jnp.inf)
    bias_ref[1] = jnp.where(band, 0.0, -jnp.inf)

    parked = [gi for gi, dil in enumerate(dilations) if dil != 1]
    (last,) = [gi for gi, dil in enumerate(dilations) if dil == 1]
    for gi in parked + [last]:
        dil = dilations[gi]
        q_ref, k_ref, v_ref = q_refs[gi], k_refs[gi], v_refs[gi]
        n_blocks = seq // (blk * dil)

        def blocks(step, carry, q_ref=q_ref, k_ref=k_ref, v_ref=v_ref, gi=gi, dil=dil, n_blocks=n_blocks):
            idxs = [step * ATTN_UNROLL + u for u in range(ATTN_UNROLL)]
            starts = [pl.multiple_of(idx * blk, blk) for idx in idxs]
            tile_rows = [pl.ds(pl.multiple_of(jnp.maximum(starts[0] + (j - 1) * blk, 0), blk), blk)
                         for j in range(ATTN_UNROLL + 1)]
            k_tiles = [k_ref[rows, :] for rows in tile_rows]
            scores = [lax.dot_general(q_ref[pl.ds(start, blk), :],
                                      jnp.concatenate([k_tiles[u], k_tiles[u + 1]], axis=0), _NT,
                                      preferred_element_type=F32)
                      for u, start in enumerate(starts)]
            soft = []
            for idx, s in zip(idxs, scores):
                n = lax.rem(idx, n_blocks)
                s = s + bias_ref[jnp.minimum(n, 1)]
                m = jnp.max(s, axis=-1, keepdims=True)
                p = jnp.exp2(s - m)
                denom = jnp.sum(p, axis=-1, keepdims=True)
                soft.append((p.astype(BF16), m, denom))
            v_tiles = [v_ref[rows, :] for rows in tile_rows]
            for u, (idx, start, (p, m, denom)) in enumerate(zip(idxs, starts, soft)):
                o = _dot(p, jnp.concatenate([v_tiles[u], v_tiles[u + 1]], axis=0)) / denom
                lse2 = jnp.broadcast_to(m + jnp.log(denom) * LOG2E, (blk, HEAD_DIM))
                if dil != 1:
                    n = lax.rem(idx, n_blocks)
                    out_rows = pl.ds(n * (blk * dil) + lax.div(idx, n_blocks), blk, stride=dil)
                    slot = parked.index(gi)
                    og_ref[slot, out_rows, :] = o
                    lse_ref[slot, out_rows, :] = lse2
                else:
                    rows = pl.ds(start, blk)
                    lses = [lse2] + [lse_ref[slot, rows, :] for slot in range(len(parked))]
                    outs = [o] + [og_ref[slot, rows, :] for slot in range(len(parked))]
                    top = functools.reduce(jnp.maximum, lses)
                    ws = [jnp.exp2(l - top) for l in lses]
                    acc = sum(w * og for w, og in zip(ws, outs))
                    o_ref[rows, :] = (acc / sum(ws)).astype(o_ref.dtype)
            return carry

        lax.fori_loop(0, dil * n_blocks // ATTN_UNROLL, blocks, 0)


def _dilated_attention(qs, ks, vs, bn, seq):
    width = qs[0].shape[-1]
    heads = width // HEAD_DIM
    dilations = tuple(d for _, d in DILATED_GROUPS)
    n_groups = len(dilations)
    as_seq = lambda a: a.reshape(bn, seq, width)
    spec = pl.BlockSpec((None, seq, HEAD_DIM), lambda b, h: (b, 0, h))
    seq_bf16 = _nbytes((seq, HEAD_DIM), BF16)
    seq_f32 = _nbytes((seq, HEAD_DIM), F32)
    n_bias = 2
    scratch = 2 * (n_groups - 1) * seq_f32 + _nbytes((n_bias, ATTN_BLOCK, 2 * ATTN_BLOCK), F32)
    staged = ATTN_UNROLL * (_nbytes((ATTN_BLOCK, 2 * ATTN_BLOCK), F32) + _nbytes((ATTN_BLOCK, 2 * ATTN_BLOCK), BF16)
                            + 2 * _nbytes((ATTN_BLOCK, HEAD_DIM), F32))
    out = pl.pallas_call(
        functools.partial(_dilated_kernel, dilations=dilations),
        out_shape=jax.ShapeDtypeStruct((bn, seq, width), BF16),
        grid=(bn, heads),
        in_specs=[spec] * (3 * n_groups),
        out_specs=spec,
        scratch_shapes=[
            pltpu.VMEM((n_groups - 1, seq, HEAD_DIM), F32),
            pltpu.VMEM((n_groups - 1, seq, HEAD_DIM), F32),
            pltpu.VMEM((n_bias, ATTN_BLOCK, 2 * ATTN_BLOCK), F32),
        ],
        compiler_params=pltpu.CompilerParams(
            dimension_semantics=("parallel", "parallel"),
            vmem_limit_bytes=_vmem_limit((3 * n_groups + 1) * seq_bf16, scratch, staged)),
        name="dilated_attention",
    )(*[as_seq(a) for a in (*qs, *ks, *vs)])
    return out.reshape(bn * seq, width)


def _rope_tables(seq):
    half = HEAD_DIM // 2
    inv = ROPE_THETA ** (-jnp.arange(half, dtype=F32) / half)
    ang = jnp.arange(seq).astype(F32)[:, None] * inv[None, :]
    cos, sin = jnp.cos(ang), jnp.sin(ang)
    return jnp.concatenate([cos, cos], axis=-1), jnp.concatenate([-sin, sin], axis=-1)


def kernel(x, mem, norm_mix, norm_ffn, a_w_in, a_lb_logits, a_onorm, b_w_in, b_qnorm, kv_norm, w_kv,
           b_knorm, mem_norm, w_mem_kv, mem_qnorm, mem_knorm, w_out, w_gate_up, w_down):
    bn, seq, dm = x.shape
    depth = norm_mix.shape[0]
    n_a = a_w_in.shape[0]
    max_dil = max(d for _, d in DILATED_GROUPS)
    assert all(w == ATTN_BLOCK * d for w, d in DILATED_GROUPS)
    assert seq % (ATTN_BLOCK * max_dil) == 0 and seq % (ATTN_BLOCK * ATTN_UNROLL) == 0
    assert seq % min(HGRN_ROWS, seq) == 0 and HGRN_ROWS % CHUNK == 0 and HGRN_IN_SUB % CHUNK == 0
    assert seq % HGRN_IN_TILE == 0 and HGRN_IN_TILE % HGRN_IN_SUB == 0 and seq % FFN_TILE == 0
    assert seq % ATTN_IN_TILE == 0 and ATTN_IN_TILE % ATTN_IN_SUB == 0 and ATTN_IN_SUB % max_dil == 0
    assert depth - n_a == 1

    bf = lambda a: a.astype(BF16)
    row_vec = lambda a: a.reshape(1, -1)
    pair = lambda a: jnp.concatenate([a, a], axis=-1).reshape(1, V7X_LANES)
    cos2, sin2 = _rope_tables(seq)
    n_groups = len(DILATED_GROUPS)

    w_out_bf, w_gate_up_bf, w_down_bf = bf(w_out), bf(w_gate_up), bf(w_down)
    mk, mv = _mem_kv(mem, mem_norm, bf(w_mem_kv), mem_knorm)
    h = x.reshape(bn * seq, dm)
    for l in range(depth):
        gain = row_vec(norm_mix[l])
        if l < n_a:
            qs, lf, k, v, gate, mo = _inproj_a(h, gain, bf(a_w_in[l]), a_lb_logits, pair(mem_qnorm[l]),
                                               mk, mv, l, seq)
            o = _hgrn2(qs, lf, k, v, gate, row_vec(a_onorm[l]), bn, seq)
        else:
            j = l - n_a
            *copies, mo = _inproj_b(h, gain, row_vec(kv_norm), bf(b_w_in[j]), bf(w_kv), b_qnorm[j],
                                    row_vec(b_knorm), cos2, sin2, pair(mem_qnorm[l]), mk, mv, l, bn, seq)
            q_groups, ks, vs = (copies[i * n_groups:(i + 1) * n_groups] for i in range(3))
            o = _dilated_attention(q_groups, ks, vs, bn, seq)
        h = _mix_ffn(h, o, mo, w_out_bf, row_vec(norm_ffn[l]), w_gate_up_bf, w_down_bf, l)
    return h.reshape(bn, seq, dm)
```

```python
import functools
import math

import jax
import jax.numpy as jnp
from jax import lax
from jax.experimental import pallas as pl
from jax.experimental.pallas import tpu as pltpu

F32 = jnp.float32
BF16 = jnp.bfloat16

EPS = 1e-6
HEAD_DIM = 128
CHUNK = 64
MEM_HEAD_DIM = 64
DILATED_GROUPS = ((128, 1), (512, 4), (2048, 16))
ROPE_THETA = 10000.0
LOG2E = math.log2(math.e)

V7X_LANES = 128
V7X_VMEM_SCOPED_DEFAULT_BYTES = 16 * 1024 * 1024
V7X_VMEM_SCOPED_MAX_BYTES = 60000 * 1024
V7X_SINGLE_LOAD_STRIDE = 4

HGRN_IN_TILE = 512
HGRN_IN_SUB = 512
ATTN_IN_TILE = 512
ATTN_IN_SUB = 256
FFN_TILE = 512
HGRN_ROWS = 4096
ATTN_BLOCK = 128
ATTN_UNROLL = 16

_NT = (((1,), (1,)), ((), ()))
_TN = (((0,), (0,)), ((), ()))


def _vmem_limit(pipelined_bytes, resident_bytes, temp_bytes):
    need = 2 * pipelined_bytes + resident_bytes + temp_bytes
    return int(min(max(need, V7X_VMEM_SCOPED_DEFAULT_BYTES), V7X_VMEM_SCOPED_MAX_BYTES))


def _nbytes(shape, dtype):
    n = 1
    for s in shape:
        n *= s
    return n * jnp.dtype(dtype).itemsize


def _resident(shape):
    zeros = (0,) * len(shape)
    return pl.BlockSpec(shape, lambda *_: zeros, pipeline_mode=pl.Buffered(1))


def _dot(a, b):
    return jnp.dot(a, b, preferred_element_type=F32)


def _rms(x, gain):
    ms = jnp.mean(x * x, axis=-1, keepdims=True)
    return x * lax.rsqrt(ms + EPS) * gain


def _silu(x):
    return x * jax.nn.sigmoid(x)


def _head_cols(hd):
    return slice(hd * HEAD_DIM, (hd + 1) * HEAD_DIM)


def _rms_head_pairs(x, gain):
    lo = lax.broadcasted_iota(jnp.int32, x.shape, 1) < MEM_HEAD_DIM
    x2 = x * x
    s_lo = jnp.sum(jnp.where(lo, x2, 0.0), axis=-1, keepdims=True)
    s_hi = jnp.sum(jnp.where(lo, 0.0, x2), axis=-1, keepdims=True)
    ms = jnp.where(lo, s_lo, s_hi) * (1.0 / MEM_HEAD_DIM)
    return x * lax.rsqrt(ms + EPS) * gain


def _rope(x, cos2, sin2):
    return x * cos2 + pltpu.roll(x, HEAD_DIM // 2, axis=1) * sin2


def _store_by_residue(slab_ref, tmp_ref, dil_outs, row0):
    n_heads, tile_rows = slab_ref.shape[0], slab_ref.shape[1]
    base = V7X_SINGLE_LOAD_STRIDE
    two_hops = any(dil > base for dil, _ in dil_outs)
    if two_hops:
        part = tile_rows // base
        for r in range(base):
            for hd in range(n_heads):
                tmp_ref[hd, r * part:(r + 1) * part, :] = slab_ref[hd, pl.ds(r, part, stride=base), :]
    for dil, out_ref in dil_outs:
        rows = tile_rows // dil
        dst = slice(row0 // dil, row0 // dil + rows)
        for r in range(dil):
            for hd in range(n_heads):
                if dil < base or (dil == base and not two_hops):
                    piece = slab_ref[hd, pl.ds(r, rows, stride=dil), :]
                elif dil == base:
                    piece = tmp_ref[hd, r * rows:(r + 1) * rows, :]
                else:
                    assert dil % base == 0 and dil // base <= base
                    piece = tmp_ref[hd, pl.ds((r % base) * part + r // base, rows, stride=dil // base), :]
                out_ref[hd, r, dst, :] = piece.astype(out_ref.dtype)


def _memory_probs(mq, qgain, mk_ref):
    scaled_gain = qgain * (MEM_HEAD_DIM ** -0.5)
    probs = []
    for t in range(mq.shape[1] // V7X_LANES):
        cols = slice(t * V7X_LANES, (t + 1) * V7X_LANES)
        qn = _rms_head_pairs(mq[:, cols], scaled_gain)
        lo = lax.broadcasted_iota(jnp.int32, qn.shape, 1) < MEM_HEAD_DIM
        for keep in (lo, jnp.logical_not(lo)):
            qh = jnp.where(keep, qn, 0.0).astype(BF16)
            s = lax.dot_general(qh, mk_ref[:, cols], _NT, preferred_element_type=F32)
            p = jnp.exp(s - jnp.max(s, axis=-1, keepdims=True))
            probs.append((p.astype(BF16), jnp.sum(p, axis=-1, keepdims=True)))
    return probs


def _memory_output(probs, mv_ref, mo_ref, rows=slice(None)):
    for t in range(mo_ref.shape[1] // V7X_LANES):
        cols = slice(t * V7X_LANES, (t + 1) * V7X_LANES)
        outs = [_dot(p, mv_ref[:, cols]) / denom for p, denom in probs[2 * t:2 * t + 2]]
        lo = lax.broadcasted_iota(jnp.int32, outs[0].shape, 1) < MEM_HEAD_DIM
        mo_ref[rows, cols] = jnp.where(lo, outs[0], outs[1]).astype(mo_ref.dtype)


def _mem_kv_kernel(mem_ref, gain_ref, w_ref, kgain_ref, mk_ref, mv_ref):
    mw = mk_ref.shape[1]
    mn = _rms(mem_ref[...], gain_ref[...]).astype(BF16)
    kv = _dot(mn, w_ref[...])
    for t in range(mw // V7X_LANES):
        cols = slice(t * V7X_LANES, (t + 1) * V7X_LANES)
        mk_ref[:, cols] = _rms_head_pairs(kv[:, cols], kgain_ref[...]).astype(mk_ref.dtype)
    mv_ref[...] = kv[:, mw:].astype(mv_ref.dtype)


def _mem_kv(mem, mem_norm, w_mem_kv, mem_knorm):
    bn, mt, dm = mem.shape
    depth = w_mem_kv.shape[0]
    mw = w_mem_kv.shape[2] // 2
    kgain = jnp.concatenate([mem_knorm, mem_knorm], axis=-1).reshape(depth, 1, V7X_LANES)
    out = jax.ShapeDtypeStruct((depth, bn, mt, mw), BF16)
    return pl.pallas_call(
        _mem_kv_kernel,
        out_shape=(out, out),
        grid=(depth, bn),
        in_specs=[
            pl.BlockSpec((None, mt, dm), lambda l, b: (b, 0, 0)),
            pl.BlockSpec((None, 1, dm), lambda l, b: (l, 0, 0)),
            pl.BlockSpec((None, dm, 2 * mw), lambda l, b: (l, 0, 0)),
            pl.BlockSpec((None, 1, V7X_LANES), lambda l, b: (l, 0, 0)),
        ],
        out_specs=(
            pl.BlockSpec((None, None, mt, mw), lambda l, b: (l, b, 0, 0)),
            pl.BlockSpec((None, None, mt, mw), lambda l, b: (l, b, 0, 0)),
        ),
        name="mem_kv",
    )(mem, mem_norm.reshape(depth, 1, dm), w_mem_kv, kgain)


def _inproj_a_kernel(x_ref, gain_ref, w_ref, lbl_ref, mqg_ref, mk_ref, mv_ref,
                     qs_ref, lf_ref, k_ref, v_ref, gate_ref, mo_ref, *, layer):
    aw = qs_ref.shape[1]
    lg = lbl_ref[...]
    e = jnp.exp(lg - jnp.max(lg, axis=0, keepdims=True))
    lb = jnp.sum(e[:layer + 1], axis=0, keepdims=True) / jnp.sum(e, axis=0, keepdims=True)
    for row0 in range(0, x_ref.shape[0], HGRN_IN_SUB):
        rows = slice(row0, row0 + HGRN_IN_SUB)
        xn = _rms(x_ref[rows, :], gain_ref[...]).astype(BF16)
        probs = _memory_probs(_dot(xn, w_ref[:, 4 * aw:]), mqg_ref[...], mk_ref)
        qs_ref[rows, :] = _silu(_dot(xn, w_ref[:, 0:aw])).astype(qs_ref.dtype)
        _memory_output(probs, mv_ref, mo_ref, rows)
        f = lb + (1.0 - lb) * jax.nn.sigmoid(_dot(xn, w_ref[:, aw:2 * aw]))
        lf_ref[rows, :] = jnp.log(f) * LOG2E
        k_ref[rows, :] = (1.0 - f).astype(k_ref.dtype)
        v_ref[rows, :] = _dot(xn, w_ref[:, 2 * aw:3 * aw]).astype(v_ref.dtype)
        gate_ref[rows, :] = _silu(_dot(xn, w_ref[:, 3 * aw:4 * aw])).astype(gate_ref.dtype)


def _inproj_a(h, gain, w, lb_logits, mq_gain, mk, mv, layer, seq):
    t, dm = h.shape
    mt, mw = mk.shape[2], mk.shape[3]
    aw = (w.shape[1] - mw) // 4
    tile = HGRN_IN_TILE
    tiles_per_seq = seq // tile
    row = lambda i: (i, 0)
    mem = lambda i: (layer, i // tiles_per_seq, 0, 0)
    wide = functools.partial(jax.ShapeDtypeStruct, (t, aw))
    pipelined = (_nbytes((tile, dm), F32) + _nbytes((tile, aw), F32)
                 + 4 * _nbytes((tile, aw), BF16) + _nbytes((tile, mw), BF16)
                 + 2 * _nbytes((mt, mw), BF16))
    return pl.pallas_call(
        functools.partial(_inproj_a_kernel, layer=layer),
        out_shape=(wide(BF16), wide(F32), wide(BF16), wide(BF16), wide(BF16),
                   jax.ShapeDtypeStruct((t, mw), BF16)),
        grid=(t // tile,),
        in_specs=[
            pl.BlockSpec((tile, dm), row),
            _resident((1, dm)),
            _resident(w.shape),
            _resident(lb_logits.shape),
            _resident((1, V7X_LANES)),
            pl.BlockSpec((None, None, mt, mw), mem),
            pl.BlockSpec((None, None, mt, mw), mem),
        ],
        out_specs=tuple([pl.BlockSpec((tile, aw), row)] * 5 + [pl.BlockSpec((tile, mw), row)]),
        compiler_params=pltpu.CompilerParams(
            dimension_semantics=("parallel",),
            vmem_limit_bytes=_vmem_limit(pipelined, _nbytes(w.shape, BF16),
                                         4 * _nbytes((HGRN_IN_SUB, aw), F32))),
        name="inproj_a",
    )(h, gain, w, lb_logits, mq_gain, mk, mv)


def _hgrn2_kernel(on_ref, qs_ref, lf_ref, k_ref, v_ref, gate_ref, o_ref, state_ref):
    @pl.when(pl.program_id(2) == 0)
    def _():
        state_ref[...] = jnp.zeros_like(state_ref)

    row = lax.broadcasted_iota(jnp.int32, (CHUNK, HEAD_DIM), 0)
    causal = (lax.broadcasted_iota(jnp.int32, (CHUNK, CHUNK), 0)
              >= lax.broadcasted_iota(jnp.int32, (CHUNK, CHUNK), 1))
    onorm = on_ref[...]

    chunks = [slice(c * CHUNK, (c + 1) * CHUNK) for c in range(qs_ref.shape[0] // CHUNK)]
    q_ins, vs, decays, atts, kvs = [], [], [], [], []
    for rows in chunks:
        b = lf_ref[rows, :]
        shift = 1
        while shift < CHUNK:
            b = b + jnp.where(row >= shift, pltpu.roll(b, shift, axis=0), 0.0)
            shift *= 2
        b_end = b[CHUNK - 1:CHUNK, :]
        k = k_ref[rows, :].astype(F32)
        q_in = (qs_ref[rows, :].astype(F32) * jnp.exp2(b)).astype(BF16)
        k_in = (k * jnp.exp2(-b)).astype(BF16)
        k_out = (k * jnp.exp2(b_end - b)).astype(BF16)
        v = v_ref[rows, :]
        q_ins.append(q_in)
        vs.append(v)
        decays.append(jnp.exp2(b_end))
        atts.append(lax.dot_general(q_in, k_in, _NT, preferred_element_type=F32))
        kvs.append(lax.dot_general(v, k_out, _TN, preferred_element_type=F32))

    state_t = state_ref[...]
    states = []
    for decay, kv in zip(decays, kvs):
        states.append(state_t.astype(BF16))
        state_t = state_t * decay + kv
    state_ref[...] = state_t

    for rows, q_in, v, att, state_in in zip(chunks, q_ins, vs, atts, states):
        att = jnp.where(causal, att, 0.0).astype(BF16)
        o = _dot(att, v) + lax.dot_general(q_in, state_in, _NT, preferred_element_type=F32)
        o_ref[rows, :] = (_rms(o, onorm) * gate_ref[rows, :].astype(F32)).astype(o_ref.dtype)


def _hgrn2(qs, lf, k, v, gate, onorm, bn, seq):
    t, aw = qs.shape
    heads = aw // HEAD_DIM
    step_rows = min(HGRN_ROWS, seq)
    steps = seq // step_rows
    spec = pl.BlockSpec((step_rows, HEAD_DIM), lambda b, h, s: (b * steps + s, h))
    return pl.pallas_call(
        _hgrn2_kernel,
        out_shape=jax.ShapeDtypeStruct((t, aw), BF16),
        grid=(bn, heads, steps),
        in_specs=[pl.BlockSpec((1, HEAD_DIM), lambda b, h, s: (0, h)), spec, spec, spec, spec, spec],
        out_specs=spec,
        scratch_shapes=[pltpu.VMEM((HEAD_DIM, HEAD_DIM), F32)],
        compiler_params=pltpu.CompilerParams(
            dimension_semantics=("parallel", "parallel", "arbitrary")),
        name="hgrn2",
    )(onorm, qs, lf, k, v, gate)


def _mix_ffn_kernel(h_ref, o_ref, mo_ref, wo_ref, nf_ref, wgu_ref, wd_ref, h_out):
    main_w = o_ref.shape[1]
    hidden = wd_ref.shape[0]
    h = h_ref[...] + _dot(o_ref[...], wo_ref[0:main_w, :]) + _dot(mo_ref[...], wo_ref[main_w:, :])
    hn = _rms(h, nf_ref[...]).astype(BF16)
    act = (_silu(_dot(hn, wgu_ref[:, 0:hidden])) * _dot(hn, wgu_ref[:, hidden:])).astype(BF16)
    h_out[...] = h + _dot(act, wd_ref[...])


def _layer_resident(stacked, layer):
    zeros = (0,) * (stacked.ndim - 1)
    return pl.BlockSpec((None,) + stacked.shape[1:], lambda *_: (layer,) + zeros,
                        pipeline_mode=pl.Buffered(1))


def _mix_ffn(h, o, mo, w_out, norm_ffn, w_gate_up, w_down, layer):
    t, dm = h.shape
    main_w, mw = o.shape[1], mo.shape[1]
    row = lambda i: (i, 0)
    resident = sum(_nbytes(w.shape[1:], BF16) for w in (w_out, w_gate_up, w_down))
    tile = FFN_TILE
    pipelined = 2 * _nbytes((tile, dm), F32) + _nbytes((tile, main_w + mw), BF16)
    temps = 3 * _nbytes((tile, dm), F32) + 3 * _nbytes((tile, w_down.shape[1]), F32)
    return pl.pallas_call(
        _mix_ffn_kernel,
        out_shape=jax.ShapeDtypeStruct((t, dm), F32),
        grid=(t // tile,),
        in_specs=[
            pl.BlockSpec((tile, dm), row), pl.BlockSpec((tile, main_w), row),
            pl.BlockSpec((tile, mw), row),
            _layer_resident(w_out, layer), _resident((1, dm)), _layer_resident(w_gate_up, layer),
            _layer_resident(w_down, layer),
        ],
        out_specs=pl.BlockSpec((tile, dm), row),
        compiler_params=pltpu.CompilerParams(
            dimension_semantics=("parallel",),
            vmem_limit_bytes=_vmem_limit(pipelined, resident, temps)),
        name="mix_ffn",
    )(h, o, mo, w_out, norm_ffn, w_gate_up, w_down)


def _inproj_b_kernel(*refs, dilations):
    n_groups = len(dilations)
    (x_ref, gain_ref, kvg_ref, w_ref, wkv_ref, qn_ref, kn_ref, cos_ref, sin_ref,
     mqg_ref, mk_ref, mv_ref) = refs[:12]
    q_refs = refs[12:12 + n_groups]
    k_refs = refs[12 + n_groups:12 + 2 * n_groups]
    v_refs = refs[12 + 2 * n_groups:12 + 3 * n_groups]
    mo_ref = refs[12 + 3 * n_groups]
    scratch = refs[13 + 3 * n_groups:]
    q_slabs = dict(zip([d for d in dilations if d != 1], scratch[:-5]))
    k_slab, v_slab, q_tmp, k_tmp, v_tmp = scratch[-5:]
    kv_w = wkv_ref.shape[1] // 2
    bw = (w_ref.shape[1] - mo_ref.shape[1]) // n_groups
    heads = range(bw // HEAD_DIM)
    pair = 2
    q_scale = (HEAD_DIM ** -0.5) * LOG2E

    for row0 in range(0, x_ref.shape[0], ATTN_IN_SUB):
        rows = slice(row0, row0 + ATTN_IN_SUB)
        x = x_ref[rows, :]
        xhat = x * lax.rsqrt(jnp.mean(x * x, axis=-1, keepdims=True) + EPS)
        xn = (xhat * gain_ref[...]).astype(BF16)
        kn = (xhat * kvg_ref[...]).astype(BF16)
        cos, sin = cos_ref[rows, :], sin_ref[rows, :]
        cos_s, sin_s = cos * q_scale, sin * q_scale

        def queries(gi, row0=row0, rows=rows, xn=xn, cos_s=cos_s, sin_s=sin_s):
            dil, q_ref = dilations[gi], q_refs[gi]
            for hd in heads:
                if hd % pair == 0:
                    col0 = gi * bw + hd * HEAD_DIM
                    qs = _dot(xn, w_ref[:, col0:col0 + pair * HEAD_DIM])
                q = _rope(_rms(qs[:, _head_cols(hd % pair)], qn_ref[gi]), cos_s, sin_s)
                if dil == 1:
                    q_ref[hd, rows, :] = q.astype(q_ref.dtype)
                else:
                    q_slabs[dil][hd] = q
            if dil != 1:
                _store_by_residue(q_slabs[dil], q_tmp, [(dil, q_ref)], row0)

        def copies(slab, tmp, out_refs, row0=row0, rows=rows):
            for dil, out_ref in zip(dilations, out_refs):
                if dil == 1:
                    for hd in heads:
                        out_ref[hd, rows, :] = slab[hd].astype(out_ref.dtype)
            _store_by_residue(slab, tmp, [(d, ref) for d, ref in zip(dilations, out_refs) if d != 1], row0)

        order = sorted(range(n_groups), key=lambda gi: -dilations[gi])
        queries(order[0])
        probs = _memory_probs(_dot(xn, w_ref[:, n_groups * bw:]), mqg_ref[...], mk_ref)
        for hd in heads:
            if hd % pair == 0:
                k = _dot(kn, wkv_ref[:, hd * HEAD_DIM:(hd + pair) * HEAD_DIM])
            k_slab[hd] = _rope(_rms(k[:, _head_cols(hd % pair)], kn_ref[...]), cos, sin)
        copies(k_slab, k_tmp, k_refs)
        for gi in order[1:-1]:
            queries(gi)
        _memory_output(probs, mv_ref, mo_ref, rows)
        for hd in heads:
            if hd % pair == 0:
                v = _dot(kn, wkv_ref[:, kv_w + hd * HEAD_DIM:kv_w + (hd + pair) * HEAD_DIM])
            v_slab[hd] = v[:, _head_cols(hd % pair)]
        copies(v_slab, v_tmp, v_refs)
        queries(order[-1])


def _residue_out(bn, seq, width, dil):
    tiles_per_seq = seq // ATTN_IN_TILE
    heads = width // HEAD_DIM
    if dil == 1:
        return (jax.ShapeDtypeStruct((bn, heads, seq, HEAD_DIM), BF16),
                pl.BlockSpec((None, heads, ATTN_IN_TILE, HEAD_DIM),
                             lambda i: (i // tiles_per_seq, 0, i % tiles_per_seq, 0)))
    return (jax.ShapeDtypeStruct((bn, heads, dil, seq // dil, HEAD_DIM), BF16),
            pl.BlockSpec((None, heads, dil, ATTN_IN_TILE // dil, HEAD_DIM),
                         lambda i: (i // tiles_per_seq, 0, 0, i % tiles_per_seq, 0)))


def _inproj_b(h, gain, kv_gain, w, w_kv, q_norm, k_norm, cos2, sin2, mq_gain, mk, mv, layer, bn, seq):
    t, dm = h.shape
    mt, mw = mk.shape[2], mk.shape[3]
    dilations = tuple(d for _, d in DILATED_GROUPS)
    n_groups = len(dilations)
    bw = (w.shape[1] - mw) // n_groups
    kv_w = w_kv.shape[1] // 2
    assert kv_w == bw
    tile = ATTN_IN_TILE
    tiles_per_seq = seq // tile
    row = lambda i: (i, 0)
    pos = lambda i: (i % tiles_per_seq, 0)
    mem = lambda i: (layer, i // tiles_per_seq, 0, 0)
    copies = [_residue_out(bn, seq, bw, dil) for dil in dilations] * 3
    slab = (bw // HEAD_DIM, ATTN_IN_SUB, HEAD_DIM)
    n_slabs = sum(1 for dil in dilations if dil != 1) + 5
    pipelined = (_nbytes((tile, dm), F32) + 3 * n_groups * _nbytes((tile, bw), BF16)
                 + 2 * _nbytes((tile, HEAD_DIM), F32) + _nbytes((tile, mw), BF16)
                 + 2 * _nbytes((mt, mw), BF16))
    resident = _nbytes(w.shape, BF16) + _nbytes(w_kv.shape, BF16) + n_slabs * _nbytes(slab, F32)
    return pl.pallas_call(
        functools.partial(_inproj_b_kernel, dilations=dilations),
        out_shape=tuple([shape for shape, _ in copies] + [jax.ShapeDtypeStruct((t, mw), BF16)]),
        grid=(t // tile,),
        in_specs=[
            pl.BlockSpec((tile, dm), row),
            _resident((1, dm)), _resident((1, dm)),
            _resident(w.shape), _resident(w_kv.shape),
            _resident((n_groups, 1, HEAD_DIM)), _resident((1, HEAD_DIM)),
            pl.BlockSpec((tile, HEAD_DIM), pos), pl.BlockSpec((tile, HEAD_DIM), pos),
            _resident((1, V7X_LANES)),
            pl.BlockSpec((None, None, mt, mw), mem),
            pl.BlockSpec((None, None, mt, mw), mem),
        ],
        out_specs=tuple([spec for _, spec in copies] + [pl.BlockSpec((tile, mw), row)]),
        scratch_shapes=[pltpu.VMEM(slab, F32)] * n_slabs,
        compiler_params=pltpu.CompilerParams(
            dimension_semantics=("parallel",),
            vmem_limit_bytes=_vmem_limit(pipelined, resident, 6 * _nbytes((ATTN_IN_SUB, bw), F32))),
        name="inproj_b",
    )(h, gain, kv_gain, w, w_kv, q_norm.reshape(n_groups, 1, HEAD_DIM), k_norm, cos2, sin2,
      mq_gain, mk, mv)


def _dilated_kernel(*refs, dilations):
    n_groups = len(dilations)
    q_refs = refs[:n_groups]
    k_refs = refs[n_groups:2 * n_groups]
    v_refs = refs[2 * n_groups:3 * n_groups]
    o_ref, og_ref, lse_ref, bias_ref = refs[3 * n_groups:]
    seq = o_ref.shape[0]
    blk = ATTN_BLOCK

    qi = lax.broadcasted_iota(jnp.int32, (blk, 2 * blk), 0)
    kj = lax.broadcasted_iota(jnp.int32, (blk, 2 * blk), 1)
    band = (kj >= qi) & (kj <= qi + blk)
    bias_ref[0] = jnp.where(band & (kj >= blk), 0.0, -jnp.inf)
    bias_ref[1] = jnp.where(band, 0.0, -jnp.inf)

    parked = [gi for gi, dil in enumerate(dilations) if dil != 1]
    (last,) = [gi for gi, dil in enumerate(dilations) if dil == 1]
    for gi in parked + [last]:
        dil = dilations[gi]
        q_ref, k_ref, v_ref = q_refs[gi], k_refs[gi], v_refs[gi]
        n_blocks = seq // (blk * dil)

        def blocks(step, carry, q_ref=q_ref, k_ref=k_ref, v_ref=v_ref, gi=gi, dil=dil, n_blocks=n_blocks):
            idxs = [step * ATTN_UNROLL + u for u in range(ATTN_UNROLL)]
            starts = [pl.multiple_of(idx * blk, blk) for idx in idxs]
            tile_rows = [pl.ds(pl.multiple_of(jnp.maximum(starts[0] + (j - 1) * blk, 0), blk), blk)
                         for j in range(ATTN_UNROLL + 1)]
            k_tiles = [k_ref[rows, :] for rows in tile_rows]
            scores = [lax.dot_general(q_ref[pl.ds(start, blk), :],
                                      jnp.concatenate([k_tiles[u], k_tiles[u + 1]], axis=0), _NT,
                                      preferred_element_type=F32)
                      for u, start in enumerate(starts)]
            soft = []
            for idx, s in zip(idxs, scores):
                n = lax.rem(idx, n_blocks)
                s = s + bias_ref[jnp.minimum(n, 1)]
                m = jnp.max(s, axis=-1, keepdims=True)
                p = jnp.exp2(s - m)
                denom = jnp.sum(p, axis=-1, keepdims=True)
                soft.append((p.astype(BF16), m, denom))
            v_tiles = [v_ref[rows, :] for rows in tile_rows]
            for u, (idx, start, (p, m, denom)) in enumerate(zip(idxs, starts, soft)):
                o = _dot(p, jnp.concatenate([v_tiles[u], v_tiles[u + 1]], axis=0)) / denom
                lse2 = jnp.broadcast_to(m + jnp.log(denom) * LOG2E, (blk, HEAD_DIM))
                if dil != 1:
                    n = lax.rem(idx, n_blocks)
                    out_rows = pl.ds(n * (blk * dil) + lax.div(idx, n_blocks), blk, stride=dil)
                    slot = parked.index(gi)
                    og_ref[slot, out_rows, :] = o
                    lse_ref[slot, out_rows, :] = lse2
                else:
                    rows = pl.ds(start, blk)
                    lses = [lse2] + [lse_ref[slot, rows, :] for slot in range(len(parked))]
                    outs = [o] + [og_ref[slot, rows, :] for slot in range(len(parked))]
                    top = functools.reduce(jnp.maximum, lses)
                    ws = [jnp.exp2(l - top) for l in lses]
                    acc = sum(w * og for w, og in zip(ws, outs))
                    o_ref[rows, :] = (acc / sum(ws)).astype(o_ref.dtype)
            return carry

        lax.fori_loop(0, dil * n_blocks // ATTN_UNROLL, blocks, 0)


def _dilated_attention(qs, ks, vs, bn, seq):
    heads = qs[0].shape[1]
    width = heads * HEAD_DIM
    dilations = tuple(d for _, d in DILATED_GROUPS)
    n_groups = len(dilations)
    as_seq = lambda a: a.reshape(bn, heads, seq, HEAD_DIM)
    spec = pl.BlockSpec((None, None, seq, HEAD_DIM), lambda b, h: (b, h, 0, 0))
    seq_bf16 = _nbytes((seq, HEAD_DIM), BF16)
    seq_f32 = _nbytes((seq, HEAD_DIM), F32)
    n_bias = 2
    scratch = 2 * (n_groups - 1) * seq_f32 + _nbytes((n_bias, ATTN_BLOCK, 2 * ATTN_BLOCK), F32)
    staged = ATTN_UNROLL * (_nbytes((ATTN_BLOCK, 2 * ATTN_BLOCK), F32) + _nbytes((ATTN_BLOCK, 2 * ATTN_BLOCK), BF16)
                            + 2 * _nbytes((ATTN_BLOCK, HEAD_DIM), F32))
    out = pl.pallas_call(
        functools.partial(_dilated_kernel, dilations=dilations),
        out_shape=jax.ShapeDtypeStruct((bn, seq, width), BF16),
        grid=(bn, heads),
        in_specs=[spec] * (3 * n_groups),
        out_specs=pl.BlockSpec((None, seq, HEAD_DIM), lambda b, h: (b, 0, h)),
        scratch_shapes=[
            pltpu.VMEM((n_groups - 1, seq, HEAD_DIM), F32),
            pltpu.VMEM((n_groups - 1, seq, HEAD_DIM), F32),
            pltpu.VMEM((n_bias, ATTN_BLOCK, 2 * ATTN_BLOCK), F32),
        ],
        compiler_params=pltpu.CompilerParams(
            dimension_semantics=("parallel", "parallel"),
            vmem_limit_bytes=_vmem_limit((3 * n_groups + 1) * seq_bf16, scratch, staged)),
        name="dilated_attention",
    )(*[as_seq(a) for a in (*qs, *ks, *vs)])
    return out.reshape(bn * seq, width)


def _rope_tables(seq):
    half = HEAD_DIM // 2
    inv = ROPE_THETA ** (-jnp.arange(half, dtype=F32) / half)
    ang = jnp.arange(seq).astype(F32)[:, None] * inv[None, :]
    cos, sin = jnp.cos(ang), jnp.sin(ang)
    return jnp.concatenate([cos, cos], axis=-1), jnp.concatenate([-sin, sin], axis=-1)


def kernel(x, mem, norm_mix, norm_ffn, a_w_in, a_lb_logits, a_onorm, b_w_in, b_qnorm, kv_norm, w_kv,
           b_knorm, mem_norm, w_mem_kv, mem_qnorm, mem_knorm, w_out, w_gate_up, w_down):
    bn, seq, dm = x.shape
    depth = norm_mix.shape[0]
    n_a = a_w_in.shape[0]
    max_dil = max(d for _, d in DILATED_GROUPS)
    assert all(w == ATTN_BLOCK * d for w, d in DILATED_GROUPS)
    assert seq % (ATTN_BLOCK * max_dil) == 0 and seq % (ATTN_BLOCK * ATTN_UNROLL) == 0
    assert seq % min(HGRN_ROWS, seq) == 0 and HGRN_ROWS % CHUNK == 0 and HGRN_IN_SUB % CHUNK == 0
    assert seq % HGRN_IN_TILE == 0 and HGRN_IN_TILE % HGRN_IN_SUB == 0 and seq % FFN_TILE == 0
    assert seq % ATTN_IN_TILE == 0 and ATTN_IN_TILE % ATTN_IN_SUB == 0 and ATTN_IN_SUB % max_dil == 0
    assert depth - n_a == 1

    bf = lambda a: a.astype(BF16)
    row_vec = lambda a: a.reshape(1, -1)
    pair = lambda a: jnp.concatenate([a, a], axis=-1).reshape(1, V7X_LANES)
    cos2, sin2 = _rope_tables(seq)
    n_groups = len(DILATED_GROUPS)

    w_out_bf, w_gate_up_bf, w_down_bf = bf(w_out), bf(w_gate_up), bf(w_down)
    mk, mv = _mem_kv(mem, mem_norm, bf(w_mem_kv), mem_knorm)
    h = x.reshape(bn * seq, dm)
    for l in range(depth):
        gain = row_vec(norm_mix[l])
        if l < n_a:
            qs, lf, k, v, gate, mo = _inproj_a(h, gain, bf(a_w_in[l]), a_lb_logits, pair(mem_qnorm[l]),
                                               mk, mv, l, seq)
            o = _hgrn2(qs, lf, k, v, gate, row_vec(a_onorm[l]), bn, seq)
        else:
            j = l - n_a
            *copies, mo = _inproj_b(h, gain, row_vec(kv_norm), bf(b_w_in[j]), bf(w_kv), b_qnorm[j],
                                    row_vec(b_knorm), cos2, sin2, pair(mem_qnorm[l]), mk, mv, l, bn, seq)
            q_groups, ks, vs = (copies[i * n_groups:(i + 1) * n_groups] for i in range(3))
            o = _dilated_attention(q_groups, ks, vs, bn, seq)
        h = _mix_ffn(h, o, mo, w_out_bf, row_vec(norm_ffn[l]), w_gate_up_bf, w_down_bf, l)
    return h.reshape(bn, seq, dm)
```

```python
import functools
import math

import jax
import jax.numpy as jnp
from jax import lax
from jax.experimental import pallas as pl
from jax.experimental.pallas import tpu as pltpu

F32 = jnp.float32
BF16 = jnp.bfloat16

EPS = 1e-6
HEAD_DIM = 128
CHUNK = 64
MEM_HEAD_DIM = 64
DILATED_GROUPS = ((128, 1), (512, 4), (2048, 16))
ROPE_THETA = 10000.0
LOG2E = math.log2(math.e)

V7X_LANES = 128
V7X_VMEM_SCOPED_DEFAULT_BYTES = 16 * 1024 * 1024
V7X_VMEM_SCOPED_MAX_BYTES = 60000 * 1024
V7X_SINGLE_LOAD_STRIDE = 4

HGRN_IN_TILE = 512
HGRN_IN_SUB = 512
ATTN_IN_TILE = 512
ATTN_IN_SUB = 256
FFN_TILE = 512
HGRN_ROWS = 4096
ATTN_BLOCK = 128
ATTN_UNROLL = 16

_NT = (((1,), (1,)), ((), ()))
_TN = (((0,), (0,)), ((), ()))


def _vmem_limit(pipelined_bytes, resident_bytes, temp_bytes):
    need = 2 * pipelined_bytes + resident_bytes + temp_bytes
    return int(min(max(need, V7X_VMEM_SCOPED_DEFAULT_BYTES), V7X_VMEM_SCOPED_MAX_BYTES))


def _nbytes(shape, dtype):
    n = 1
    for s in shape:
        n *= s
    return n * jnp.dtype(dtype).itemsize


def _resident(shape):
    zeros = (0,) * len(shape)
    return pl.BlockSpec(shape, lambda *_: zeros, pipeline_mode=pl.Buffered(1))


def _dot(a, b):
    return jnp.dot(a, b, preferred_element_type=F32)


def _rms(x, gain):
    ms = jnp.mean(x * x, axis=-1, keepdims=True)
    return x * lax.rsqrt(ms + EPS) * gain


def _silu(x):
    return x * jax.nn.sigmoid(x)


def _head_cols(hd):
    return slice(hd * HEAD_DIM, (hd + 1) * HEAD_DIM)


def _rms_head_pairs(x, gain):
    lo = lax.broadcasted_iota(jnp.int32, x.shape, 1) < MEM_HEAD_DIM
    x2 = x * x
    s_lo = jnp.sum(jnp.where(lo, x2, 0.0), axis=-1, keepdims=True)
    s_hi = jnp.sum(jnp.where(lo, 0.0, x2), axis=-1, keepdims=True)
    ms = jnp.where(lo, s_lo, s_hi) * (1.0 / MEM_HEAD_DIM)
    return x * lax.rsqrt(ms + EPS) * gain


def _rope(x, cos2, sin2):
    return x * cos2 + pltpu.roll(x, HEAD_DIM // 2, axis=1) * sin2


def _store_by_residue(slab_ref, tmp_ref, dil_outs, row0):
    n_heads, tile_rows = slab_ref.shape[0], slab_ref.shape[1]
    base = V7X_SINGLE_LOAD_STRIDE
    two_hops = any(dil > base for dil, _ in dil_outs)
    if two_hops:
        part = tile_rows // base
        for r in range(base):
            for hd in range(n_heads):
                tmp_ref[hd, r * part:(r + 1) * part, :] = slab_ref[hd, pl.ds(r, part, stride=base), :]
    for dil, out_ref in dil_outs:
        rows = tile_rows // dil
        dst = slice(row0 // dil, row0 // dil + rows)
        for r in range(dil):
            for hd in range(n_heads):
                if dil < base or (dil == base and not two_hops):
                    piece = slab_ref[hd, pl.ds(r, rows, stride=dil), :]
                elif dil == base:
                    piece = tmp_ref[hd, r * rows:(r + 1) * rows, :]
                else:
                    assert dil % base == 0 and dil // base <= base
                    piece = tmp_ref[hd, pl.ds((r % base) * part + r // base, rows, stride=dil // base), :]
                out_ref[hd, r, dst, :] = piece.astype(out_ref.dtype)


def _memory_probs(mq, qgain, mk_ref):
    scaled_gain = qgain * (MEM_HEAD_DIM ** -0.5)
    probs = []
    for t in range(mq.shape[1] // V7X_LANES):
        cols = slice(t * V7X_LANES, (t + 1) * V7X_LANES)
        qn = _rms_head_pairs(mq[:, cols], scaled_gain)
        lo = lax.broadcasted_iota(jnp.int32, qn.shape, 1) < MEM_HEAD_DIM
        for keep in (lo, jnp.logical_not(lo)):
            qh = jnp.where(keep, qn, 0.0).astype(BF16)
            s = lax.dot_general(qh, mk_ref[:, cols], _NT, preferred_element_type=F32)
            p = jnp.exp(s - jnp.max(s, axis=-1, keepdims=True))
            probs.append((p.astype(BF16), jnp.sum(p, axis=-1, keepdims=True)))
    return probs


def _memory_output(probs, mv_ref, mo_ref, rows=slice(None)):
    for t in range(mo_ref.shape[1] // V7X_LANES):
        cols = slice(t * V7X_LANES, (t + 1) * V7X_LANES)
        outs = [_dot(p, mv_ref[:, cols]) / denom for p, denom in probs[2 * t:2 * t + 2]]
        lo = lax.broadcasted_iota(jnp.int32, outs[0].shape, 1) < MEM_HEAD_DIM
        mo_ref[rows, cols] = jnp.where(lo, outs[0], outs[1]).astype(mo_ref.dtype)


def _mem_kv_kernel(mem_ref, gain_ref, w_ref, kgain_ref, mk_ref, mv_ref):
    mw = mk_ref.shape[1]
    mn = _rms(mem_ref[...], gain_ref[...]).astype(BF16)
    kv = _dot(mn, w_ref[...])
    for t in range(mw // V7X_LANES):
        cols = slice(t * V7X_LANES, (t + 1) * V7X_LANES)
        mk_ref[:, cols] = _rms_head_pairs(kv[:, cols], kgain_ref[...]).astype(mk_ref.dtype)
    mv_ref[...] = kv[:, mw:].astype(mv_ref.dtype)


def _mem_kv(mem, mem_norm, w_mem_kv, mem_knorm):
    bn, mt, dm = mem.shape
    depth = w_mem_kv.shape[0]
    mw = w_mem_kv.shape[2] // 2
    kgain = jnp.concatenate([mem_knorm, mem_knorm], axis=-1).reshape(depth, 1, V7X_LANES)
    out = jax.ShapeDtypeStruct((depth, bn, mt, mw), BF16)
    return pl.pallas_call(
        _mem_kv_kernel,
        out_shape=(out, out),
        grid=(depth, bn),
        in_specs=[
            pl.BlockSpec((None, mt, dm), lambda l, b: (b, 0, 0)),
            pl.BlockSpec((None, 1, dm), lambda l, b: (l, 0, 0)),
            pl.BlockSpec((None, dm, 2 * mw), lambda l, b: (l, 0, 0)),
            pl.BlockSpec((None, 1, V7X_LANES), lambda l, b: (l, 0, 0)),
        ],
        out_specs=(
            pl.BlockSpec((None, None, mt, mw), lambda l, b: (l, b, 0, 0)),
            pl.BlockSpec((None, None, mt, mw), lambda l, b: (l, b, 0, 0)),
        ),
        name="mem_kv",
    )(mem, mem_norm.reshape(depth, 1, dm), w_mem_kv, kgain)


def _inproj_a_kernel(x_ref, gain_ref, w_ref, lbl_ref, mqg_ref, mk_ref, mv_ref,
                     qs_ref, lf_ref, k_ref, v_ref, gate_ref, mo_ref, *, layer):
    aw = qs_ref.shape[1]
    lg = lbl_ref[...]
    e = jnp.exp(lg - jnp.max(lg, axis=0, keepdims=True))
    lb = jnp.sum(e[:layer + 1], axis=0, keepdims=True) / jnp.sum(e, axis=0, keepdims=True)
    for row0 in range(0, x_ref.shape[0], HGRN_IN_SUB):
        rows = slice(row0, row0 + HGRN_IN_SUB)
        xn = _rms(x_ref[rows, :], gain_ref[...]).astype(BF16)
        probs = _memory_probs(_dot(xn, w_ref[:, 4 * aw:]), mqg_ref[...], mk_ref)
        qs_ref[rows, :] = _silu(_dot(xn, w_ref[:, 0:aw])).astype(qs_ref.dtype)
        _memory_output(probs, mv_ref, mo_ref, rows)
        f = lb + (1.0 - lb) * jax.nn.sigmoid(_dot(xn, w_ref[:, aw:2 * aw]))
        lf_ref[rows, :] = jnp.log(f) * LOG2E
        k_ref[rows, :] = (1.0 - f).astype(k_ref.dtype)
        v_ref[rows, :] = _dot(xn, w_ref[:, 2 * aw:3 * aw]).astype(v_ref.dtype)
        gate_ref[rows, :] = _silu(_dot(xn, w_ref[:, 3 * aw:4 * aw])).astype(gate_ref.dtype)


def _inproj_a(h, gain, w, lb_logits, mq_gain, mk, mv, layer, seq):
    t, dm = h.shape
    mt, mw = mk.shape[2], mk.shape[3]
    aw = (w.shape[1] - mw) // 4
    tile = HGRN_IN_TILE
    tiles_per_seq = seq // tile
    row = lambda i: (i, 0)
    mem = lambda i: (layer, i // tiles_per_seq, 0, 0)
    wide = functools.partial(jax.ShapeDtypeStruct, (t, aw))
    pipelined = (_nbytes((tile, dm), F32) + _nbytes((tile, aw), F32)
                 + 4 * _nbytes((tile, aw), BF16) + _nbytes((tile, mw), BF16)
                 + 2 * _nbytes((mt, mw), BF16))
    return pl.pallas_call(
        functools.partial(_inproj_a_kernel, layer=layer),
        out_shape=(wide(BF16), wide(F32), wide(BF16), wide(BF16), wide(BF16),
                   jax.ShapeDtypeStruct((t, mw), BF16)),
        grid=(t // tile,),
        in_specs=[
            pl.BlockSpec((tile, dm), row),
            _resident((1, dm)),
            _resident(w.shape),
            _resident(lb_logits.shape),
            _resident((1, V7X_LANES)),
            pl.BlockSpec((None, None, mt, mw), mem),
            pl.BlockSpec((None, None, mt, mw), mem),
        ],
        out_specs=tuple([pl.BlockSpec((tile, aw), row)] * 5 + [pl.BlockSpec((tile, mw), row)]),
        compiler_params=pltpu.CompilerParams(
            dimension_semantics=("parallel",),
            vmem_limit_bytes=_vmem_limit(pipelined, _nbytes(w.shape, BF16),
                                         4 * _nbytes((HGRN_IN_SUB, aw), F32))),
        name="inproj_a",
    )(h, gain, w, lb_logits, mq_gain, mk, mv)


def _hgrn2_kernel(on_ref, qs_ref, lf_ref, k_ref, v_ref, gate_ref, o_ref, state_ref):
    @pl.when(pl.program_id(2) == 0)
    def _():
        state_ref[...] = jnp.zeros_like(state_ref)

    row = lax.broadcasted_iota(jnp.int32, (CHUNK, HEAD_DIM), 0)
    causal = (lax.broadcasted_iota(jnp.int32, (CHUNK, CHUNK), 0)
              >= lax.broadcasted_iota(jnp.int32, (CHUNK, CHUNK), 1))
    onorm = on_ref[...]

    chunks = [slice(c * CHUNK, (c + 1) * CHUNK) for c in range(qs_ref.shape[0] // CHUNK)]
    q_ins, vs, decays, atts, kvs = [], [], [], [], []
    for rows in chunks:
        b = lf_ref[rows, :]
        shift = 1
        while shift < CHUNK:
            b = b + jnp.where(row >= shift, pltpu.roll(b, shift, axis=0), 0.0)
            shift *= 2
        b_end = b[CHUNK - 1:CHUNK, :]
        k = k_ref[rows, :].astype(F32)
        q_in = (qs_ref[rows, :].astype(F32) * jnp.exp2(b)).astype(BF16)
        k_in = (k * jnp.exp2(-b)).astype(BF16)
        k_out = (k * jnp.exp2(b_end - b)).astype(BF16)
        v = v_ref[rows, :]
        q_ins.append(q_in)
        vs.append(v)
        decays.append(jnp.exp2(b_end))
        atts.append(lax.dot_general(q_in, k_in, _NT, preferred_element_type=F32))
        kvs.append(lax.dot_general(v, k_out, _TN, preferred_element_type=F32))

    state_t = state_ref[...]
    states = []
    for decay, kv in zip(decays, kvs):
        states.append(state_t.astype(BF16))
        state_t = state_t * decay + kv
    state_ref[...] = state_t

    for rows, q_in, v, att, state_in in zip(chunks, q_ins, vs, atts, states):
        att = jnp.where(causal, att, 0.0).astype(BF16)
        o = _dot(att, v) + lax.dot_general(q_in, state_in, _NT, preferred_element_type=F32)
        o_ref[rows, :] = (_rms(o, onorm) * gate_ref[rows, :].astype(F32)).astype(o_ref.dtype)


def _hgrn2(qs, lf, k, v, gate, onorm, bn, seq):
    t, aw = qs.shape
    heads = aw // HEAD_DIM
    step_rows = min(HGRN_ROWS, seq)
    steps = seq // step_rows
    spec = pl.BlockSpec((step_rows, HEAD_DIM), lambda b, h, s: (b * steps + s, h))
    return pl.pallas_call(
        _hgrn2_kernel,
        out_shape=jax.ShapeDtypeStruct((t, aw), BF16),
        grid=(bn, heads, steps),
        in_specs=[pl.BlockSpec((1, HEAD_DIM), lambda b, h, s: (0, h)), spec, spec, spec, spec, spec],
        out_specs=spec,
        scratch_shapes=[pltpu.VMEM((HEAD_DIM, HEAD_DIM), F32)],
        compiler_params=pltpu.CompilerParams(
            dimension_semantics=("parallel", "parallel", "arbitrary")),
        name="hgrn2",
    )(onorm, qs, lf, k, v, gate)


def _mix_ffn_kernel(h_ref, o_ref, mo_ref, wo_ref, nf_ref, wgu_ref, wd_ref, h_out):
    hidden = wd_ref.shape[0]
    mix =jnp.concatenate([o_ref[...], mo_ref[...]], axis=1)
    h = h_ref[...] + _dot(mix, wo_ref[...])
    hn = _rms(h, nf_ref[...]).astype(BF16)
    act = (_silu(_dot(hn, wgu_ref[:, 0:hidden])) * _dot(hn, wgu_ref[:, hidden:])).astype(BF16)
    h_out[...] = h + _dot(act, wd_ref[...])


def _layer_resident(stacked, layer):
    zeros = (0,) * (stacked.ndim - 1)
    return pl.BlockSpec((None,) + stacked.shape[1:], lambda *_: (layer,) + zeros,
                        pipeline_mode=pl.Buffered(1))


def _mix_ffn(h, o, mo, w_out, norm_ffn, w_gate_up, w_down, layer):
    t, dm = h.shape
    main_w, mw = o.shape[1], mo.shape[1]
    row = lambda i: (i, 0)
    resident = sum(_nbytes(w.shape[1:], BF16) for w in (w_out, w_gate_up, w_down))
    tile = FFN_TILE
    pipelined = 2 * _nbytes((tile, dm), F32) + _nbytes((tile, main_w + mw), BF16)
    temps = 3 * _nbytes((tile, dm), F32) + 3 * _nbytes((tile, w_down.shape[1]), F32)
    return pl.pallas_call(
        _mix_ffn_kernel,
        out_shape=jax.ShapeDtypeStruct((t, dm), F32),
        grid=(t // tile,),
        in_specs=[
            pl.BlockSpec((tile, dm), row), pl.BlockSpec((tile, main_w), row),
            pl.BlockSpec((tile, mw), row),
            _layer_resident(w_out, layer), _resident((1, dm)), _layer_resident(w_gate_up, layer),
            _layer_resident(w_down, layer),
        ],
        out_specs=pl.BlockSpec((tile, dm), row),
        compiler_params=pltpu.CompilerParams(
            dimension_semantics=("parallel",),
            vmem_limit_bytes=_vmem_limit(pipelined, resident, temps)),
        name="mix_ffn",
    )(h, o, mo, w_out, norm_ffn, w_gate_up, w_down)


def _inproj_b_kernel(*refs, dilations):
    n_groups = len(dilations)
    (x_ref, gain_ref, kvg_ref, w_ref, wkv_ref, qn_ref, kn_ref, cos_ref, sin_ref,
     mqg_ref, mk_ref, mv_ref) = refs[:12]
    q_refs = refs[12:12 + n_groups]
    k_refs = refs[12 + n_groups:12 + 2 * n_groups]
    v_refs = refs[12 + 2 * n_groups:12 + 3 * n_groups]
    mo_ref = refs[12 + 3 * n_groups]
    scratch = refs[13 + 3 * n_groups:]
    q_slabs = dict(zip([d for d in dilations if d != 1], scratch[:-5]))
    k_slab, v_slab, q_tmp, k_tmp, v_tmp = scratch[-5:]
    kv_w = wkv_ref.shape[1] // 2
    bw = (w_ref.shape[1] - mo_ref.shape[1]) // n_groups
    heads = range(bw // HEAD_DIM)
    pair = 2
    q_scale = (HEAD_DIM ** -0.5) * LOG2E

    for row0 in range(0, x_ref.shape[0], ATTN_IN_SUB):
        rows = slice(row0, row0 + ATTN_IN_SUB)
        x = x_ref[rows, :]
        xhat = x * lax.rsqrt(jnp.mean(x * x, axis=-1, keepdims=True) + EPS)
        xn = (xhat * gain_ref[...]).astype(BF16)
        kn = (xhat * kvg_ref[...]).astype(BF16)
        cos, sin = cos_ref[rows, :], sin_ref[rows, :]
        cos_s, sin_s = cos * q_scale, sin * q_scale

        def queries(gi, row0=row0, rows=rows, xn=xn, cos_s=cos_s, sin_s=sin_s):
            dil, q_ref = dilations[gi], q_refs[gi]
            for hd in heads:
                if hd % pair == 0:
                    col0 = gi * bw + hd * HEAD_DIM
                    qs = _dot(xn, w_ref[:, col0:col0 + pair * HEAD_DIM])
                q = _rope(_rms(qs[:, _head_cols(hd % pair)], qn_ref[gi]), cos_s, sin_s)
                if dil == 1:
                    q_ref[hd, rows, :] = q.astype(q_ref.dtype)
                else:
                    q_slabs[dil][hd] = q
            if dil != 1:
                _store_by_residue(q_slabs[dil], q_tmp, [(dil, q_ref)], row0)

        def copies(slab, tmp, out_refs, row0=row0, rows=rows):
            for dil, out_ref in zip(dilations, out_refs):
                if dil == 1:
                    for hd in heads:
                        out_ref[hd, rows, :] = slab[hd].astype(out_ref.dtype)
            _store_by_residue(slab, tmp, [(d, ref) for d, ref in zip(dilations, out_refs) if d != 1], row0)

        order = sorted(range(n_groups), key=lambda gi: -dilations[gi])
        queries(order[0])
        probs = _memory_probs(_dot(xn, w_ref[:, n_groups * bw:]), mqg_ref[...], mk_ref)
        for hd in heads:
            if hd % pair == 0:
                k = _dot(kn, wkv_ref[:, hd * HEAD_DIM:(hd + pair) * HEAD_DIM])
            k_slab[hd] = _rope(_rms(k[:, _head_cols(hd % pair)], kn_ref[...]), cos, sin)
        copies(k_slab, k_tmp, k_refs)
        for gi in order[1:-1]:
            queries(gi)
        _memory_output(probs, mv_ref, mo_ref, rows)
        for hd in heads:
            if hd % pair == 0:
                v = _dot(kn, wkv_ref[:, kv_w + hd * HEAD_DIM:kv_w + (hd + pair) * HEAD_DIM])
            v_slab[hd] = v[:, _head_cols(hd % pair)]
        copies(v_slab, v_tmp, v_refs)
        queries(order[-1])


def _residue_out(bn, seq, width, dil):
    tiles_per_seq = seq // ATTN_IN_TILE
    heads = width // HEAD_DIM
    if dil == 1:
        return (jax.ShapeDtypeStruct((bn, heads, seq, HEAD_DIM), BF16),
                pl.BlockSpec((None, heads, ATTN_IN_TILE, HEAD_DIM),
                             lambda i: (i // tiles_per_seq, 0, i % tiles_per_seq, 0)))
    return (jax.ShapeDtypeStruct((bn, heads, dil, seq // dil, HEAD_DIM), BF16),
            pl.BlockSpec((None, heads, dil, ATTN_IN_TILE // dil, HEAD_DIM),
                         lambda i: (i // tiles_per_seq, 0, 0, i % tiles_per_seq, 0)))


def _inproj_b(h, gain, kv_gain, w, w_kv, q_norm, k_norm, cos2, sin2, mq_gain, mk, mv, layer, bn, seq):
    t, dm = h.shape
    mt, mw = mk.shape[2], mk.shape[3]
    dilations = tuple(d for _, d in DILATED_GROUPS)
    n_groups = len(dilations)
    bw = (w.shape[1] - mw) // n_groups
    kv_w = w_kv.shape[1] // 2
    assert kv_w == bw
    tile = ATTN_IN_TILE
    tiles_per_seq = seq // tile
    row = lambda i: (i, 0)
    pos = lambda i: (i % tiles_per_seq, 0)
    mem = lambda i: (layer, i // tiles_per_seq, 0, 0)
    copies = [_residue_out(bn, seq, bw, dil) for dil in dilations] * 3
    slab = (bw // HEAD_DIM, ATTN_IN_SUB, HEAD_DIM)
    n_slabs = sum(1 for dil in dilations if dil != 1) + 5
    pipelined = (_nbytes((tile, dm), F32) + 3 * n_groups * _nbytes((tile, bw), BF16)
                 + 2 * _nbytes((tile, HEAD_DIM), F32) + _nbytes((tile, mw), BF16)
                 + 2 * _nbytes((mt, mw), BF16))
    resident = _nbytes(w.shape, BF16) + _nbytes(w_kv.shape, BF16) + n_slabs * _nbytes(slab, F32)
    return pl.pallas_call(
        functools.partial(_inproj_b_kernel, dilations=dilations),
        out_shape=tuple([shape for shape, _ in copies] + [jax.ShapeDtypeStruct((t, mw), BF16)]),
        grid=(t // tile,),
        in_specs=[
            pl.BlockSpec((tile, dm), row),
            _resident((1, dm)), _resident((1, dm)),
            _resident(w.shape), _resident(w_kv.shape),
            _resident((n_groups, 1, HEAD_DIM)), _resident((1, HEAD_DIM)),
            pl.BlockSpec((tile, HEAD_DIM), pos), pl.BlockSpec((tile, HEAD_DIM), pos),
            _resident((1, V7X_LANES)),
            pl.BlockSpec((None, None, mt, mw), mem),
            pl.BlockSpec((None, None, mt, mw), mem),
        ],
        out_specs=tuple([spec for _, spec in copies] + [pl.BlockSpec((tile, mw), row)]),
        scratch_shapes=[pltpu.VMEM(slab, F32)] * n_slabs,
        compiler_params=pltpu.CompilerParams(
            dimension_semantics=("parallel",),
            vmem_limit_bytes=_vmem_limit(pipelined, resident, 6 * _nbytes((ATTN_IN_SUB, bw), F32))),
        name="inproj_b",
    )(h, gain, kv_gain, w, w_kv, q_norm.reshape(n_groups, 1, HEAD_DIM), k_norm, cos2, sin2,
      mq_gain, mk, mv)


def _dilated_kernel(*refs, dilations):
    n_groups = len(dilations)
    q_refs = refs[:n_groups]
    k_refs = refs[n_groups:2 * n_groups]
    v_refs = refs[2 * n_groups:3 * n_groups]
    o_ref, og_ref, lse_ref, bias_ref = refs[3 * n_groups:]
    seq = o_ref.shape[0]
    blk = ATTN_BLOCK

    qi = lax.broadcasted_iota(jnp.int32, (blk, 2 * blk), 0)
    kj = lax.broadcasted_iota(jnp.int32, (blk, 2 * blk), 1)
    band = (kj >= qi) & (kj <= qi + blk)
    bias_ref[0] = jnp.where(band & (kj >= blk), 0.0, -jnp.inf)
    bias_ref[1] = jnp.where(band, 0.0, -jnp.inf)

    parked = [gi for gi, dil in enumerate(dilations) if dil != 1]
    (last,) = [gi for gi, dil in enumerate(dilations) if dil == 1]
    for gi in parked + [last]:
        dil = dilations[gi]
        q_ref, k_ref, v_ref = q_refs[gi], k_refs[gi], v_refs[gi]
        n_blocks = seq // (blk * dil)

        def blocks(step, carry, q_ref=q_ref, k_ref=k_ref, v_ref=v_ref, gi=gi, dil=dil, n_blocks=n_blocks):
            idxs = [step * ATTN_UNROLL + u for u in range(ATTN_UNROLL)]
            starts = [pl.multiple_of(idx * blk, blk) for idx in idxs]
            tile_rows = [pl.ds(pl.multiple_of(jnp.maximum(starts[0] + (j - 1) * blk, 0), blk), blk)
                         for j in range(ATTN_UNROLL + 1)]
            k_tiles = [k_ref[rows, :] for rows in tile_rows]
            scores = [lax.dot_general(q_ref[pl.ds(start, blk), :],
                                      jnp.concatenate([k_tiles[u], k_tiles[u + 1]], axis=0), _NT,
                                      preferred_element_type=F32)
                      for u, start in enumerate(starts)]
            soft = []
            for idx, s in zip(idxs, scores):
                n = lax.rem(idx, n_blocks)
                s = s + bias_ref[jnp.minimum(n, 1)]
                m = jnp.max(s, axis=-1, keepdims=True)
                p = jnp.exp2(s - m)
                denom = jnp.sum(p, axis=-1, keepdims=True)
                soft.append((p.astype(BF16), m, denom))
            v_tiles = [v_ref[rows, :] for rows in tile_rows]
            for u, (idx, start, (p, m, denom)) in enumerate(zip(idxs, starts, soft)):
                o = _dot(p, jnp.concatenate([v_tiles[u], v_tiles[u + 1]], axis=0)) / denom
                lse2 = jnp.broadcast_to(m + jnp.log(denom) * LOG2E, (blk, HEAD_DIM))
                if dil != 1:
                    n = lax.rem(idx, n_blocks)
                    out_rows = pl.ds(n * (blk * dil) + lax.div(idx, n_blocks), blk, stride=dil)
                    slot = parked.index(gi)
                    og_ref[slot, out_rows, :] = o
                    lse_ref[slot, out_rows, :] = lse2
                else:
                    rows = pl.ds(start, blk)
                    lses = [lse2] + [lse_ref[slot, rows, :] for slot in range(len(parked))]
                    outs = [o] + [og_ref[slot, rows, :] for slot in range(len(parked))]
                    top = functools.reduce(jnp.maximum, lses)
                    ws = [jnp.exp2(l - top) for l in lses]
                    acc = sum(w * og for w, og in zip(ws, outs))
                    o_ref[rows, :] = (acc / sum(ws)).astype(o_ref.dtype)
            return carry

        lax.fori_loop(0, dil * n_blocks // ATTN_UNROLL, blocks, 0)


def _dilated_attention(qs, ks, vs, bn, seq):
    heads = qs[0].shape[1]
    width = heads * HEAD_DIM
    dilations = tuple(d for _, d in DILATED_GROUPS)
    n_groups = len(dilations)
    as_seq = lambda a: a.reshape(bn, heads, seq, HEAD_DIM)
    spec = pl.BlockSpec((None, None, seq, HEAD_DIM), lambda b, h: (b, h, 0, 0))
    seq_bf16 = _nbytes((seq, HEAD_DIM), BF16)
    seq_f32 = _nbytes((seq, HEAD_DIM), F32)
    n_bias = 2
    scratch = 2 * (n_groups - 1) * seq_f32 + _nbytes((n_bias, ATTN_BLOCK, 2 * ATTN_BLOCK), F32)
    staged = ATTN_UNROLL * (_nbytes((ATTN_BLOCK, 2 * ATTN_BLOCK), F32) + _nbytes((ATTN_BLOCK, 2 * ATTN_BLOCK), BF16)
                            + 2 * _nbytes((ATTN_BLOCK, HEAD_DIM), F32))
    out = pl.pallas_call(
        functools.partial(_dilated_kernel, dilations=dilations),
        out_shape=jax.ShapeDtypeStruct((bn, seq, width), BF16),
        grid=(bn, heads),
        in_specs=[spec] * (3 * n_groups),
        out_specs=pl.BlockSpec((None, seq, HEAD_DIM), lambda b, h: (b, 0, h)),
        scratch_shapes=[
            pltpu.VMEM((n_groups - 1, seq, HEAD_DIM), F32),
            pltpu.VMEM((n_groups - 1, seq, HEAD_DIM), F32),
            pltpu.VMEM((n_bias, ATTN_BLOCK, 2 * ATTN_BLOCK), F32),
        ],
        compiler_params=pltpu.CompilerParams(
            dimension_semantics=("parallel", "parallel"),
            vmem_limit_bytes=_vmem_limit((3 * n_groups + 1) * seq_bf16, scratch, staged)),
        name="dilated_attention",
    )(*[as_seq(a) for a in (*qs, *ks, *vs)])
    return out.reshape(bn * seq, width)


def _rope_tables(seq):
    half = HEAD_DIM // 2
    inv = ROPE_THETA ** (-jnp.arange(half, dtype=F32) / half)
    ang = jnp.arange(seq).astype(F32)[:, None] * inv[None, :]
    cos, sin = jnp.cos(ang), jnp.sin(ang)
    return jnp.concatenate([cos, cos], axis=-1), jnp.concatenate([-sin, sin], axis=-1)


def kernel(x, mem, norm_mix, norm_ffn, a_w_in, a_lb_logits, a_onorm, b_w_in, b_qnorm, kv_norm, w_kv,
           b_knorm, mem_norm, w_mem_kv, mem_qnorm, mem_knorm, w_out, w_gate_up, w_down):
    bn, seq, dm = x.shape
    depth = norm_mix.shape[0]
    n_a = a_w_in.shape[0]
    max_dil = max(d for _, d in DILATED_GROUPS)
    assert all(w == ATTN_BLOCK * d for w, d in DILATED_GROUPS)
    assert seq % (ATTN_BLOCK * max_dil) == 0 and seq % (ATTN_BLOCK * ATTN_UNROLL) == 0
    assert seq % min(HGRN_ROWS, seq) == 0 and HGRN_ROWS % CHUNK == 0 and HGRN_IN_SUB % CHUNK == 0
    assert seq % HGRN_IN_TILE == 0 and HGRN_IN_TILE % HGRN_IN_SUB == 0 and seq % FFN_TILE == 0
    assert seq % ATTN_IN_TILE == 0 and ATTN_IN_TILE % ATTN_IN_SUB == 0 and ATTN_IN_SUB % max_dil == 0
    assert depth - n_a == 1

    bf = lambda a: a.astype(BF16)
    row_vec = lambda a: a.reshape(1, -1)
    pair = lambda a: jnp.concatenate([a, a], axis=-1).reshape(1, V7X_LANES)
    cos2, sin2 = _rope_tables(seq)
    n_groups = len(DILATED_GROUPS)

    w_out_bf, w_gate_up_bf, w_down_bf = bf(w_out), bf(w_gate_up), bf(w_down)
    mk, mv = _mem_kv(mem, mem_norm, bf(w_mem_kv), mem_knorm)
    h = x.reshape(bn * seq, dm)
    for l in range(depth):
        gain = row_vec(norm_mix[l])
        if l < n_a:
            qs, lf, k, v, gate, mo = _inproj_a(h, gain, bf(a_w_in[l]), a_lb_logits, pair(mem_qnorm[l]),
                                               mk, mv, l, seq)
            o = _hgrn2(qs, lf, k, v, gate, row_vec(a_onorm[l]), bn, seq)
        else:
            j = l - n_a
            *copies, mo = _inproj_b(h, gain, row_vec(kv_norm), bf(b_w_in[j]), bf(w_kv), b_qnorm[j],
                                    row_vec(b_knorm), cos2, sin2, pair(mem_qnorm[l]), mk, mv, l, bn, seq)
            q_groups, ks, vs = (copies[i * n_groups:(i + 1) * n_groups] for i in range(3))
            o = _dilated_attention(q_groups, ks, vs, bn, seq)
        h = _mix_ffn(h, o, mo, w_out_bf, row_vec(norm_ffn[l]), w_gate_up_bf, w_down_bf, l)
    return h.reshape(bn, seq, dm)
```
